```python
import math
import jax, jax.numpy as jnp
from jax import lax
import numpy as np

D_MODEL = 1024
BATCH = 8
SEQ = 2048
DEPTH = 4
DEC_BATCH = 32
DEC_SEQ = 8
PAST_LEN = 8192
PAGE_SIZE = 128

N_EVEN = (DEPTH + 1) // 2
N_ODD = DEPTH // 2
H_A = 8
DK_A = 32
DV_A = 64
QK_A = H_A * 2 * DK_A
W_A = H_A * DV_A
A_COLS = 2 * QK_A + W_A
H_B = 8
N_B = 64
W_B = H_B * N_B
LORA_W = 64
LORA_A = 64
LORA_G = 128
B_COLS = 3 * W_B + LORA_W + LORA_A + LORA_G
IN_COLS_EVEN = A_COLS + B_COLS
MIX_EVEN = W_A + W_B
CHUNK = 128
G_C = 8
E_C = 2 * D_MODEL
CG = E_C // G_C
D_FF = 2816
N_EXPERTS = 8
TOP_K = 2
D_FF_E = 1408
Q_BLOCK = 128
RMS_EPS = 1e-6
LN_EPS = 1e-5
LNX_EPS = 64e-5
NEG_INF = -1e30

kernel_name = 'hybrid_diffattn_rwkv7_gmlp_moe_step'


def _rmsnorm(x, g):
    xf = x.astype(jnp.float32)
    y = xf * lax.rsqrt(jnp.mean(xf * xf, axis=-1, keepdims=True) + RMS_EPS)
    return (y * g.astype(jnp.float32)).astype(x.dtype)


def _layernorm(x, g, b, eps):
    xf = x.astype(jnp.float32)
    xc = xf - jnp.mean(xf, axis=-1, keepdims=True)
    y = xc * lax.rsqrt(jnp.mean(xc * xc, axis=-1, keepdims=True) + eps)
    return (y * g.astype(jnp.float32) + b.astype(jnp.float32)).astype(x.dtype)


def _swiglu(h, w_gate, w_up, w_down):
    return (jax.nn.silu(h @ w_gate) * (h @ w_up)) @ w_down


def _moe_swiglu(h, w_r, b_r, e_gate, e_up, e_down):
    logits = (h @ w_r).astype(jnp.float32) + b_r.astype(jnp.float32)
    top_v, top_i = lax.top_k(logits, TOP_K)
    wts = jax.nn.softmax(top_v, axis=-1)
    gate = jnp.sum(jax.nn.one_hot(top_i, N_EXPERTS, dtype=jnp.float32) * wts[..., None], axis=-2)
    gate = gate.astype(h.dtype)
    out = jnp.zeros_like(h)
    for e in range(N_EXPERTS):
        out = out + gate[..., e:e + 1] * _swiglu(h, e_gate[e], e_up[e], e_down[e])
    return out


def _diff_attention(q, k, v, q_pos, k_pos, lam):
    n, t = q.shape[:2]
    slopes = jnp.exp2(-8.0 * jnp.arange(1, H_A + 1, dtype=jnp.float32) / H_A)
    kf = k.astype(jnp.float32)
    vf = v.astype(jnp.float32)
    scale = DK_A ** -0.5

    def block(args):
        qb, qp = args
        s = jnp.einsum('nthmd,nshmd->nmhts', qb.astype(jnp.float32), kf) * scale
        dist = jnp.abs(qp[:, None] - k_pos[None, :]).astype(jnp.float32)
        s = s - slopes[:, None, None] * dist
        s = jnp.where(k_pos[None, :] <= qp[:, None], s, NEG_INF)
        pr = jax.nn.softmax(s, axis=-1)
        pr = pr[:, 0] - lam * pr[:, 1]
        return jnp.einsum('nhts,nshd->nthd', pr, vf)

    if t <= Q_BLOCK:
        return block((q, q_pos))
    nb = -(-t // Q_BLOCK)
    pad = nb * Q_BLOCK - t
    qpad = jnp.pad(q, ((0, 0), (0, pad), (0, 0), (0, 0), (0, 0)))
    ppad = jnp.pad(q_pos, (0, pad), mode='edge')
    qb = jnp.swapaxes(qpad.reshape(n, nb, Q_BLOCK, H_A, 2, DK_A), 0, 1)
    pb = ppad.reshape(nb, Q_BLOCK)
    out = lax.map(block, (qb, pb))
    return jnp.swapaxes(out, 0, 1).reshape(n, nb * Q_BLOCK, H_A, DV_A)[:, :t]


def _rwkv7_scan(r, w, k, v, kk, a, s0):
    def step(s, inp):
        r_t, w_t, k_t, v_t, kk_t, a_t = inp
        sa = jnp.einsum('nhvk,nhk->nhv', s, kk_t)
        s = (s * w_t[:, :, None, :] - sa[..., None] * (kk_t * a_t)[:, :, None, :]
             + v_t[..., None] * k_t[:, :, None, :])
        return s, jnp.einsum('nhvk,nhk->nhv', s, r_t)
    xs = tuple(jnp.moveaxis(z, 1, 0) for z in (r, w, k, v, kk, a))
    s_fin, ys = lax.scan(step, s0, xs)
    return jnp.moveaxis(ys, 0, 1), s_fin


def _even_mixer(h, p, j, layer_idx, past_k, past_v, wkv0, shift0):
    n, t, _ = h.shape
    f32 = jnp.float32
    proj = h @ p['w_in_e'][j]
    pa, pb = proj[..., :A_COLS], proj[..., A_COLS:]
    q = pa[..., :QK_A].reshape(n, t, H_A, 2, DK_A)
    k = pa[..., QK_A:2 * QK_A].reshape(n, t, H_A, 2, DK_A)
    v = pa[..., 2 * QK_A:].reshape(n, t, H_A, DV_A)
    if past_k is None:
        start = 0
        k_all, v_all = k, v
    else:
        start = past_k.shape[1]
        k_all = jnp.concatenate([past_k.reshape(n, start, H_A, 2, DK_A).astype(k.dtype), k], axis=1)
        v_all = jnp.concatenate([past_v.astype(v.dtype), v], axis=1)
    q_pos = start + jnp.arange(t, dtype=jnp.int32)
    k_pos = jnp.arange(start + t, dtype=jnp.int32)
    lam_init = 0.8 - 0.6 * math.exp(-0.3 * layer_idx)
    lam = (jnp.exp(jnp.sum(p['lam_q1'][j].astype(f32) * p['lam_k1'][j].astype(f32)))
           - jnp.exp(jnp.sum(p['lam_q2'][j].astype(f32) * p['lam_k2'][j].astype(f32))) + lam_init)
    oa = _diff_attention(q, k_all, v_all, q_pos, k_pos, lam)
    oa = _rmsnorm(oa, p['subln_g'][j]) * (1.0 - lam_init)
    oa = oa.reshape(n, t, W_A).astype(h.dtype)
    prev = jnp.concatenate([shift0[:, None, :].astype(pb.dtype), pb[:, :-1]], axis=1)
    xb = pb + (prev - pb) * p['mu_b'][j]
    o1, o2, o3 = W_B, 2 * W_B, 3 * W_B
    o4 = o3 + LORA_W
    o5 = o4 + LORA_A
    r = xb[..., :o1].astype(f32)
    kb = xb[..., o1:o2].astype(f32)
    vb = xb[..., o2:o3].astype(f32)
    w_log = (p['w0'][j] + jnp.tanh(xb[..., o3:o4]) @ p['w_up'][j]).astype(f32)
    decay = jnp.exp(-jnp.exp(-jax.nn.softplus(-w_log) - 0.5))
    a = jax.nn.sigmoid((p['a0'][j] + xb[..., o4:o5] @ p['a_up'][j]).astype(f32))
    g = (jax.nn.sigmoid(xb[..., o5:]) @ p['g_up'][j]).astype(f32)

    def heads(z):
        return z.reshape(n, t, H_B, N_B)

    kk = heads(kb * p['k_k'][j].astype(f32))
    kk = kk / jnp.maximum(jnp.linalg.norm(kk, axis=-1, keepdims=True), 1e-12)
    kb = kb * (1.0 + (a - 1.0) * p['k_a'][j].astype(f32))
    rh, kh, vh, ah, dh = heads(r), heads(kb), heads(vb), heads(a), heads(decay)
    yb, s_fin = _rwkv7_scan(rh, dh, kh, vh, kk, ah, wkv0.astype(f32))
    yb = _layernorm(yb, p['lnx_g'][j].reshape(H_B, N_B), p['lnx_b'][j].reshape(H_B, N_B), LNX_EPS)
    yb = yb + jnp.sum(rh * kh * p['r_k'][j].astype(f32), axis=-1, keepdims=True) * vh
    ob = (yb.reshape(n, t, W_B) * g).astype(h.dtype)
    out = jnp.concatenate([oa, ob], axis=-1) @ p['w_out_e'][j]
    return out, k.reshape(n, t, H_A, 2 * DK_A), v, s_fin.astype(h.dtype), pb[:, -1]


def _odd_mixer(h, p, j):
    n, t, _ = h.shape
    uv = jax.nn.gelu(h @ p['w_in_o'][j], approximate=False)
    u, v = uv[..., :E_C], uv[..., E_C:]
    v = _layernorm(v, p['lnv_g'][j], p['lnv_b'][j], LN_EPS)
    length = min(t, CHUNK)
    nc = -(-t // length)
    pad = nc * length - t
    vp = jnp.pad(v, ((0, 0), (0, pad), (0, 0))).reshape(n, nc, length, G_C, CG)
    ws = jnp.tril(p['w_s'][j][:, :length, :length])
    bs = p['b_s'][j][:, :length]
    mixed = jnp.einsum('gts,ncsgd->nctgd', ws, vp) + bs.T[:, :, None]
    mixed = mixed.reshape(n, nc * length, E_C)[:, :t]
    return (u * mixed) @ p['w_out_o'][j], v


def _forward(x, c, p, cache_k, cache_v, page_table, state_wkv, state_shift):
    n = x.shape[0]
    new_k, new_v, new_wkv, new_shift, new_cv = [], [], [], [], []
    for i in range(DEPTH):
        j = i // 2
        mod = jax.nn.silu(c) @ p['w_mod'][i] + p['b_mod'][i]
        sh1, sc1, g1, sh2, sc2, g2 = jnp.split(mod[:, None, :], 6, axis=-1)
        hn = _rmsnorm(x, p['norm1_g'][i]) * (1.0 + sc1) + sh1
        if i % 2 == 0:
            if cache_k is None:
                pk = None
                pv = None
                wkv0 = jnp.zeros((n, H_B, N_B, N_B), x.dtype)
                sh0 = jnp.zeros((n, B_COLS), x.dtype)
            else:
                pk = cache_k[j][page_table].reshape(n, -1, H_A, 2 * DK_A)
                pv = cache_v[j][page_table].reshape(n, -1, H_A, DV_A)
                wkv0 = state_wkv[j]
                sh0 = state_shift[j]
            mix, kn, vn, s_fin, sh_last = _even_mixer(hn, p, j, i, pk, pv, wkv0, sh0)
            new_k.append(kn)
            new_v.append(vn)
            new_wkv.append(s_fin)
            new_shift.append(sh_last)
            x = x + g1 * mix
            hn = _rmsnorm(x, p['norm2_g'][i]) * (1.0 + sc2) + sh2
            ffn = _swiglu(hn, p['ffn_gate'][j], p['ffn_up'][j], p['ffn_down'][j])
        else:
            mix, v_rows = _odd_mixer(hn, p, j)
            new_cv.append(v_rows)
            x = x + g1 * mix
            hn = _rmsnorm(x, p['norm2_g'][i]) * (1.0 + sc2) + sh2
            ffn = _moe_swiglu(hn, p['router_w'][j], p['router_b'][j],
                              p['exp_gate'][j], p['exp_up'][j], p['exp_down'][j])
        x = x + g2 * ffn
    y = _rmsnorm(x, p['final_g'])
    return (y, jnp.stack(new_k), jnp.stack(new_v), jnp.stack(new_wkv),
            jnp.stack(new_shift), jnp.stack(new_cv))


def setup_inputs(seed: int = 0) -> dict:
    key = jax.random.key(seed)
    keys = list(jax.random.split(key, 64))
    f32 = jnp.float32

    def nrm(shape, scale):
        return jax.random.normal(keys.pop(), shape, f32) * scale

    d = D_MODEL
    n_pages = PAST_LEN // PAGE_SIZE
    used = DEC_BATCH * n_pages
    n_pool = used + max(1, used // 4)
    page_table = jax.random.permutation(keys.pop(), n_pool)[:used].reshape(DEC_BATCH, n_pages).astype(jnp.int32)
    inp = {}
    inp['x_prompt'] = nrm((BATCH, SEQ, d), 1.0)
    inp['x_sample'] = nrm((DEC_BATCH, DEC_SEQ, d), 1.0)
    inp['cache_k'] = nrm((N_EVEN, n_pool, PAGE_SIZE, H_A, 2 * DK_A), 1.0)
    inp['cache_v'] = nrm((N_EVEN, n_pool, PAGE_SIZE, H_A, DV_A), 1.0)
    inp['state_wkv'] = nrm((N_EVEN, DEC_BATCH, H_B, N_B, N_B), 0.5)
    inp['state_shift'] = nrm((N_EVEN, DEC_BATCH, B_COLS), 1.0)
    inp['page_table'] = page_table
    inp['c_prompt'] = nrm((BATCH, d), 1.0)
    inp['c_sample'] = nrm((DEC_BATCH, d), 1.0)
    inp['w_mod'] = nrm((DEPTH, d, 6 * d), 0.5 * d ** -0.5)
    inp['b_mod'] = nrm((DEPTH, 6 * d), 0.02)
    inp['norm1_g'] = 1.0 + nrm((DEPTH, d), 0.02)
    inp['norm2_g'] = 1.0 + nrm((DEPTH, d), 0.02)
    inp['final_g'] = 1.0 + nrm((d,), 0.02)
    inp['w_in_e'] = nrm((N_EVEN, d, IN_COLS_EVEN), d ** -0.5)
    inp['w_out_e'] = nrm((N_EVEN, MIX_EVEN, d), MIX_EVEN ** -0.5)
    inp['lam_q1'] = nrm((N_EVEN, DK_A), 0.1)
    inp['lam_k1'] = nrm((N_EVEN, DK_A), 0.1)
    inp['lam_q2'] = nrm((N_EVEN, DK_A), 0.1)
    inp['lam_k2'] = nrm((N_EVEN, DK_A), 0.1)
    inp['subln_g'] = 1.0 + nrm((N_EVEN, DV_A), 0.02)
    inp['mu_b'] = jax.random.uniform(keys.pop(), (N_EVEN, B_COLS), f32)
    inp['w0'] = nrm((N_EVEN, W_B), 1.0) - 1.0
    inp['w_up'] = nrm((N_EVEN, LORA_W, W_B), LORA_W ** -0.5)
    inp['a0'] = nrm((N_EVEN, W_B), 0.5)
    inp['a_up'] = nrm((N_EVEN, LORA_A, W_B), LORA_A ** -0.5)
    inp['g_up'] = nrm((N_EVEN, LORA_G, W_B), LORA_G ** -0.5)
    inp['k_k'] = 0.85 + nrm((N_EVEN, W_B), 0.02)
    inp['k_a'] = 1.0 + nrm((N_EVEN, W_B), 0.02)
    inp['r_k'] = nrm((N_EVEN, H_B, N_B), 0.1)
    inp['lnx_g'] = 1.0 + nrm((N_EVEN, W_B), 0.02)
    inp['lnx_b'] = nrm((N_EVEN, W_B), 0.02)
    inp['w_in_o'] = nrm((N_ODD, d, 2 * E_C), d ** -0.5)
    inp['lnv_g'] = 1.0 + nrm((N_ODD, E_C), 0.02)
    inp['lnv_b'] = nrm((N_ODD, E_C), 0.02)
    inp['w_s'] = nrm((N_ODD, G_C, CHUNK, CHUNK), CHUNK ** -0.5)
    inp['b_s'] = 1.0 + nrm((N_ODD, G_C, CHUNK), 0.02)
    inp['w_out_o'] = nrm((N_ODD, E_C, d), E_C ** -0.5)
    inp['ffn_gate'] = nrm((N_EVEN, d, D_FF), d ** -0.5)
    inp['ffn_up'] = nrm((N_EVEN, d, D_FF), d ** -0.5)
    inp['ffn_down'] = nrm((N_EVEN, D_FF, d), D_FF ** -0.5)
    inp['router_w'] = nrm((N_ODD, d, N_EXPERTS), d ** -0.5)
    inp['router_b'] = nrm((N_ODD, N_EXPERTS), 0.01)
    inp['exp_gate'] = nrm((N_ODD, N_EXPERTS, d, D_FF_E), d ** -0.5)
    inp['exp_up'] = nrm((N_ODD, N_EXPERTS, d, D_FF_E), d ** -0.5)
    inp['exp_down'] = nrm((N_ODD, N_EXPERTS, D_FF_E, d), D_FF_E ** -0.5)
    return inp


def reference(x_prompt, x_sample, cache_k, cache_v, state_wkv, state_shift, page_table,
              c_prompt, c_sample, w_mod, b_mod, norm1_g, norm2_g, final_g, w_in_e, w_out_e,
              lam_q1, lam_k1, lam_q2, lam_k2, subln_g, mu_b, w0, w_up, a0, a_up, g_up,
              k_k, k_a, r_k, lnx_g, lnx_b, w_in_o, lnv_g, lnv_b, w_s, b_s, w_out_o,
              ffn_gate, ffn_up, ffn_down, router_w, router_b, exp_gate, exp_up, exp_down):
    p = dict(w_mod=w_mod, b_mod=b_mod, norm1_g=norm1_g, norm2_g=norm2_g, final_g=final_g,
             w_in_e=w_in_e, w_out_e=w_out_e, lam_q1=lam_q1, lam_k1=lam_k1, lam_q2=lam_q2,
             lam_k2=lam_k2, subln_g=subln_g, mu_b=mu_b, w0=w0, w_up=w_up, a0=a0, a_up=a_up,
             g_up=g_up, k_k=k_k, k_a=k_a, r_k=r_k, lnx_g=lnx_g, lnx_b=lnx_b, w_in_o=w_in_o,
             lnv_g=lnv_g, lnv_b=lnv_b, w_s=w_s, b_s=b_s, w_out_o=w_out_o, ffn_gate=ffn_gate,
             ffn_up=ffn_up, ffn_down=ffn_down, router_w=router_w, router_b=router_b,
             exp_gate=exp_gate, exp_up=exp_up, exp_down=exp_down)
    y_prompt, k_p, v_p, wkv_p, sh_p, _ = _forward(x_prompt, c_prompt, p, None, None, None, None, None)
    y_sample, k_s, v_s, wkv_s, sh_s, cv_s = _forward(x_sample, c_sample, p, cache_k, cache_v,
                                                     page_table, state_wkv, state_shift)
    return (y_prompt, y_sample, k_p, v_p, wkv_p, sh_p, k_s, v_s, wkv_s, sh_s, cv_s)
```

```python
import functools
import math
from typing import NamedTuple

import jax
import jax.numpy as jnp
from jax import lax
from jax.experimental import pallas as pl
from jax.experimental.pallas import tpu as pltpu

F32 = jnp.float32
BF16 = jnp.bfloat16

D_MODEL = 1024
DEPTH = 4
H_A, DK_A, DV_A = 8, 32, 64
QK_A = H_A * 2 * DK_A
W_A = H_A * DV_A
A_COLS = 2 * QK_A + W_A
H_B, N_B = 8, 64
W_B = H_B * N_B
LORA_W, LORA_A, LORA_G = 64, 64, 128
LORA_COLS = LORA_W + LORA_A + LORA_G
B_COLS = 3 * W_B + LORA_COLS
IN_COLS_EVEN = A_COLS + B_COLS
CHUNK, G_C = 128, 8
E_C = 2 * D_MODEL
CG = E_C // G_C
D_FF = 2816
N_EXPERTS, D_FF_E = 8, 1408
PAGE_SIZE = 128
RMS_EPS, LN_EPS, LNX_EPS = 1e-6, 1e-5, 64e-5
NEG_INF = -1e30
ATTN_SCALE = DK_A ** -0.5
DECAY_SCALE = math.exp(-0.5)
INV_SQRT2 = 2.0 ** -0.5

LANES = 128
SUBLANES = 8
MIB = 1 << 20
VMEM_LIMIT = 56 * MIB

ROW_TILE = 256
ATTN_TQ, ATTN_TK = 256, 512
PAGES_PER_STEP = 8
SCAN_TC = 16
CHAINS = 64


class _Group(NamedTuple):
    n: int
    t: int
    tm: int

    @property
    def rows(self):
        return self.n * self.t

    @property
    def tiles(self):
        return self.rows // self.tm

    @property
    def tiles_per_seq(self):
        return max(self.t // self.tm, 1)

    @property
    def mod_rows(self):
        return 1 if self.t >= self.tm else self.tm


def _cparams(*sem):
    return pltpu.CompilerParams(dimension_semantics=sem, vmem_limit_bytes=VMEM_LIMIT)


def _row_spec(tm, cols):
    return pl.BlockSpec((tm, cols), lambda i: (i, 0))


def _full_spec(shape):
    zeros = (0,) * len(shape)
    return pl.BlockSpec(shape, lambda *_: zeros)


def _mod_spec(grp):
    tps = grp.tiles_per_seq
    return pl.BlockSpec((None, grp.mod_rows, D_MODEL), lambda i: (i // tps, 0, 0))


def _expand_mod(m, grp):
    if grp.mod_rows == 1:
        return m[:, None, :]
    return jnp.repeat(m, grp.t, axis=0)[None]


def _dot(a, b):
    return jnp.dot(a, b, preferred_element_type=F32)


def _dot_nt(a, b):
    return lax.dot_general(a, b, (((1,), (1,)), ((), ())), preferred_element_type=F32)


def _split_bf16(x):
    hi = x.astype(BF16)
    lo = (x - hi.astype(F32)).astype(BF16)
    return hi, lo


def _segsum(x, seg):
    hi, lo = _split_bf16(x)
    return _dot(hi, seg) + _dot(lo, seg)


def _silu(x):
    return x * jax.nn.sigmoid(x)


def _mod_kernel(c_ref, w_ref, b_ref, o_ref):
    a_hi, a_lo = _split_bf16(_silu(c_ref[...]))
    w_hi, w_lo = _split_bf16(w_ref[...])
    o_ref[...] = _dot(a_hi, w_hi) + _dot(a_hi, w_lo) + _dot(a_lo, w_hi) + b_ref[...]


def _mod_call(c_all, w_mod, b_mod):
    rows = c_all.shape[0]
    tn = D_MODEL
    return pl.pallas_call(
        _mod_kernel,
        grid=(DEPTH, 6 * D_MODEL // tn),
        in_specs=[
            pl.BlockSpec((rows, D_MODEL), lambda i, j: (0, 0)),
            pl.BlockSpec((None, D_MODEL, tn), lambda i, j: (i, 0, j)),
            pl.BlockSpec((None, 1, tn), lambda i, j: (i, 0, j)),
        ],
        out_specs=pl.BlockSpec((None, rows, tn), lambda i, j: (i, 0, j)),
        out_shape=jax.ShapeDtypeStruct((DEPTH, rows, 6 * D_MODEL), F32),
        compiler_params=_cparams("parallel", "parallel"),
        name="adaln_mod",
    )(c_all, w_mod, b_mod[:, None, :])


def _norm_mod_kernel(x_ref, g_ref, sc_ref, sh_ref, o_ref):
    x = x_ref[...]
    y = x * lax.rsqrt(jnp.mean(x * x, axis=-1, keepdims=True) + RMS_EPS) * g_ref[...]
    o_ref[...] = (y * (1.0 + sc_ref[...]) + sh_ref[...]).astype(o_ref.dtype)


def _norm_mod_call(x, g, sc, sh, grp):
    return pl.pallas_call(
        _norm_mod_kernel,
        grid=(grp.tiles,),
        in_specs=[_row_spec(grp.tm, D_MODEL), _full_spec((1, D_MODEL)), _mod_spec(grp), _mod_spec(grp)],
        out_specs=_row_spec(grp.tm, D_MODEL),
        out_shape=jax.ShapeDtypeStruct((grp.rows, D_MODEL), BF16),
        compiler_params=_cparams("parallel"),
        name="norm_mod",
    )(x, g[None, :], sc, sh)


def _final_norm_kernel(x_ref, g_ref, o_ref):
    x = x_ref[...]
    o_ref[...] = x * lax.rsqrt(jnp.mean(x * x, axis=-1, keepdims=True) + RMS_EPS) * g_ref[...]


def _final_norm_call(x, g, grp):
    return pl.pallas_call(
        _final_norm_kernel,
        grid=(grp.tiles,),
        in_specs=[_row_spec(grp.tm, D_MODEL), _full_spec((1, D_MODEL))],
        out_specs=_row_spec(grp.tm, D_MODEL),
        out_shape=jax.ShapeDtypeStruct((grp.rows, D_MODEL), F32),
        compiler_params=_cparams("parallel"),
        name="final_norm",
    )(x, g[None, :])


def _inproj_even_kernel(a_ref, w_ref, q_ref, k_ref, v_ref, kb_ref, vb_ref, pb_ref):
    a = a_ref[...]
    q_ref[...] = (_dot(a, w_ref[:, :QK_A]) * ATTN_SCALE).astype(BF16)
    k = _dot(a, w_ref[:, QK_A:2 * QK_A])
    k_ref[...] = k
    kb_ref[...] = k.astype(BF16)
    v = _dot(a, w_ref[:, 2 * QK_A:A_COLS])
    v_ref[...] = v
    vb_ref[...] = v.astype(BF16)
    half = B_COLS // 2
    pb_ref[:, :half] = _dot(a, w_ref[:, A_COLS:A_COLS + half])
    pb_ref[:, half:] = _dot(a, w_ref[:, A_COLS + half:])


def _inproj_even_call(hn, w, grp):
    tm, rows = grp.tm, grp.rows
    sds = jax.ShapeDtypeStruct
    return pl.pallas_call(
        _inproj_even_kernel,
        grid=(grp.tiles,),
        in_specs=[_row_spec(tm, D_MODEL), _full_spec((D_MODEL, IN_COLS_EVEN))],
        out_specs=[_row_spec(tm, QK_A), _row_spec(tm, QK_A), _row_spec(tm, W_A),
                   _row_spec(tm, QK_A), _row_spec(tm, W_A), _row_spec(tm, B_COLS)],
        out_shape=[sds((rows, QK_A), BF16), sds((rows, QK_A), F32), sds((rows, W_A), F32),
                   sds((rows, QK_A), BF16), sds((rows, W_A), BF16), sds((rows, B_COLS), F32)],
        compiler_params=_cparams("parallel"),
        name="inproj_even",
    )(hn, w)


def _lambda(lq_ref, lk_ref, lam_init):
    t = jnp.sum(lq_ref[...] * lk_ref[...], axis=1, keepdims=True)
    e = jnp.exp(t)
    return e[0:1] - e[1:2] + lam_init


def _online_softmax_update(s, m_ref, l_ref, acc_ref, v):
    m_old = m_ref[...]
    m_new = jnp.maximum(m_old, jnp.max(s, axis=1, keepdims=True))
    alpha = jnp.exp(m_old - m_new)
    p = jnp.exp(s - m_new)
    l_ref[...] = alpha * l_ref[...] + jnp.sum(p, axis=1, keepdims=True)
    acc_ref[...] = alpha * acc_ref[...] + _dot(p.astype(BF16), v)
    m_ref[...] = m_new


def _attn_prompt_kernel(slopes_ref, lq_ref, lk_ref, g_ref, q_ref, k_ref, v_ref, o_ref,
                        m_scr, l_scr, acc_scr, *, tq, tk, lam_init):
    hp = pl.program_id(1)
    qi = pl.program_id(2)
    lane = lax.broadcasted_iota(jnp.int32, (1, LANES), 1)
    q = q_ref[...]
    q_maps = [jnp.where(lane // DK_A == i, q, jnp.zeros_like(q)) for i in range(4)]
    m_scr[...] = jnp.full(m_scr.shape, NEG_INF, F32)
    l_scr[...] = jnp.zeros(l_scr.shape, F32)
    acc_scr[...] = jnp.zeros(acc_scr.shape, F32)
    col = lax.broadcasted_iota(jnp.int32, (1, tk), 1)
    row = lax.broadcasted_iota(jnp.int32, (tq, 1), 0)
    q0 = qi * tq

    def tile(kt, masked):
        k0 = pl.multiple_of(kt * tk, tk)
        k = k_ref[pl.ds(k0, tk), :]
        v = v_ref[pl.ds(k0, tk), :]
        v_heads = [jnp.where(lane // DV_A == h, v, jnp.zeros_like(v)) for h in range(2)]
        k_rel = (k0 - q0 + col).astype(F32)
        if masked:
            keep = (k0 + col) <= (q0 + row)
        for i in range(4):
            h = i // 2
            s = _dot_nt(q_maps[i], k) + slopes_ref[2 * hp + h] * k_rel
            if masked:
                s = jnp.where(keep, s, NEG_INF)
            _online_softmax_update(s, m_scr.at[i], l_scr.at[i], acc_scr.at[i], v_heads[h])

    n_full = q0 // tk

    def body(kt, carry):
        tile(kt, False)
        return carry

    lax.fori_loop(0, n_full, body, 0)
    tile(n_full, True)

    lam = _lambda(lq_ref, lk_ref, lam_init)
    o = jnp.zeros((tq, LANES), F32)
    for h in range(2):
        oh = acc_scr[2 * h] / l_scr[2 * h] - lam * (acc_scr[2 * h + 1] / l_scr[2 * h + 1])
        in_head = lane // DV_A == h
        ms = jnp.sum(jnp.where(in_head, oh * oh, 0.0), axis=1, keepdims=True) * (1.0 / DV_A)
        o = o + jnp.where(in_head, oh * lax.rsqrt(ms + RMS_EPS), 0.0)
    o_ref[...] = (o * g_ref[...] * (1.0 - lam_init)).astype(o_ref.dtype)


def _attn_prompt_call(q, k, v, slopes, lq, lk, g2, n, t, lam_init):
    tq, tk = ATTN_TQ, min(ATTN_TK, t)
    tq = min(tq, tk)
    kern = functools.partial(_attn_prompt_kernel, tq=tq, tk=tk, lam_init=lam_init)
    qspec = pl.BlockSpec((None, tq, LANES), lambda b, hp, qi: (b, qi, hp))
    kvspec = pl.BlockSpec((None, t, LANES), lambda b, hp, qi: (b, 0, hp))
    return pl.pallas_call(
        kern,
        grid=(n, H_A // 2, t // tq),
        in_specs=[pl.BlockSpec(memory_space=pltpu.SMEM), _full_spec((2, DK_A)), _full_spec((2, DK_A)),
                  _full_spec((1, LANES)), qspec, kvspec, kvspec],
        out_specs=qspec,
        out_shape=jax.ShapeDtypeStruct((n, t, W_A), BF16),
        scratch_shapes=[pltpu.VMEM((4, tq, 1), F32), pltpu.VMEM((4, tq, 1), F32),
                        pltpu.VMEM((4, tq, LANES), F32)],
        compiler_params=_cparams("parallel", "parallel", "parallel"),
        name="diff_attn_prompt",
    )(slopes, lq, lk, g2, q.reshape(n, t, QK_A), k.reshape(n, t, QK_A), v.reshape(n, t, W_A))


def _attn_sample_kernel(pt_ref, lq_ref, lk_ref, g_ref, q_ref, kn_ref, vn_ref, *rest,
                        pages, steps, past_len, t_new, lam_init):
    k_pages, v_pages = rest[:pages], rest[pages:2 * pages]
    o_ref, m_scr, l_scr, acc_scr = rest[2 * pages:]
    step = pl.program_id(1)
    n_rows = 2 * H_A * t_new
    row = lax.broadcasted_iota(jnp.int32, (n_rows, 1), 0)
    lane = lax.broadcasted_iota(jnp.int32, (1, QK_A), 1)
    col = lax.broadcasted_iota(jnp.int32, (1, PAGE_SIZE), 1)
    q_rep = jnp.concatenate([q_ref[...].astype(F32)] * (2 * H_A), axis=0)
    q_big = jnp.where(row // t_new == lane // DK_A, q_rep, 0.0).astype(BF16)
    slope = jnp.exp2(-(row // (2 * t_new) + 1).astype(F32))
    q_pos = past_len + row % t_new

    @pl.when(step == 0)
    def _():
        m_scr[...] = jnp.full(m_scr.shape, NEG_INF, F32)
        l_scr[...] = jnp.zeros(l_scr.shape, F32)
        acc_scr[...] = jnp.zeros(acc_scr.shape, F32)

    def page(k, v, k_pos, masked):
        s = _dot_nt(q_big, k) - slope * (q_pos - k_pos).astype(F32)
        if masked:
            s = jnp.where(k_pos <= q_pos, s, NEG_INF)
        _online_softmax_update(s, m_scr, l_scr, acc_scr, v)

    for i in range(pages):
        k_pos = (step * pages + i) * PAGE_SIZE + col
        page(k_pages[i][...].astype(BF16), v_pages[i][...].astype(BF16), k_pos, False)

    @pl.when(step == steps - 1)
    def _():
        pad = jnp.zeros((PAGE_SIZE - t_new, QK_A), F32)
        k_new = jnp.concatenate([kn_ref[...].astype(F32), pad], axis=0).astype(BF16)
        v_new = jnp.concatenate([vn_ref[...].astype(F32), pad], axis=0).astype(BF16)
        page(k_new, v_new, past_len + col, True)
        lam = _lambda(lq_ref, lk_ref, lam_init)
        a = acc_scr[...] / l_scr[...]
        o = jnp.zeros((t_new, W_A), F32)
        for h in range(H_A):
            r0 = 2 * t_new * h
            oh = a[r0:r0 + t_new] - lam * a[r0 + t_new:r0 + 2 * t_new]
            in_head = lane // DV_A == h
            ms = jnp.sum(jnp.where(in_head, oh * oh, 0.0), axis=1, keepdims=True) * (1.0 / DV_A)
            o = o + jnp.where(in_head, oh * lax.rsqrt(ms + RMS_EPS), 0.0)
        o_ref[...] = (o * g_ref[...] * (1.0 - lam_init)).astype(o_ref.dtype)


def _attn_sample_call(q, k_new, v_new, cache_k, cache_v, page_table, j, lq, lk, g8, n, t, lam_init):
    n_pages = page_table.shape[1]
    pages = PAGES_PER_STEP
    steps = n_pages // pages
    n_pool = cache_k.shape[1]
    ck = cache_k.reshape(cache_k.shape[0], n_pool, PAGE_SIZE, QK_A)
    cv = cache_v.reshape(cache_v.shape[0], n_pool, PAGE_SIZE, W_A)
    kern = functools.partial(_attn_sample_kernel, pages=pages, steps=steps,
                             past_len=n_pages * PAGE_SIZE, t_new=t, lam_init=lam_init)

    def page_spec(i):
        return pl.BlockSpec((None, None, PAGE_SIZE, QK_A),
                            lambda b, s, pt: (j, pt[b, s * pages + i], 0, 0))

    seq_spec = pl.BlockSpec((None, t, QK_A), lambda b, s, pt: (b, 0, 0))
    small = lambda shape: pl.BlockSpec(shape, lambda b, s, pt: (0,) * len(shape))
    n_rows = 2 * H_A * t
    grid_spec = pltpu.PrefetchScalarGridSpec(
        num_scalar_prefetch=1,
        grid=(n, steps),
        in_specs=[small((2, DK_A)), small((2, DK_A)), small((1, W_A)), seq_spec, seq_spec, seq_spec]
                 + [page_spec(i) for i in range(pages)] * 2,
        out_specs=seq_spec,
        scratch_shapes=[pltpu.VMEM((n_rows, 1), F32), pltpu.VMEM((n_rows, 1), F32),
                        pltpu.VMEM((n_rows, W_A), F32)],
    )
    return pl.pallas_call(
        kern,
        grid_spec=grid_spec,
        out_shape=jax.ShapeDtypeStruct((n, t, W_A), BF16),
        compiler_params=_cparams("parallel", "arbitrary"),
        name="diff_attn_paged",
    )(page_table, lq, lk, g8, q.reshape(n, t, QK_A), k_new.reshape(n, t, QK_A),
      v_new.reshape(n, t, W_A), *([ck] * pages), *([cv] * pages))


def _rwkv_prep_kernel(pb_ref, first_ref, mu_ref, wl_ref, w0_ref, a0_ref, kk_w_ref, ka_w_ref, rk_ref,
                      seg_ref, r_o, w_o, k_o, v_o, kk_o, b_o, bonus_o, g_o, *, period):
    pb = pb_ref[...]
    tm = pb.shape[0]
    row = lax.broadcasted_iota(jnp.int32, (tm, 1), 0)
    prev = jnp.where(row % period == 0, first_ref[...], pltpu.roll(pb, 1, 0))
    xb = pb + (prev - pb) * mu_ref[...]
    r = xb[:, :W_B]
    kb = xb[:, W_B:2 * W_B]
    vb = xb[:, 2 * W_B:3 * W_B]
    z = xb[:, 3 * W_B:]
    ll = lax.broadcasted_iota(jnp.int32, (1, LORA_COLS), 1)
    zz = jnp.where(ll < LORA_W, jnp.tanh(z), jnp.where(ll < LORA_W + LORA_A, z, jax.nn.sigmoid(z)))
    lo = _dot(zz.astype(BF16), wl_ref[...])
    decay = jnp.exp(-DECAY_SCALE * jax.nn.sigmoid(w0_ref[...] + lo[:, :W_B]))
    a = jax.nn.sigmoid(a0_ref[...] + lo[:, W_B:2 * W_B])
    seg = seg_ref[...]
    kk = kb * kk_w_ref[...]
    kk = kk / jnp.maximum(jnp.sqrt(_segsum(kk * kk, seg)), 1e-12)
    k_adj = kb * (1.0 + (a - 1.0) * ka_w_ref[...])
    r_o[...] = r
    w_o[...] = decay
    k_o[...] = k_adj
    v_o[...] = vb
    kk_o[...] = kk
    b_o[...] = kk * a
    bonus_o[...] = _segsum(r * k_adj * rk_ref[...], seg) * vb
    g_o[...] = lo[:, 2 * W_B:]


def _rwkv_prep_call(pb, first, p, j, seg, grp):
    tm, rows = grp.tm, grp.rows
    period = tm if grp.mod_rows == 1 else grp.t
    first_rows = first.shape[1]
    vec = lambda x: x[j].reshape(1, -1)
    wl = jnp.zeros((LORA_COLS, 3 * W_B), F32)
    wl = wl.at[:LORA_W, :W_B].set(p['w_up'][j])
    wl = wl.at[LORA_W:LORA_W + LORA_A, W_B:2 * W_B].set(p['a_up'][j])
    wl = wl.at[LORA_W + LORA_A:, 2 * W_B:].set(p['g_up'][j])
    outs = [jax.ShapeDtypeStruct((rows, W_B), F32)] * 8
    return pl.pallas_call(
        functools.partial(_rwkv_prep_kernel, period=period),
        grid=(grp.tiles,),
        in_specs=[_row_spec(tm, B_COLS),
                  pl.BlockSpec((None, first_rows, B_COLS), lambda i: (i, 0, 0)),
                  _full_spec((1, B_COLS)), _full_spec((LORA_COLS, 3 * W_B)),
                  _full_spec((1, W_B)), _full_spec((1, W_B)), _full_spec((1, W_B)),
                  _full_spec((1, W_B)), _full_spec((1, W_B)), _full_spec((W_B, W_B))],
        out_specs=[_row_spec(tm, W_B)] * 8,
        out_shape=outs,
        compiler_params=_cparams("parallel"),
        name="rwkv_prep",
    )(pb, first, vec(p['mu_b']), wl.astype(BF16), vec(p['w0']), vec(p['a0']), vec(p['k_k']),
      vec(p['k_a']), vec(p['r_k']), seg)


def _rwkv_scan_kernel(kk_ref, w_ref, b_ref, k_ref, r_ref, v_ref, s0_ref, y_ref, s_ref, *, tc):
    @pl.when(pl.program_id(1) == 0)
    def _():
        s_ref[...] = s0_ref[...]

    n_acc = 4

    def step(t, carry):
        parts = [None] * n_acc
        for k in range(N_B):
            term = s_ref[k] * kk_ref[t, pl.ds(k, 1), :]
            parts[k % n_acc] = term if parts[k % n_acc] is None else parts[k % n_acc] + term
        sa = (parts[0] + parts[1]) + (parts[2] + parts[3])
        v_t = v_ref[t]
        parts = [None] * n_acc
        for k in range(N_B):
            s_new = (s_ref[k] * w_ref[t, pl.ds(k, 1), :] - sa * b_ref[t, pl.ds(k, 1), :]
                     + v_t * k_ref[t, pl.ds(k, 1), :])
            s_ref[k] = s_new
            term = s_new * r_ref[t, pl.ds(k, 1), :]
            parts[k % n_acc] = term if parts[k % n_acc] is None else parts[k % n_acc] + term
        y_ref[t] = (parts[0] + parts[1]) + (parts[2] + parts[3])
        return carry

    lax.fori_loop(0, tc, step, 0)


def _to_chain_layout(x, n, t):
    groups = n * H_B // CHAINS
    x = x.reshape(n, t, H_B, N_B).transpose(1, 3, 0, 2).reshape(t, N_B, groups, CHAINS)
    x = x.transpose(2, 0, 1, 3)
    return jnp.concatenate([x, x], axis=-1)


def _value_to_chain_layout(v, n, t):
    groups = n * H_B // CHAINS
    v = v.reshape(n, t, H_B, N_B).transpose(1, 3, 0, 2).reshape(t, 2, N_B // 2, groups, CHAINS)
    return v.transpose(3, 0, 2, 1, 4).reshape(groups, t, N_B // 2, 2 * CHAINS)


def _value_from_chain_layout(y, n, t):
    groups = n * H_B // CHAINS
    y = y.reshape(groups, t, N_B // 2, 2, CHAINS).transpose(1, 3, 2, 0, 4)
    y = y.reshape(t, N_B, n, H_B).transpose(2, 0, 3, 1)
    return y.reshape(n * t, W_B)


def _state_to_chain_layout(s, n):
    groups = n * H_B // CHAINS
    s = s.reshape(groups, CHAINS, 2, N_B // 2, N_B)
    return s.transpose(0, 4, 3, 2, 1).reshape(groups, N_B, N_B // 2, 2 * CHAINS)


def _state_from_chain_layout(s, n):
    groups = n * H_B // CHAINS
    s = s.reshape(groups, N_B, N_B // 2, 2, CHAINS).transpose(0, 4, 3, 2, 1)
    return s.reshape(n, H_B, N_B, N_B)


def _rwkv_scan_call(kk, w, b, k, r, v, s0, n, t):
    groups = n * H_B // CHAINS
    tc = min(SCAN_TC, t)
    vec_spec = pl.BlockSpec((None, tc, N_B, 2 * CHAINS), lambda g, i: (g, i, 0, 0))
    val_spec = pl.BlockSpec((None, tc, N_B // 2, 2 * CHAINS), lambda g, i: (g, i, 0, 0))
    st_spec = pl.BlockSpec((None, N_B, N_B // 2, 2 * CHAINS), lambda g, i: (g, 0, 0, 0))
    y, s_fin = pl.pallas_call(
        functools.partial(_rwkv_scan_kernel, tc=tc),
        grid=(groups, t // tc),
        in_specs=[vec_spec] * 5 + [val_spec, st_spec],
        out_specs=[val_spec, st_spec],
        out_shape=[jax.ShapeDtypeStruct((groups, t, N_B // 2, 2 * CHAINS), F32),
                   jax.ShapeDtypeStruct((groups, N_B, N_B // 2, 2 * CHAINS), F32)],
        compiler_params=_cparams("parallel", "arbitrary"),
        name="rwkv_scan",
    )(*[_to_chain_layout(x, n, t) for x in (kk, w, b, k, r)], _value_to_chain_layout(v, n, t),
      _state_to_chain_layout(s0, n))
    return _value_from_chain_layout(y, n, t), _state_from_chain_layout(s_fin, n)


def _rwkv_post_kernel(y_ref, bonus_ref, g_ref, lg_ref, lb_ref, seg_ref, o_ref):
    y = y_ref[...]
    seg = seg_ref[...]
    yc = y - _segsum(y, seg) * (1.0 / N_B)
    var = _segsum(yc * yc, seg) * (1.0 / N_B)
    yn = yc * lax.rsqrt(var + LNX_EPS) * lg_ref[...] + lb_ref[...]
    o_ref[...] = ((yn + bonus_ref[...]) * g_ref[...]).astype(o_ref.dtype)


def _rwkv_post_call(y, bonus, g, lg, lb, seg, grp):
    tm = grp.tm
    return pl.pallas_call(
        _rwkv_post_kernel,
        grid=(grp.tiles,),
        in_specs=[_row_spec(tm, W_B)] * 3 + [_full_spec((1, W_B)), _full_spec((1, W_B)),
                                              _full_spec((W_B, W_B))],
        out_specs=_row_spec(tm, W_B),
        out_shape=jax.ShapeDtypeStruct((grp.rows, W_B), BF16),
        compiler_params=_cparams("parallel"),
        name="rwkv_post",
    )(y, bonus, g, lg[None, :], lb[None, :], seg)


def _outproj_even_kernel(oa_ref, ob_ref, w_ref, x_ref, g_ref, o_ref):
    mix = _dot(oa_ref[...], w_ref[:W_A, :]) + _dot(ob_ref[...], w_ref[W_A:, :])
    o_ref[...] = x_ref[...] + g_ref[...] * mix


def _outproj_even_call(oa, ob, w, x, gate, grp):
    tm = grp.tm
    return pl.pallas_call(
        _outproj_even_kernel,
        grid=(grp.tiles,),
        in_specs=[_row_spec(tm, W_A), _row_spec(tm, W_B), _full_spec((W_A + W_B, D_MODEL)),
                  _row_spec(tm, D_MODEL), _mod_spec(grp)],
        out_specs=_row_spec(tm, D_MODEL),
        out_shape=jax.ShapeDtypeStruct((grp.rows, D_MODEL), F32),
        compiler_params=_cparams("parallel"),
        name="outproj_even",
    )(oa, ob, w, x, gate)


def _swiglu_up_kernel(a_ref, wg_ref, wu_ref, o_ref):
    a = a_ref[...]
    o_ref[...] = (_silu(_dot(a, wg_ref[...])) * _dot(a, wu_ref[...])).astype(o_ref.dtype)


def _swiglu_up_call(hn, wg, wu, grp):
    tm = grp.tm
    tn = D_FF // 2
    return pl.pallas_call(
        _swiglu_up_kernel,
        grid=(D_FF // tn, grp.tiles),
        in_specs=[pl.BlockSpec((tm, D_MODEL), lambda c, i: (i, 0)),
                  pl.BlockSpec((D_MODEL, tn), lambda c, i: (0, c)),
                  pl.BlockSpec((D_MODEL, tn), lambda c, i: (0, c))],
        out_specs=pl.BlockSpec((tm, tn), lambda c, i: (i, c)),
        out_shape=jax.ShapeDtypeStruct((grp.rows, D_FF), BF16),
        compiler_params=_cparams("parallel", "parallel"),
        name="swiglu_up",
    )(hn, wg, wu)


def _down_res_kernel(h_ref, w_ref, x_ref, g_ref, o_ref):
    o_ref[...] = x_ref[...] + g_ref[...] * _dot(h_ref[...], w_ref[...])


def _down_res_call(h, w, x, gate, grp):
    tm = grp.tm
    kdim = h.shape[1]
    return pl.pallas_call(
        _down_res_kernel,
        grid=(grp.tiles,),
        in_specs=[_row_spec(tm, kdim), _full_spec((kdim, D_MODEL)), _row_spec(tm, D_MODEL), _mod_spec(grp)],
        out_specs=_row_spec(tm, D_MODEL),
        out_shape=jax.ShapeDtypeStruct((grp.rows, D_MODEL), F32),
        compiler_params=_cparams("parallel"),
        name="down_res",
    )(h, w, x, gate)


def _gelu(x):
    return 0.5 * x * (1.0 + lax.erf(x * INV_SQRT2))


def _gmlp_u_kernel(a_ref, w_ref, o_ref):
    o_ref[...] = _gelu(_dot(a_ref[...], w_ref[...]))


def _gmlp_v_kernel(a_ref, w_ref, lg_ref, lb_ref, vb_ref, vf_ref):
    v = _gelu(_dot(a_ref[...], w_ref[...]))
    vc = v - jnp.mean(v, axis=-1, keepdims=True)
    vn = vc * lax.rsqrt(jnp.mean(vc * vc, axis=-1, keepdims=True) + LN_EPS) * lg_ref[...] + lb_ref[...]
    vb_ref[...] = vn.astype(BF16)
    vf_ref[...] = vn


def _gmlp_in_call(hn, w, lg, lb, grp):
    tm, rows = grp.tm, grp.rows
    wspec = lambda half: pl.BlockSpec((D_MODEL, E_C), lambda i: (0, half))
    u = pl.pallas_call(
        _gmlp_u_kernel,
        grid=(grp.tiles,),
        in_specs=[_row_spec(tm, D_MODEL), wspec(0)],
        out_specs=_row_spec(tm, E_C),
        out_shape=jax.ShapeDtypeStruct((rows, E_C), F32),
        compiler_params=_cparams("parallel"),
        name="gmlp_u",
    )(hn, w)
    vb, vf = pl.pallas_call(
        _gmlp_v_kernel,
        grid=(grp.tiles,),
        in_specs=[_row_spec(tm, D_MODEL), wspec(1), _full_spec((1, E_C)), _full_spec((1, E_C))],
        out_specs=[_row_spec(tm, E_C), _row_spec(tm, E_C)],
        out_shape=[jax.ShapeDtypeStruct((rows, E_C), BF16), jax.ShapeDtypeStruct((rows, E_C), F32)],
        compiler_params=_cparams("parallel"),
        name="gmlp_v",
    )(hn, w, lg[None, :], lb[None, :])
    return u, vb, vf


def _gmlp_out_kernel(u_ref, v_ref, ws_ref, bs_ref, wo_ref, x_ref, g_ref, o_ref, z_scr, *, lm):
    tm = u_ref.shape[0]
    for c in range(tm // lm):
        r0 = c * lm
        for gi in range(G_C):
            c0 = gi * CG
            mixed = _dot(ws_ref[gi], v_ref[r0:r0 + lm, c0:c0 + CG]) + bs_ref[:, gi:gi + 1]
            z_scr[r0:r0 + lm, c0:c0 + CG] = (u_ref[r0:r0 + lm, c0:c0 + CG] * mixed).astype(BF16)
    o_ref[...] = x_ref[...] + g_ref[...] * _dot(z_scr[...], wo_ref[...])


def _gmlp_out_call(u, vb, ws, bs, wo, x, gate, grp):
    tm = grp.tm
    lm = ws.shape[1]
    return pl.pallas_call(
        functools.partial(_gmlp_out_kernel, lm=lm),
        grid=(grp.tiles,),
        in_specs=[_row_spec(tm, E_C), _row_spec(tm, E_C), _full_spec((G_C, lm, lm)), _full_spec((lm, G_C)),
                  _full_spec((E_C, D_MODEL)), _row_spec(tm, D_MODEL), _mod_spec(grp)],
        out_specs=_row_spec(tm, D_MODEL),
        out_shape=jax.ShapeDtypeStruct((grp.rows, D_MODEL), F32),
        scratch_shapes=[pltpu.VMEM((tm, E_C), BF16)],
        compiler_params=_cparams("parallel"),
        name="gmlp_out",
    )(u, vb, ws, bs, wo, x, gate)


def _spatial_weights(w_s, b_s, grp):
    length = min(grp.t, CHUNK)
    ws = jnp.tril(w_s[:, :length, :length])
    bs = b_s[:, :length]
    if length < CHUNK:
        reps = grp.tm // length
        ws = jnp.einsum('ab,gij->gaibj', jnp.eye(reps, dtype=ws.dtype), ws).reshape(
            G_C, reps * length, reps * length)
        bs = jnp.tile(bs, (1, reps))
    return ws.astype(BF16), bs.T


def _moe_kernel(a_ref, wr_ref, br_ref, wg_ref, wu_ref, wd_ref, x_ref, g_ref, o_ref, gate_scr, acc_scr):
    e = pl.program_id(1)
    lane = lax.broadcasted_iota(jnp.int32, (1, LANES), 1)
    a = a_ref[...]

    @pl.when(e == 0)
    def _():
        logits = _dot(a, wr_ref[...]) + br_ref[...]
        m1 = jnp.max(logits, axis=1, keepdims=True)
        i1 = jnp.min(jnp.where(logits == m1, lane, LANES), axis=1, keepdims=True)
        rest = jnp.where(lane == i1, -jnp.inf, logits)
        m2 = jnp.max(rest, axis=1, keepdims=True)
        i2 = jnp.min(jnp.where(rest == m2, lane, LANES), axis=1, keepdims=True)
        e2 = jnp.exp(m2 - m1)
        gate_scr[...] = jnp.where(lane == i1, 1.0 / (1.0 + e2), jnp.where(lane == i2, e2 / (1.0 + e2), 0.0))
        acc_scr[...] = jnp.zeros(acc_scr.shape, F32)

    h = (_silu(_dot(a, wg_ref[...])) * _dot(a, wu_ref[...])).astype(BF16)
    y = _dot(h, wd_ref[...])
    gate_e = jnp.sum(jnp.where(lane == e, gate_scr[...], 0.0), axis=1, keepdims=True)
    acc_scr[...] += gate_e * y

    @pl.when(e == N_EXPERTS - 1)
    def _():
        o_ref[...] = x_ref[...] + g_ref[...] * acc_scr[...]


def _moe_call(hn, wr, br, wg, wu, wd, j, x, gate, grp):
    tm = grp.tm
    tps = grp.tiles_per_seq
    wr_pad = jnp.zeros((D_MODEL, LANES), F32).at[:, :N_EXPERTS].set(wr).astype(BF16)
    br_pad = jnp.full((1, LANES), -jnp.inf, F32).at[0, :N_EXPERTS].set(br)
    row = lambda cols: pl.BlockSpec((tm, cols), lambda i, e: (i, 0))
    const = lambda shape: pl.BlockSpec(shape, lambda i, e: (0,) * len(shape))
    return pl.pallas_call(
        _moe_kernel,
        grid=(grp.tiles, N_EXPERTS),
        in_specs=[row(D_MODEL), const((D_MODEL, LANES)), const((1, LANES)),
                  pl.BlockSpec((None, None, D_MODEL, D_FF_E), lambda i, e: (j, e, 0, 0)),
                  pl.BlockSpec((None, None, D_MODEL, D_FF_E), lambda i, e: (j, e, 0, 0)),
                  pl.BlockSpec((None, None, D_FF_E, D_MODEL), lambda i, e: (j, e, 0, 0)),
                  row(D_MODEL),
                  pl.BlockSpec((None, grp.mod_rows, D_MODEL), lambda i, e: (i // tps, 0, 0))],
        out_specs=row(D_MODEL),
        out_shape=jax.ShapeDtypeStruct((grp.rows, D_MODEL), F32),
        scratch_shapes=[pltpu.VMEM((tm, LANES), F32), pltpu.VMEM((tm, D_MODEL), F32)],
        compiler_params=_cparams("parallel", "arbitrary"),
        name="moe_experts",
    )(hn, wr_pad, br_pad, wg, wu, wd, x, gate)


def _forward(x, mods, p, wb, grp, cache):
    n, t = grp.n, grp.t
    seg = jnp.kron(jnp.eye(H_B, dtype=F32), jnp.ones((N_B, N_B), F32)).astype(BF16)
    slopes = jnp.exp2(-8.0 * jnp.arange(1, H_A + 1, dtype=F32) / H_A)
    new_k, new_v, new_wkv, new_shift, new_cv = [], [], [], [], []
    for i in range(DEPTH):
        j = i // 2
        sh1, sc1, g1, sh2, sc2, g2 = [_expand_mod(m, grp) for m in mods[i]]
        hn = _norm_mod_call(x, p['norm1_g'][i], sc1, sh1, grp)
        if i % 2 == 0:
            q, k, v, kb, vb, pb = _inproj_even_call(hn, wb['w_in_e'][j], grp)
            lam_init = 0.8 - 0.6 * math.exp(-0.3 * i)
            lq = jnp.stack([p['lam_q1'][j], p['lam_q2'][j]])
            lk = jnp.stack([p['lam_k1'][j], p['lam_k2'][j]])
            if cache is None:
                g2h = jnp.tile(p['subln_g'][j], 2)[None, :]
                oa = _attn_prompt_call(q, kb, vb, slopes, lq, lk, g2h, n, t, lam_init)
                wkv0 = jnp.zeros((n, H_B, N_B, N_B), F32)
                last = pb.reshape(grp.tiles, grp.tm, B_COLS)[:, -1, :]
                first = jnp.concatenate([jnp.zeros((1, B_COLS), F32), last[:-1]], axis=0)
                starts = (jnp.arange(grp.tiles) % grp.tiles_per_seq == 0)[:, None]
                first = jnp.where(starts, 0.0, first)[:, None, :]
            else:
                cache_k, cache_v, page_table, state_wkv, state_shift = cache
                g8h = jnp.tile(p['subln_g'][j], H_A)[None, :]
                oa = _attn_sample_call(q, kb, vb, cache_k, cache_v, page_table, j, lq, lk, g8h, n, t, lam_init)
                wkv0 = state_wkv[j]
                first = jnp.repeat(state_shift[j], t, axis=0)[None]
            r, w, kv, vv, kk, b, bonus, g = _rwkv_prep_call(pb, first, p, j, seg, grp)
            y, s_fin = _rwkv_scan_call(kk, w, b, kv, r, vv, wkv0, n, t)
            ob = _rwkv_post_call(y, bonus, g, p['lnx_g'][j], p['lnx_b'][j], seg, grp)
            new_k.append(k.reshape(n, t, H_A, 2 * DK_A))
            new_v.append(v.reshape(n, t, H_A, DV_A))
            new_wkv.append(s_fin)
            new_shift.append(pb.reshape(n, t, B_COLS)[:, -1])
            x = _outproj_even_call(oa.reshape(grp.rows, W_A), ob, wb['w_out_e'][j], x, g1, grp)
            hn = _norm_mod_call(x, p['norm2_g'][i], sc2, sh2, grp)
            h = _swiglu_up_call(hn, wb['ffn_gate'][j], wb['ffn_up'][j], grp)
            x = _down_res_call(h, wb['ffn_down'][j], x, g2, grp)
        else:
            u, vb, vf = _gmlp_in_call(hn, wb['w_in_o'][j], p['lnv_g'][j], p['lnv_b'][j], grp)
            new_cv.append(vf.reshape(n, t, E_C))
            ws, bs = _spatial_weights(p['w_s'][j], p['b_s'][j], grp)
            x = _gmlp_out_call(u, vb, ws, bs, wb['w_out_o'][j], x, g1, grp)
            hn = _norm_mod_call(x, p['norm2_g'][i], sc2, sh2, grp)
            x = _moe_call(hn, p['router_w'][j], p['router_b'][j], wb['exp_gate'], wb['exp_up'],
                          wb['exp_down'], j, x, g2, grp)
    y = _final_norm_call(x, p['final_g'], grp)
    return (y.reshape(n, t, D_MODEL), jnp.stack(new_k), jnp.stack(new_v), jnp.stack(new_wkv),
            jnp.stack(new_shift), jnp.stack(new_cv))


_BF16_WEIGHTS = ('w_in_e', 'w_out_e', 'ffn_gate', 'ffn_up', 'ffn_down', 'w_in_o', 'w_out_o',
                 'exp_gate', 'exp_up', 'exp_down')


def _run(x_prompt, x_sample, cache_k, cache_v, state_wkv, state_shift, page_table, c_prompt, c_sample, p):
    n_p, t_p, _ = x_prompt.shape
    n_s, t_s, _ = x_sample.shape
    grp_p = _Group(n_p, t_p, min(ROW_TILE, t_p))
    grp_s = _Group(n_s, t_s, n_s * t_s)
    wb = {name: p[name].astype(BF16) for name in _BF16_WEIGHTS}
    mod = _mod_call(jnp.concatenate([c_prompt, c_sample], axis=0), p['w_mod'], p['b_mod'])
    mods_p = [jnp.split(mod[i, :n_p], 6, axis=-1) for i in range(DEPTH)]
    mods_s = [jnp.split(mod[i, n_p:], 6, axis=-1) for i in range(DEPTH)]
    y_p, k_p, v_p, wkv_p, sh_p, _ = _forward(x_prompt.reshape(n_p * t_p, D_MODEL), mods_p, p, wb, grp_p, None)
    y_s, k_s, v_s, wkv_s, sh_s, cv_s = _forward(
        x_sample.reshape(n_s * t_s, D_MODEL), mods_s, p, wb, grp_s,
        (cache_k, cache_v, page_table, state_wkv, state_shift))
    return (y_p, y_s, k_p, v_p, wkv_p, sh_p, k_s, v_s, wkv_s, sh_s, cv_s)


def kernel(x_prompt, x_sample, cache_k, cache_v, state_wkv, state_shift, page_table, c_prompt, c_sample,
           w_mod, b_mod, norm1_g, norm2_g, final_g, w_in_e, w_out_e, lam_q1, lam_k1, lam_q2, lam_k2,
           subln_g, mu_b, w0, w_up, a0, a_up, g_up, k_k, k_a, r_k, lnx_g, lnx_b, w_in_o, lnv_g, lnv_b,
           w_s, b_s, w_out_o, ffn_gate, ffn_up, ffn_down, router_w, router_b, exp_gate, exp_up, exp_down):
    p = dict(w_mod=w_mod, b_mod=b_mod, norm1_g=norm1_g, norm2_g=norm2_g, final_g=final_g,
             w_in_e=w_in_e, w_out_e=w_out_e, lam_q1=lam_q1, lam_k1=lam_k1, lam_q2=lam_q2,
             lam_k2=lam_k2, subln_g=subln_g, mu_b=mu_b, w0=w0, w_up=w_up, a0=a0, a_up=a_up,
             g_up=g_up, k_k=k_k, k_a=k_a, r_k=r_k, lnx_g=lnx_g, lnx_b=lnx_b, w_in_o=w_in_o,
             lnv_g=lnv_g, lnv_b=lnv_b, w_s=w_s, b_s=b_s, w_out_o=w_out_o, ffn_gate=ffn_gate,
             ffn_up=ffn_up, ffn_down=ffn_down, router_w=router_w, router_b=router_b,
             exp_gate=exp_gate, exp_up=exp_up, exp_down=exp_down)
    return _run(x_prompt, x_sample, cache_k, cache_v, state_wkv, state_shift, page_table,
                c_prompt, c_sample, p)
```

```python
import functools
import math
from typing import NamedTuple

import jax
import jax.numpy as jnp
from jax import lax
from jax.experimental import pallas as pl
from jax.experimental.pallas import tpu as pltpu

F32 = jnp.float32
BF16 = jnp.bfloat16

D_MODEL = 1024
DEPTH = 4
H_A, DK_A, DV_A = 8, 32, 64
QK_A = H_A * 2 * DK_A
W_A = H_A * DV_A
A_COLS = 2 * QK_A + W_A
H_B, N_B = 8, 64
W_B = H_B * N_B
LORA_W, LORA_A, LORA_G = 64, 64, 128
LORA_COLS = LORA_W + LORA_A + LORA_G
B_COLS = 3 * W_B + LORA_COLS
IN_COLS_EVEN = A_COLS + B_COLS
CHUNK, G_C = 128, 8
E_C = 2 * D_MODEL
CG = E_C // G_C
D_FF = 2816
N_EXPERTS, D_FF_E = 8, 1408
PAGE_SIZE = 128
RMS_EPS, LN_EPS, LNX_EPS = 1e-6, 1e-5, 64e-5
NEG_INF = -1e30
ATTN_SCALE = DK_A ** -0.5
DECAY_SCALE = math.exp(-0.5)
INV_SQRT2 = 2.0 ** -0.5

LANES = 128
SUBLANES = 8
MIB = 1 << 20
VMEM_LIMIT = 56 * MIB

ROW_TILE = 256
ATTN_TQ, ATTN_TK = 256, 512
PAGES_PER_STEP = 8
SCAN_TC = 16
CHAINS = 64


class _Group(NamedTuple):
    n: int
    t: int
    tm: int

    @property
    def rows(self):
        return self.n * self.t

    @property
    def tiles(self):
        return self.rows // self.tm

    @property
    def tiles_per_seq(self):
        return max(self.t // self.tm, 1)

    @property
    def mod_rows(self):
        return 1 if self.t >= self.tm else self.tm


def _cparams(*sem):
    return pltpu.CompilerParams(dimension_semantics=sem, vmem_limit_bytes=VMEM_LIMIT)


def _row_spec(tm, cols):
    return pl.BlockSpec((tm, cols), lambda i: (i, 0))


def _full_spec(shape):
    zeros = (0,) * len(shape)
    return pl.BlockSpec(shape, lambda *_: zeros)


def _mod_spec(grp):
    tps = grp.tiles_per_seq
    return pl.BlockSpec((None, grp.mod_rows, D_MODEL), lambda i: (i // tps, 0, 0))


def _expand_mod(m, grp):
    if grp.mod_rows == 1:
        return m[:, None, :]
    return jnp.repeat(m, grp.t, axis=0)[None]


def _dot(a, b):
    return jnp.dot(a, b, preferred_element_type=F32)


def _dot_nt(a, b):
    return lax.dot_general(a, b, (((1,), (1,)), ((), ())), preferred_element_type=F32)


def _split_bf16(x):
    hi = x.astype(BF16)
    lo = (x - hi.astype(F32)).astype(BF16)
    return hi, lo


def _segsum(x, seg):
    hi, lo = _split_bf16(x)
    return _dot(hi, seg) + _dot(lo, seg)


def _silu(x):
    return x * jax.nn.sigmoid(x)


def _mod_kernel(c_ref, w_ref, b_ref, o_ref):
    a_hi, a_lo = _split_bf16(_silu(c_ref[...]))
    w_hi, w_lo = _split_bf16(w_ref[...])
    o_ref[...] = _dot(a_hi, w_hi) + _dot(a_hi, w_lo) + _dot(a_lo, w_hi) + b_ref[...]


def _mod_call(c_all, w_mod, b_mod):
    rows = c_all.shape[0]
    tn = D_MODEL
    return pl.pallas_call(
        _mod_kernel,
        grid=(DEPTH, 6 * D_MODEL // tn),
        in_specs=[
            pl.BlockSpec((rows, D_MODEL), lambda i, j: (0, 0)),
            pl.BlockSpec((None, D_MODEL, tn), lambda i, j: (i, 0, j)),
            pl.BlockSpec((None, 1, tn), lambda i, j: (i, 0, j)),
        ],
        out_specs=pl.BlockSpec((None, rows, tn), lambda i, j: (i, 0, j)),
        out_shape=jax.ShapeDtypeStruct((DEPTH, rows, 6 * D_MODEL), F32),
        compiler_params=_cparams("parallel", "parallel"),
        name="adaln_mod",
    )(c_all, w_mod, b_mod[:, None, :])


def _norm_mod_kernel(x_ref, g_ref, sc_ref, sh_ref, o_ref):
    x = x_ref[...]
    y = x * lax.rsqrt(jnp.mean(x * x, axis=-1, keepdims=True) + RMS_EPS) * g_ref[...]
    o_ref[...] = (y * (1.0 + sc_ref[...]) + sh_ref[...]).astype(o_ref.dtype)


def _norm_mod_call(x, g, sc, sh, grp):
    return pl.pallas_call(
        _norm_mod_kernel,
        grid=(grp.tiles,),
        in_specs=[_row_spec(grp.tm, D_MODEL), _full_spec((1, D_MODEL)), _mod_spec(grp), _mod_spec(grp)],
        out_specs=_row_spec(grp.tm, D_MODEL),
        out_shape=jax.ShapeDtypeStruct((grp.rows, D_MODEL), BF16),
        compiler_params=_cparams("parallel"),
        name="norm_mod",
    )(x, g[None, :], sc, sh)


def _norm_mod_router_kernel(x_ref, g_ref, sc_ref, sh_ref, wr_ref, br_ref, o_ref, logit_ref):
    x = x_ref[...]
    y = x * lax.rsqrt(jnp.mean(x * x, axis=-1, keepdims=True) + RMS_EPS) * g_ref[...]
    hn = y * (1.0 + sc_ref[...]) + sh_ref[...]
    o_ref[...] = hn.astype(o_ref.dtype)
    h_hi, h_lo = _split_bf16(hn)
    w_hi, w_lo = _split_bf16(wr_ref[...])
    logit_ref[...] = _dot(h_hi, w_hi) + _dot(h_hi, w_lo) + _dot(h_lo, w_hi) + br_ref[...]


def _norm_mod_router_call(x, g, sc, sh, wr, br, grp):
    wr_pad = jnp.zeros((D_MODEL, LANES), F32).at[:, :N_EXPERTS].set(wr)
    br_pad = jnp.full((1, LANES), -jnp.inf, F32).at[0, :N_EXPERTS].set(br)
    return pl.pallas_call(
        _norm_mod_router_kernel,
        grid=(grp.tiles,),
        in_specs=[_row_spec(grp.tm, D_MODEL), _full_spec((1, D_MODEL)), _mod_spec(grp), _mod_spec(grp),
                  _full_spec((D_MODEL, LANES)), _full_spec((1, LANES))],
        out_specs=[_row_spec(grp.tm, D_MODEL), _row_spec(grp.tm, LANES)],
        out_shape=[jax.ShapeDtypeStruct((grp.rows, D_MODEL), BF16),
                   jax.ShapeDtypeStruct((grp.rows, LANES), F32)],
        compiler_params=_cparams("parallel"),
        name="norm_mod_router",
    )(x, g[None, :], sc, sh, wr_pad, br_pad)


def _final_norm_kernel(x_ref, g_ref, o_ref):
    x = x_ref[...]
    o_ref[...] = x * lax.rsqrt(jnp.mean(x * x, axis=-1, keepdims=True) + RMS_EPS) * g_ref[...]


def _final_norm_call(x, g, grp):
    return pl.pallas_call(
        _final_norm_kernel,
        grid=(grp.tiles,),
        in_specs=[_row_spec(grp.tm, D_MODEL), _full_spec((1, D_MODEL))],
        out_specs=_row_spec(grp.tm, D_MODEL),
        out_shape=jax.ShapeDtypeStruct((grp.rows, D_MODEL), F32),
        compiler_params=_cparams("parallel"),
        name="final_norm",
    )(x, g[None, :])


def _inproj_even_kernel(a_ref, w_ref, q_ref, k_ref, v_ref, kb_ref, vb_ref, pb_ref):
    a = a_ref[...]
    q_ref[...] = (_dot(a, w_ref[:, :QK_A]) * ATTN_SCALE).astype(BF16)
    k = _dot(a, w_ref[:, QK_A:2 * QK_A])
    k_ref[...] = k
    kb_ref[...] = k.astype(BF16)
    v = _dot(a, w_ref[:, 2 * QK_A:A_COLS])
    v_ref[...] = v
    vb_ref[...] = v.astype(BF16)
    half = B_COLS // 2
    pb_ref[:, :half] = _dot(a, w_ref[:, A_COLS:A_COLS + half])
    pb_ref[:, half:] = _dot(a, w_ref[:, A_COLS + half:])


def _inproj_even_call(hn, w, grp):
    tm, rows = grp.tm, grp.rows
    sds = jax.ShapeDtypeStruct
    return pl.pallas_call(
        _inproj_even_kernel,
        grid=(grp.tiles,),
        in_specs=[_row_spec(tm, D_MODEL), _full_spec((D_MODEL, IN_COLS_EVEN))],
        out_specs=[_row_spec(tm, QK_A), _row_spec(tm, QK_A), _row_spec(tm, W_A),
                   _row_spec(tm, QK_A), _row_spec(tm, W_A), _row_spec(tm, B_COLS)],
        out_shape=[sds((rows, QK_A), BF16), sds((rows, QK_A), F32), sds((rows, W_A), F32),
                   sds((rows, QK_A), BF16), sds((rows, W_A), BF16), sds((rows, B_COLS), F32)],
        compiler_params=_cparams("parallel"),
        name="inproj_even",
    )(hn, w)


def _lambda(lq_ref, lk_ref, lam_init):
    t = jnp.sum(lq_ref[...] * lk_ref[...], axis=1, keepdims=True)
    e = jnp.exp(t)
    return e[0:1] - e[1:2] + lam_init


def _attn_prompt_kernel(lq_ref, lk_ref, g_ref, qf_ref, q_ref, k_ref, kf_ref, vt_ref, o_ref,
                        m_scr, l_scr, acc_scr, *, tq, tk, lam_init):
    hp = pl.program_id(1)
    qi = pl.program_id(2)
    lane = lax.broadcasted_iota(jnp.int32, (1, LANES), 1)
    q = q_ref[...]
    q_aug = []
    for i in range(4):
        q_map = jnp.where(lane // DK_A == i, q, jnp.zeros_like(q))
        feat = jnp.broadcast_to(qf_ref[pl.ds(2 * hp + i // 2, 1), :], (tq, LANES)).astype(BF16)
        q_aug.append(jnp.concatenate([q_map, feat], axis=1))
    m_scr[...] = jnp.full(m_scr.shape, NEG_INF, F32)
    l_scr[...] = jnp.zeros(l_scr.shape, F32)
    acc_scr[...] = jnp.zeros(acc_scr.shape, F32)
    row = lax.broadcasted_iota(jnp.int32, (tk, 1), 0)
    col = lax.broadcasted_iota(jnp.int32, (1, tq), 1)
    q0 = qi * tq

    def tile(kt, masked):
        k0 = pl.multiple_of(kt * tk, tk)
        k_aug = jnp.concatenate([k_ref[pl.ds(k0, tk), :], kf_ref[pl.ds(k0, tk), :]], axis=1)
        vt = vt_ref[kt]
        if masked:
            keep = (k0 + row) <= (q0 + col)
        for i in range(4):
            h = i // 2
            s = _dot_nt(k_aug, q_aug[i])
            if masked:
                s = jnp.where(keep, s, NEG_INF)
            m_old = m_scr[i]
            m_new = jnp.maximum(m_old, jnp.max(s, axis=0, keepdims=True))
            alpha = jnp.exp(m_old - m_new)
            p = jnp.exp(s - m_new)
            l_scr[i] = alpha * l_scr[i] + jnp.sum(p, axis=0, keepdims=True)
            acc_scr[i] = alpha * acc_scr[i] + _dot(vt[h * DV_A:(h + 1) * DV_A, :], p.astype(BF16))
            m_scr[i] = m_new

    n_full = q0 // tk

    def body(kt, carry):
        tile(kt, False)
        return carry

    lax.fori_loop(0, n_full, body, 0)
    tile(n_full, True)

    lam = _lambda(lq_ref, lk_ref, lam_init)
    heads = []
    for h in range(2):
        oh = acc_scr[2 * h] / l_scr[2 * h] - lam * (acc_scr[2 * h + 1] / l_scr[2 * h + 1])
        ms = jnp.mean(oh * oh, axis=0, keepdims=True)
        heads.append(oh * lax.rsqrt(ms + RMS_EPS))
    o = jnp.concatenate(heads, axis=0).T
    o_ref[...] = (o * g_ref[...] * (1.0 - lam_init)).astype(o_ref.dtype)


def _attn_prompt_call(q, k, v, slopes, lq, lk, g2, n, t, lam_init):
    tq, tk = ATTN_TQ, min(ATTN_TK, t)
    tq = min(tq, tk)
    kern = functools.partial(_attn_prompt_kernel, tq=tq, tk=tk, lam_init=lam_init)
    pos = jnp.arange(t)
    k_feat = jnp.zeros((t, LANES), F32).at[:, 0].set(pos // 256).at[:, 1].set(pos % 256).astype(BF16)
    q_feat = jnp.zeros((H_A, LANES), F32).at[:, 0].set(256.0 * slopes).at[:, 1].set(slopes)
    v_t = v.reshape(n, t // tk, tk, W_A).transpose(0, 1, 3, 2)
    qspec = pl.BlockSpec((None, tq, LANES), lambda b, hp, qi: (b, qi, hp))
    return pl.pallas_call(
        kern,
        grid=(n, H_A // 2, t // tq),
        in_specs=[_full_spec((2, DK_A)), _full_spec((2, DK_A)), _full_spec((1, LANES)),
                  _full_spec((H_A, LANES)), qspec,
                  pl.BlockSpec((None, t, LANES), lambda b, hp, qi: (b, 0, hp)),
                  _full_spec((t, LANES)),
                  pl.BlockSpec((None, t // tk, LANES, tk), lambda b, hp, qi: (b, 0, hp, 0))],
        out_specs=qspec,
        out_shape=jax.ShapeDtypeStruct((n, t, W_A), BF16),
        scratch_shapes=[pltpu.VMEM((4, 1, tq), F32), pltpu.VMEM((4, 1, tq), F32),
                        pltpu.VMEM((4, DV_A, tq), F32)],
        compiler_params=_cparams("parallel", "parallel", "parallel"),
        name="diff_attn_prompt",
    )(lq, lk, g2, q_feat, q.reshape(n, t, QK_A), k.reshape(n, t, QK_A), k_feat, v_t)


def _attn_sample_kernel(pt_ref, lq_ref, lk_ref, g_ref, q_ref, kn_ref, vn_ref, *rest,
                        pages, steps, past_len, t_new, lam_init):
    k_pages, v_pages = rest[:pages], rest[pages:2 * pages]
    o_ref, m_scr, l_scr, acc_scr = rest[2 * pages:]
    step = pl.program_id(1)
    hr = 2 * t_new
    row = lax.broadcasted_iota(jnp.int32, (H_A * hr, 1), 0)
    slope = jnp.exp2(-(row // hr + 1).astype(F32))

    @pl.when(step == 0)
    def _():
        m_scr[...] = jnp.full(m_scr.shape, NEG_INF, F32)
        l_scr[...] = jnp.zeros(l_scr.shape, F32)
        acc_scr[...] = jnp.zeros(acc_scr.shape, F32)

    def head_rows(page_ref, h):
        return page_ref[pl.ds(h, PAGE_SIZE, stride=H_A), :]

    def update(scores, values, keep):
        s = jnp.concatenate(scores, axis=0)
        if keep is not None:
            s = jnp.where(keep, s, NEG_INF)
        m_old = m_scr[...]
        m_new = jnp.maximum(m_old, jnp.max(s, axis=1, keepdims=True))
        alpha = jnp.exp(m_old - m_new)
        p = jnp.exp(s - m_new)
        l_scr[...] = alpha * l_scr[...] + jnp.sum(p, axis=1, keepdims=True)
        p = p.astype(BF16)
        pv = jnp.concatenate([_dot(p[h * hr:(h + 1) * hr], values[h]) for h in range(H_A)], axis=0)
        acc_scr[...] = alpha * acc_scr[...] + pv
        m_scr[...] = m_new

    col = lax.broadcasted_iota(jnp.int32, (1, pages * PAGE_SIZE), 1)
    k_rel = (step * (pages * PAGE_SIZE) - past_len + col).astype(F32)
    scores, values = [], []
    for h in range(H_A):
        k = jnp.concatenate([head_rows(kp, h) for kp in k_pages], axis=0).astype(BF16)
        values.append(jnp.concatenate([head_rows(vp, h) for vp in v_pages], axis=0).astype(BF16))
        scores.append(_dot_nt(q_ref[h], k))
    update([s + slope[h * hr:(h + 1) * hr] * k_rel for h, s in enumerate(scores)], values, None)

    @pl.when(step == steps - 1)
    def _():
        new_col = lax.broadcasted_iota(jnp.int32, (1, PAGE_SIZE), 1)
        keep = new_col <= row % t_new
        bias = slope * new_col.astype(F32)
        update([_dot_nt(q_ref[h], kn_ref[h]) + bias[h * hr:(h + 1) * hr] for h in range(H_A)],
               [vn_ref[h] for h in range(H_A)], keep)
        lam = _lambda(lq_ref, lk_ref, lam_init)
        a = acc_scr[...] / l_scr[...]
        for h in range(H_A):
            oh = a[h * hr:h * hr + t_new] - lam * a[h * hr + t_new:(h + 1) * hr]
            ms = jnp.mean(oh * oh, axis=1, keepdims=True)
            o_ref[h] = oh * lax.rsqrt(ms + RMS_EPS) * g_ref[...] * (1.0 - lam_init)


def _attn_sample_call(q, k_new, v_new, cache_k, cache_v, page_table, j, lq, lk, g1, n, t, lam_init):
    n_pages = page_table.shape[1]
    pages = PAGES_PER_STEP
    steps = n_pages // pages
    n_pool = cache_k.shape[1]
    rows = PAGE_SIZE * H_A
    ck = cache_k.reshape(cache_k.shape[0], n_pool, rows, 2 * DK_A)
    cv = cache_v.reshape(cache_v.shape[0], n_pool, rows, DV_A)
    kern = functools.partial(_attn_sample_kernel, pages=pages, steps=steps,
                             past_len=n_pages * PAGE_SIZE, t_new=t, lam_init=lam_init)
    qh = jnp.einsum('nthmd,km->nhktmd', q.reshape(n, t, H_A, 2, DK_A), jnp.eye(2, dtype=q.dtype))
    qh = qh.reshape(n, H_A, 2 * t, 2 * DK_A)

    def new_rows(x):
        x = x.reshape(n, t, H_A, DV_A).transpose(0, 2, 1, 3)
        return jnp.pad(x, ((0, 0), (0, 0), (0, PAGE_SIZE - t), (0, 0)))

    def page_spec(i):
        return pl.BlockSpec((None, None, rows, DV_A),
                            lambda b, s, pt: (j, pt[b, s * pages + i], 0, 0))

    small = lambda shape: pl.BlockSpec(shape, lambda b, s, pt: (0,) * len(shape))
    head_spec = lambda r: pl.BlockSpec((None, H_A, r, DV_A), lambda b, s, pt: (b, 0, 0, 0))
    grid_spec = pltpu.PrefetchScalarGridSpec(
        num_scalar_prefetch=1,
        grid=(n, steps),
        in_specs=[small((2, DK_A)), small((2, DK_A)), small((1, DV_A)),
                  head_spec(2 * t), head_spec(PAGE_SIZE), head_spec(PAGE_SIZE)]
                 + [page_spec(i) for i in range(pages)] * 2,
        out_specs=head_spec(t),
        scratch_shapes=[pltpu.VMEM((H_A * 2 * t, 1), F32), pltpu.VMEM((H_A * 2 * t, 1), F32),
                        pltpu.VMEM((H_A * 2 * t, DV_A), F32)],
    )
    o = pl.pallas_call(
        kern,
        grid_spec=grid_spec,
        out_shape=jax.ShapeDtypeStruct((n, H_A, t, DV_A), F32),
        compiler_params=_cparams("parallel", "arbitrary"),
        name="diff_attn_paged",
    )(page_table, lq, lk, g1, qh, new_rows(k_new), new_rows(v_new), *([ck] * pages), *([cv] * pages))
    return o.transpose(0, 2, 1, 3).reshape(n * t, W_A).astype(BF16)


def _rwkv_prep_kernel(pb_ref, first_ref, mu_ref, wl_ref, w0_ref, a0_ref, kk_w_ref, ka_w_ref, rk_ref,
                      seg_ref, r_o, w_o, k_o, v_o, kk_o, b_o, bonus_o, g_o, *, period):
    pb = pb_ref[...]
    tm = pb.shape[0]
    row = lax.broadcasted_iota(jnp.int32, (tm, 1), 0)
    prev = jnp.where(row % period == 0, first_ref[...], pltpu.roll(pb, 1, 0))
    xb = pb + (prev - pb) * mu_ref[...]
    r = xb[:, :W_B]
    kb = xb[:, W_B:2 * W_B]
    vb = xb[:, 2 * W_B:3 * W_B]
    z = xb[:, 3 * W_B:]
    ll = lax.broadcasted_iota(jnp.int32, (1, LORA_COLS), 1)
    zz = jnp.where(ll < LORA_W, jnp.tanh(z), jnp.where(ll < LORA_W + LORA_A, z, jax.nn.sigmoid(z)))
    lo = _dot(zz.astype(BF16), wl_ref[...])
    decay = jnp.exp(-DECAY_SCALE * jax.nn.sigmoid(w0_ref[...] + lo[:, :W_B]))
    a = jax.nn.sigmoid(a0_ref[...] + lo[:, W_B:2 * W_B])
    seg = seg_ref[...]
    kk = kb * kk_w_ref[...]
    kk = kk / jnp.maximum(jnp.sqrt(_segsum(kk * kk, seg)), 1e-12)
    k_adj = kb * (1.0 + (a - 1.0) * ka_w_ref[...])
    r_o[...] = r
    w_o[...] = decay
    k_o[...] = k_adj
    v_o[...] = vb
    kk_o[...] = kk
    b_o[...] = kk * a
    bonus_o[...] = _segsum(r * k_adj * rk_ref[...], seg) * vb
    g_o[...] = lo[:, 2 * W_B:]


def _rwkv_prep_call(pb, first, p, j, seg, grp):
    tm, rows = grp.tm, grp.rows
    period = tm if grp.mod_rows == 1 else grp.t
    first_rows = first.shape[1]
    vec = lambda x: x[j].reshape(1, -1)
    wl = jnp.zeros((LORA_COLS, 3 * W_B), F32)
    wl = wl.at[:LORA_W, :W_B].set(p['w_up'][j])
    wl = wl.at[LORA_W:LORA_W + LORA_A, W_B:2 * W_B].set(p['a_up'][j])
    wl = wl.at[LORA_W + LORA_A:, 2 * W_B:].set(p['g_up'][j])
    outs = [jax.ShapeDtypeStruct((rows, W_B), F32)] * 8
    return pl.pallas_call(
        functools.partial(_rwkv_prep_kernel, period=period),
        grid=(grp.tiles,),
        in_specs=[_row_spec(tm, B_COLS),
                  pl.BlockSpec((None, first_rows, B_COLS), lambda i: (i, 0, 0)),
                  _full_spec((1, B_COLS)), _full_spec((LORA_COLS, 3 * W_B)),
                  _full_spec((1, W_B)), _full_spec((1, W_B)), _full_spec((1, W_B)),
                  _full_spec((1, W_B)), _full_spec((1, W_B)), _full_spec((W_B, W_B))],
        out_specs=[_row_spec(tm, W_B)] * 8,
        out_shape=outs,
        compiler_params=_cparams("parallel"),
        name="rwkv_prep",
    )(pb, first, vec(p['mu_b']), wl.astype(BF16), vec(p['w0']), vec(p['a0']), vec(p['k_k']),
      vec(p['k_a']), vec(p['r_k']), seg)


def _rwkv_scan_kernel(kk_ref, w_ref, b_ref, k_ref, r_ref, v_ref, s0_ref, y_ref, s_ref, *, tc):
    @pl.when(pl.program_id(1) == 0)
    def _():
        s_ref[...] = s0_ref[...]

    n_acc = 4

    def step(t, carry):
        parts = [None] * n_acc
        for k in range(N_B):
            term = s_ref[k] * kk_ref[t, pl.ds(k, 1), :]
            parts[k % n_acc] = term if parts[k % n_acc] is None else parts[k % n_acc] + term
        sa = (parts[0] + parts[1]) + (parts[2] + parts[3])
        v_t = v_ref[t]
        parts = [None] * n_acc
        for k in range(N_B):
            s_new = (s_ref[k] * w_ref[t, pl.ds(k, 1), :] - sa * b_ref[t, pl.ds(k, 1), :]
                     + v_t * k_ref[t, pl.ds(k, 1), :])
            s_ref[k] = s_new
            term = s_new * r_ref[t, pl.ds(k, 1), :]
            parts[k % n_acc] = term if parts[k % n_acc] is None else parts[k % n_acc] + term
        y_ref[t] = (parts[0] + parts[1]) + (parts[2] + parts[3])
        return carry

    lax.fori_loop(0, tc, step, 0)


def _to_chain_layout(x, n, t):
    groups = n * H_B // CHAINS
    x = x.reshape(n, t, H_B, N_B).transpose(1, 3, 0, 2).reshape(t, N_B, groups, CHAINS)
    x = x.transpose(2, 0, 1, 3)
    return jnp.concatenate([x, x], axis=-1)


def _value_to_chain_layout(v, n, t):
    groups = n * H_B // CHAINS
    v = v.reshape(n, t, H_B, N_B).transpose(1, 3, 0, 2).reshape(t, 2, N_B // 2, groups, CHAINS)
    return v.transpose(3, 0, 2, 1, 4).reshape(groups, t, N_B // 2, 2 * CHAINS)


def _value_from_chain_layout(y, n, t):
    groups = n * H_B // CHAINS
    y = y.reshape(groups, t, N_B // 2, 2, CHAINS).transpose(1, 3, 2, 0, 4)
    y = y.reshape(t, N_B, n, H_B).transpose(2, 0, 3, 1)
    return y.reshape(n * t, W_B)


def _state_to_chain_layout(s, n):
    groups = n * H_B // CHAINS
    s = s.reshape(groups, CHAINS, 2, N_B // 2, N_B)
    return s.transpose(0, 4, 3, 2, 1).reshape(groups, N_B, N_B // 2, 2 * CHAINS)


def _state_from_chain_layout(s, n):
    groups = n * H_B // CHAINS
    s = s.reshape(groups, N_B, N_B // 2, 2, CHAINS).transpose(0, 4, 3, 2, 1)
    return s.reshape(n, H_B, N_B, N_B)


def _chain_in_kernel(x_ref, o_ref, z_scr, *, per_key, tt):
    n_seq = x_ref.shape[0]
    for s in range(n_seq):
        z_scr[s] = x_ref[s].T
    for j in range(o_ref.shape[0] // tt):
        rows = []
        for half in range(2):
            start = j if per_key else half * (N_B // 2) + j
            rows += [z_scr[s, pl.ds(start, H_B, stride=N_B), :] for s in range(n_seq)]
        tile = jnp.concatenate(rows, axis=0).T
        o_ref[pl.ds(j, tt, stride=o_ref.shape[0] // tt), :] = tile


def _chain_in_call(x, n, t, per_key):
    tt = LANES
    rows = N_B if per_key else N_B // 2
    out = pl.pallas_call(
        functools.partial(_chain_in_kernel, per_key=per_key, tt=tt),
        grid=(t // tt,),
        in_specs=[pl.BlockSpec((n, tt, W_B), lambda i: (0, i, 0))],
        out_specs=pl.BlockSpec((tt * rows, 2 * CHAINS), lambda i: (i, 0)),
        out_shape=jax.ShapeDtypeStruct((t * rows, 2 * CHAINS), F32),
        scratch_shapes=[pltpu.VMEM((n, W_B, tt), F32)],
        compiler_params=_cparams("parallel"),
        name="chain_layout_in",
    )(x.reshape(n, t, W_B))
    return out.reshape(1, t, rows, 2 * CHAINS)


def _chain_out_kernel(y_ref, o_ref, z_scr, *, tt):
    n_seq = o_ref.shape[0]
    rows = N_B // 2
    for j in range(rows):
        tile = y_ref[pl.ds(j, tt, stride=rows), :].T
        for half in range(2):
            for s in range(n_seq):
                r0 = (half * n_seq + s) * H_B
                z_scr[s, pl.ds(half * rows + j, H_B, stride=N_B), :] = tile[r0:r0 + H_B]
    for s in range(n_seq):
        o_ref[s] = z_scr[s].T


def _chain_out_call(y, n, t):
    tt = LANES
    rows = N_B // 2
    out = pl.pallas_call(
        functools.partial(_chain_out_kernel, tt=tt),
        grid=(t // tt,),
        in_specs=[pl.BlockSpec((tt * rows, 2 * CHAINS), lambda i: (i, 0))],
        out_specs=pl.BlockSpec((n, tt, W_B), lambda i: (0, i, 0)),
        out_shape=jax.ShapeDtypeStruct((n, t, W_B), F32),
        scratch_shapes=[pltpu.VMEM((n, W_B, tt), F32)],
        compiler_params=_cparams("parallel"),
        name="chain_layout_out",
    )(y.reshape(t * rows, 2 * CHAINS))
    return out.reshape(n * t, W_B)


def _rwkv_scan_call(kk, w, b, k, r, v, s0, n, t):
    groups = n * H_B // CHAINS
    tc = min(SCAN_TC, t)
    vec_spec = pl.BlockSpec((None, tc, N_B, 2 * CHAINS), lambda g, i: (g, i, 0, 0))
    val_spec = pl.BlockSpec((None, tc, N_B // 2, 2 * CHAINS), lambda g, i: (g, i, 0, 0))
    st_spec = pl.BlockSpec((None, N_B, N_B // 2, 2 * CHAINS), lambda g, i: (g, 0, 0, 0))
    in_kernel_layout = groups == 1 and t % LANES == 0
    if in_kernel_layout:
        vecs = [_chain_in_call(x, n, t, True) for x in (kk, w, b, k, r)]
        val = _chain_in_call(v, n, t, False)
    else:
        vecs = [_to_chain_layout(x, n, t) for x in (kk, w, b, k, r)]
        val = _value_to_chain_layout(v, n, t)
    y, s_fin = pl.pallas_call(
        functools.partial(_rwkv_scan_kernel, tc=tc),
        grid=(groups, t // tc),
        in_specs=[vec_spec] * 5 + [val_spec, st_spec],
        out_specs=[val_spec, st_spec],
        out_shape=[jax.ShapeDtypeStruct((groups, t, N_B // 2, 2 * CHAINS), F32),
                   jax.ShapeDtypeStruct((groups, N_B, N_B // 2, 2 * CHAINS), F32)],
        compiler_params=_cparams("parallel", "arbitrary"),
        name="rwkv_scan",
    )(*vecs, val, _state_to_chain_layout(s0, n))
    y = _chain_out_call(y, n, t) if in_kernel_layout else _value_from_chain_layout(y, n, t)
    return y, _state_from_chain_layout(s_fin, n)


def _rwkv_post_kernel(y_ref, bonus_ref, g_ref, lg_ref, lb_ref, seg_ref, o_ref):
    y = y_ref[...]
    seg = seg_ref[...]
    yc = y - _segsum(y, seg) * (1.0 / N_B)
    var = _segsum(yc * yc, seg) * (1.0 / N_B)
    yn = yc * lax.rsqrt(var + LNX_EPS) * lg_ref[...] + lb_ref[...]
    o_ref[...] = ((yn + bonus_ref[...]) * g_ref[...]).astype(o_ref.dtype)


def _rwkv_post_call(y, bonus, g, lg, lb, seg, grp):
    tm = grp.tm
    return pl.pallas_call(
        _rwkv_post_kernel,
        grid=(grp.tiles,),
        in_specs=[_row_spec(tm, W_B)] * 3 + [_full_spec((1, W_B)), _full_spec((1, W_B)),
                                              _full_spec((W_B, W_B))],
        out_specs=_row_spec(tm, W_B),
        out_shape=jax.ShapeDtypeStruct((grp.rows, W_B), BF16),
        compiler_params=_cparams("parallel"),
        name="rwkv_post",
    )(y, bonus, g, lg[None, :], lb[None, :], seg)


def _outproj_even_kernel(oa_ref, ob_ref, w_ref, x_ref, g_ref, o_ref):
    mix = _dot(oa_ref[...], w_ref[:W_A, :]) + _dot(ob_ref[...], w_ref[W_A:, :])
    o_ref[...] = x_ref[...] + g_ref[...] * mix


def _outproj_even_call(oa, ob, w, x, gate, grp):
    tm = grp.tm
    return pl.pallas_call(
        _outproj_even_kernel,
        grid=(grp.tiles,),
        in_specs=[_row_spec(tm, W_A), _row_spec(tm, W_B), _full_spec((W_A + W_B, D_MODEL)),
                  _row_spec(tm, D_MODEL), _mod_spec(grp)],
        out_specs=_row_spec(tm, D_MODEL),
        out_shape=jax.ShapeDtypeStruct((grp.rows, D_MODEL), F32),
        compiler_params=_cparams("parallel"),
        name="outproj_even",
    )(oa, ob, w, x, gate)


def _swiglu_up_kernel(a_ref, wg_ref, wu_ref, o_ref):
    a = a_ref[...]
    o_ref[...] = (_silu(_dot(a, wg_ref[...])) * _dot(a, wu_ref[...])).astype(o_ref.dtype)


def _swiglu_up_call(hn, wg, wu, grp):
    tm = grp.tm
    tn = D_FF // 2
    return pl.pallas_call(
        _swiglu_up_kernel,
        grid=(D_FF // tn, grp.tiles),
        in_specs=[pl.BlockSpec((tm, D_MODEL), lambda c, i: (i, 0)),
                  pl.BlockSpec((D_MODEL, tn), lambda c, i: (0, c)),
                  pl.BlockSpec((D_MODEL, tn), lambda c, i: (0, c))],
        out_specs=pl.BlockSpec((tm, tn), lambda c, i: (i, c)),
        out_shape=jax.ShapeDtypeStruct((grp.rows, D_FF), BF16),
        compiler_params=_cparams("parallel", "parallel"),
        name="swiglu_up",
    )(hn, wg, wu)


def _down_res_kernel(h_ref, w_ref, x_ref, g_ref, o_ref):
    o_ref[...] = x_ref[...] + g_ref[...] * _dot(h_ref[...], w_ref[...])


def _down_res_call(h, w, x, gate, grp):
    tm = grp.tm
    kdim = h.shape[1]
    return pl.pallas_call(
        _down_res_kernel,
        grid=(grp.tiles,),
        in_specs=[_row_spec(tm, kdim), _full_spec((kdim, D_MODEL)), _row_spec(tm, D_MODEL), _mod_spec(grp)],
        out_specs=_row_spec(tm, D_MODEL),
        out_shape=jax.ShapeDtypeStruct((grp.rows, D_MODEL), F32),
        compiler_params=_cparams("parallel"),
        name="down_res",
    )(h, w, x, gate)


def _gelu(x):
    return 0.5 * x * (1.0 + lax.erf(x * INV_SQRT2))


def _gmlp_u_kernel(a_ref, w_ref, o_ref):
    o_ref[...] = _gelu(_dot(a_ref[...], w_ref[...]))


def _gmlp_v_kernel(a_ref, w_ref, lg_ref, lb_ref, vb_ref, vf_ref):
    v = _gelu(_dot(a_ref[...], w_ref[...]))
    vc = v - jnp.mean(v, axis=-1, keepdims=True)
    vn = vc * lax.rsqrt(jnp.mean(vc * vc, axis=-1, keepdims=True) + LN_EPS) * lg_ref[...] + lb_ref[...]
    vb_ref[...] = vn.astype(BF16)
    vf_ref[...] = vn


def _gmlp_in_call(hn, w, lg, lb, grp):
    tm, rows = grp.tm, grp.rows
    wspec = lambda half: pl.BlockSpec((D_MODEL, E_C), lambda i: (0, half))
    u = pl.pallas_call(
        _gmlp_u_kernel,
        grid=(grp.tiles,),
        in_specs=[_row_spec(tm, D_MODEL), wspec(0)],
        out_specs=_row_spec(tm, E_C),
        out_shape=jax.ShapeDtypeStruct((rows, E_C), F32),
        compiler_params=_cparams("parallel"),
        name="gmlp_u",
    )(hn, w)
    vb, vf = pl.pallas_call(
        _gmlp_v_kernel,
        grid=(grp.tiles,),
        in_specs=[_row_spec(tm, D_MODEL), wspec(1), _full_spec((1, E_C)), _full_spec((1, E_C))],
        out_specs=[_row_spec(tm, E_C), _row_spec(tm, E_C)],
        out_shape=[jax.ShapeDtypeStruct((rows, E_C), BF16), jax.ShapeDtypeStruct((rows, E_C), F32)],
        compiler_params=_cparams("parallel"),
        name="gmlp_v",
    )(hn, w, lg[None, :], lb[None, :])
    return u, vb, vf


def _gmlp_out_kernel(u_ref, v_ref, ws_ref, bs_ref, wo_ref, x_ref, g_ref, o_ref, z_scr, *, lm):
    tm = u_ref.shape[0]
    for c in range(tm // lm):
        r0 = c * lm
        for gi in range(G_C):
            c0 = gi * CG
            mixed = _dot(ws_ref[gi], v_ref[r0:r0 + lm, c0:c0 + CG]) + bs_ref[:, gi:gi + 1]
            z_scr[r0:r0 + lm, c0:c0 + CG] = (u_ref[r0:r0 + lm, c0:c0 + CG] * mixed).astype(BF16)
    o_ref[...] = x_ref[...] + g_ref[...] * _dot(z_scr[...], wo_ref[...])


def _gmlp_out_call(u, vb, ws, bs, wo, x, gate, grp):
    tm = grp.tm
    lm = ws.shape[1]
    return pl.pallas_call(
        functools.partial(_gmlp_out_kernel, lm=lm),
        grid=(grp.tiles,),
        in_specs=[_row_spec(tm, E_C), _row_spec(tm, E_C), _full_spec((G_C, lm, lm)), _full_spec((lm, G_C)),
                  _full_spec((E_C, D_MODEL)), _row_spec(tm, D_MODEL), _mod_spec(grp)],
        out_specs=_row_spec(tm, D_MODEL),
        out_shape=jax.ShapeDtypeStruct((grp.rows, D_MODEL), F32),
        scratch_shapes=[pltpu.VMEM((tm, E_C), BF16)],
        compiler_params=_cparams("parallel"),
        name="gmlp_out",
    )(u, vb, ws, bs, wo, x, gate)


def _spatial_weights(w_s, b_s, grp):
    length = min(grp.t, CHUNK)
    ws = jnp.tril(w_s[:, :length, :length])
    bs = b_s[:, :length]
    if length < CHUNK:
        reps = grp.tm // length
        ws = jnp.einsum('ab,gij->gaibj', jnp.eye(reps, dtype=ws.dtype), ws).reshape(
            G_C, reps * length, reps * length)
        bs = jnp.tile(bs, (1, reps))
    return ws.astype(BF16), bs.T


def _moe_kernel(a_ref, logit_ref, wg_ref, wu_ref, wd_ref, x_ref, g_ref, o_ref, gate_scr, acc_scr):
    e = pl.program_id(1)
    lane = lax.broadcasted_iota(jnp.int32, (1, LANES), 1)
    a = a_ref[...]

    @pl.when(e == 0)
    def _():
        logits = logit_ref[...]
        m1 = jnp.max(logits, axis=1, keepdims=True)
        i1 = jnp.min(jnp.where(logits == m1, lane, LANES), axis=1, keepdims=True)
        rest = jnp.where(lane == i1, -jnp.inf, logits)
        m2 = jnp.max(rest, axis=1, keepdims=True)
        i2 = jnp.min(jnp.where(rest == m2, lane, LANES), axis=1, keepdims=True)
        e2 = jnp.exp(m2 - m1)
        gate_scr[...] = jnp.where(lane == i1, 1.0 / (1.0 + e2), jnp.where(lane == i2, e2 / (1.0 + e2), 0.0))
        acc_scr[...] = jnp.zeros(acc_scr.shape, F32)

    h = (_silu(_dot(a, wg_ref[...])) * _dot(a, wu_ref[...])).astype(BF16)
    y = _dot(h, wd_ref[...])
    gate_e = jnp.sum(jnp.where(lane == e, gate_scr[...], 0.0), axis=1, keepdims=True)
    acc_scr[...] += gate_e * y

    @pl.when(e == N_EXPERTS - 1)
    def _():
        o_ref[...] = x_ref[...] + g_ref[...] * acc_scr[...]


def _moe_call(hn, logits, wg, wu, wd, j, x, gate, grp):
    tm = grp.tm
    tps = grp.tiles_per_seq
    row = lambda cols: pl.BlockSpec((tm, cols), lambda i, e: (i, 0))
    return pl.pallas_call(
        _moe_kernel,
        grid=(grp.tiles, N_EXPERTS),
        in_specs=[row(D_MODEL), row(LANES),
                  pl.BlockSpec((None, None, D_MODEL, D_FF_E), lambda i, e: (j, e, 0, 0)),
                  pl.BlockSpec((None, None, D_MODEL, D_FF_E), lambda i, e: (j, e, 0, 0)),
                  pl.BlockSpec((None, None, D_FF_E, D_MODEL), lambda i, e: (j, e, 0, 0)),
                  row(D_MODEL),
                  pl.BlockSpec((None, grp.mod_rows, D_MODEL), lambda i, e: (i // tps, 0, 0))],
        out_specs=row(D_MODEL),
        out_shape=jax.ShapeDtypeStruct((grp.rows, D_MODEL), F32),
        scratch_shapes=[pltpu.VMEM((tm, LANES), F32), pltpu.VMEM((tm, D_MODEL), F32)],
        compiler_params=_cparams("parallel", "arbitrary"),
        name="moe_experts",
    )(hn, logits, wg, wu, wd, x, gate)


def _forward(x, mods, p, wb, grp, cache):
    n, t = grp.n, grp.t
    seg = jnp.kron(jnp.eye(H_B, dtype=F32), jnp.ones((N_B, N_B), F32)).astype(BF16)
    slopes = jnp.exp2(-8.0 * jnp.arange(1, H_A + 1, dtype=F32) / H_A)
    new_k, new_v, new_wkv, new_shift, new_cv = [], [], [], [], []
    for i in range(DEPTH):
        j = i // 2
        sh1, sc1, g1, sh2, sc2, g2 = [_expand_mod(m, grp) for m in mods[i]]
        hn = _norm_mod_call(x, p['norm1_g'][i], sc1, sh1, grp)
        if i % 2 == 0:
            q, k, v, kb, vb, pb = _inproj_even_call(hn, wb['w_in_e'][j], grp)
            lam_init = 0.8 - 0.6 * math.exp(-0.3 * i)
            lq = jnp.stack([p['lam_q1'][j], p['lam_q2'][j]])
            lk = jnp.stack([p['lam_k1'][j], p['lam_k2'][j]])
            if cache is None:
                g2h = jnp.tile(p['subln_g'][j], 2)[None, :]
                oa = _attn_prompt_call(q, kb, vb, slopes, lq, lk, g2h, n, t, lam_init)
                wkv0 = jnp.zeros((n, H_B, N_B, N_B), F32)
                last = pb.reshape(grp.tiles, grp.tm, B_COLS)[:, -1, :]
                first = jnp.concatenate([jnp.zeros((1, B_COLS), F32), last[:-1]], axis=0)
                starts = (jnp.arange(grp.tiles) % grp.tiles_per_seq == 0)[:, None]
                first = jnp.where(starts, 0.0, first)[:, None, :]
            else:
                cache_k, cache_v, page_table, state_wkv, state_shift = cache
                oa = _attn_sample_call(q, kb, vb, cache_k, cache_v, page_table, j, lq, lk,
                                       p['subln_g'][j][None, :], n, t, lam_init)
                wkv0 = state_wkv[j]
                first = jnp.repeat(state_shift[j], t, axis=0)[None]
            r, w, kv, vv, kk, b, bonus, g = _rwkv_prep_call(pb, first, p, j, seg, grp)
            y, s_fin = _rwkv_scan_call(kk, w, b, kv, r, vv, wkv0, n, t)
            ob = _rwkv_post_call(y, bonus, g, p['lnx_g'][j], p['lnx_b'][j], seg, grp)
            new_k.append(k.reshape(n, t, H_A, 2 * DK_A))
            new_v.append(v.reshape(n, t, H_A, DV_A))
            new_wkv.append(s_fin)
            new_shift.append(pb.reshape(n, t, B_COLS)[:, -1])
            x = _outproj_even_call(oa.reshape(grp.rows, W_A), ob, wb['w_out_e'][j], x, g1, grp)
            hn = _norm_mod_call(x, p['norm2_g'][i], sc2, sh2, grp)
            h = _swiglu_up_call(hn, wb['ffn_gate'][j], wb['ffn_up'][j], grp)
            x = _down_res_call(h, wb['ffn_down'][j], x, g2, grp)
        else:
            u, vb, vf = _gmlp_in_call(hn, wb['w_in_o'][j], p['lnv_g'][j], p['lnv_b'][j], grp)
            new_cv.append(vf.reshape(n, t, E_C))
            ws, bs = _spatial_weights(p['w_s'][j], p['b_s'][j], grp)
            x = _gmlp_out_call(u, vb, ws, bs, wb['w_out_o'][j], x, g1, grp)
            hn, logits = _norm_mod_router_call(x, p['norm2_g'][i], sc2, sh2, p['router_w'][j],
                                               p['router_b'][j], grp)
            x = _moe_call(hn, logits, wb['exp_gate'], wb['exp_up'], wb['exp_down'], j, x, g2, grp)
    y = _final_norm_call(x, p['final_g'], grp)
    return (y.reshape(n, t, D_MODEL), jnp.stack(new_k), jnp.stack(new_v), jnp.stack(new_wkv),
            jnp.stack(new_shift), jnp.stack(new_cv))


_BF16_WEIGHTS = ('w_in_e', 'w_out_e', 'ffn_gate', 'ffn_up', 'ffn_down', 'w_in_o', 'w_out_o',
                 'exp_gate', 'exp_up', 'exp_down')


def _run(x_prompt, x_sample, cache_k, cache_v, state_wkv, state_shift, page_table, c_prompt, c_sample, p):
    n_p, t_p, _ = x_prompt.shape
    n_s, t_s, _ = x_sample.shape
    grp_p = _Group(n_p, t_p, min(ROW_TILE, t_p))
    grp_s = _Group(n_s, t_s, n_s * t_s)
    wb = {name: p[name].astype(BF16) for name in _BF16_WEIGHTS}
    mod = _mod_call(jnp.concatenate([c_prompt, c_sample], axis=0), p['w_mod'], p['b_mod'])
    mods_p = [jnp.split(mod[i, :n_p], 6, axis=-1) for i in range(DEPTH)]
    mods_s = [jnp.split(mod[i, n_p:], 6, axis=-1) for i in range(DEPTH)]
    y_p, k_p, v_p, wkv_p, sh_p, _ = _forward(x_prompt.reshape(n_p * t_p, D_MODEL), mods_p, p, wb, grp_p, None)
    y_s, k_s, v_s, wkv_s, sh_s, cv_s = _forward(
        x_sample.reshape(n_s * t_s, D_MODEL), mods_s, p, wb, grp_s,
        (cache_k, cache_v, page_table, state_wkv, state_shift))
    return (y_p, y_s, k_p, v_p, wkv_p, sh_p, k_s, v_s, wkv_s, sh_s, cv_s)


def kernel(x_prompt, x_sample, cache_k, cache_v, state_wkv, state_shift, page_table, c_prompt, c_sample,
           w_mod, b_mod, norm1_g, norm2_g, final_g, w_in_e, w_out_e, lam_q1, lam_k1, lam_q2, lam_k2,
           subln_g, mu_b, w0, w_up, a0, a_up, g_up, k_k, k_a, r_k, lnx_g, lnx_b, w_in_o, lnv_g, lnv_b,
           w_s, b_s, w_out_o, ffn_gate, ffn_up, ffn_down, router_w, router_b, exp_gate, exp_up, exp_down):
    p = dict(w_mod=w_mod, b_mod=b_mod, norm1_g=norm1_g, norm2_g=norm2_g, final_g=final_g,
             w_in_e=w_in_e, w_out_e=w_out_e, lam_q1=lam_q1, lam_k1=lam_k1, lam_q2=lam_q2,
             lam_k2=lam_k2, subln_g=subln_g, mu_b=mu_b, w0=w0, w_up=w_up, a0=a0, a_up=a_up,
             g_up=g_up, k_k=k_k, k_a=k_a, r_k=r_k, lnx_g=lnx_g, lnx_b=lnx_b, w_in_o=w_in_o,
             lnv_g=lnv_g, lnv_b=lnv_b, w_s=w_s, b_s=b_s, w_out_o=w_out_o, ffn_gate=ffn_gate,
             ffn_up=ffn_up, ffn_down=ffn_down, router_w=router_w, router_b=router_b,
             exp_gate=exp_gate, exp_up=exp_up, exp_down=exp_down)
    return _run(x_prompt, x_sample, cache_k, cache_v, state_wkv, state_shift, page_table,
                c_prompt, c_sample, p)
```

```python
import functools
import math
from typing import NamedTuple

import jax
import jax.numpy as jnp
from jax import lax
from jax.experimental import pallas as pl
from jax.experimental.pallas import tpu as pltpu

F32 = jnp.float32
BF16 = jnp.bfloat16

D_MODEL = 1024
DEPTH = 4
H_A, DK_A, DV_A = 8, 32, 64
QK_A = H_A * 2 * DK_A
W_A = H_A * DV_A
A_COLS = 2 * QK_A + W_A
H_B, N_B = 8, 64
W_B = H_B * N_B
LORA_W, LORA_A, LORA_G = 64, 64, 128
LORA_COLS = LORA_W + LORA_A + LORA_G
B_COLS = 3 * W_B + LORA_COLS
IN_COLS_EVEN = A_COLS + B_COLS
CHUNK, G_C = 128, 8
E_C = 2 * D_MODEL
CG = E_C // G_C
D_FF = 2816
N_EXPERTS, D_FF_E = 8, 1408
PAGE_SIZE = 128
RMS_EPS, LN_EPS, LNX_EPS = 1e-6, 1e-5, 64e-5
NEG_INF = -1e30
ATTN_SCALE = DK_A ** -0.5
DECAY_SCALE = math.exp(-0.5)
INV_SQRT2 = 2.0 ** -0.5

LANES = 128
SUBLANES = 8
MIB = 1 << 20
VMEM_LIMIT = 56 * MIB

ROW_TILE = 256
ATTN_TQ, ATTN_TK = 256, 512
PAGES_PER_STEP = 8
SCAN_TC = 16
CHAINS = 64


class _Group(NamedTuple):
    n: int
    t: int
    tm: int

    @property
    def rows(self):
        return self.n * self.t

    @property
    def tiles(self):
        return self.rows // self.tm

    @property
    def tiles_per_seq(self):
        return max(self.t // self.tm, 1)

    @property
    def mod_rows(self):
        return 1 if self.t >= self.tm else self.tm


def _cparams(*sem):
    return pltpu.CompilerParams(dimension_semantics=sem, vmem_limit_bytes=VMEM_LIMIT)


def _row_spec(tm, cols):
    return pl.BlockSpec((tm, cols), lambda i: (i, 0))


def _full_spec(shape):
    zeros = (0,) * len(shape)
    return pl.BlockSpec(shape, lambda *_: zeros)


def _mod_spec(grp):
    tps = grp.tiles_per_seq
    return pl.BlockSpec((None, grp.mod_rows, D_MODEL), lambda i: (i // tps, 0, 0))


def _expand_mod(m, grp):
    if grp.mod_rows == 1:
        return m[:, None, :]
    return jnp.repeat(m, grp.t, axis=0)[None]


def _dot(a, b):
    return jnp.dot(a, b, preferred_element_type=F32)


def _dot_nt(a, b):
    return lax.dot_general(a, b, (((1,), (1,)), ((), ())), preferred_element_type=F32)


def _split_bf16(x):
    hi = x.astype(BF16)
    lo = (x - hi.astype(F32)).astype(BF16)
    return hi, lo


def _segsum(x, seg):
    hi, lo = _split_bf16(x)
    return _dot(hi, seg) + _dot(lo, seg)


def _silu(x):
    return x * jax.nn.sigmoid(x)


def _mod_kernel(c_ref, w_ref, b_ref, o_ref):
    a_hi, a_lo = _split_bf16(_silu(c_ref[...]))
    w_hi, w_lo = _split_bf16(w_ref[...])
    o_ref[...] = _dot(a_hi, w_hi) + _dot(a_hi, w_lo) + _dot(a_lo, w_hi) + b_ref[...]


def _mod_call(c_all, w_mod, b_mod):
    rows = c_all.shape[0]
    tn = D_MODEL
    return pl.pallas_call(
        _mod_kernel,
        grid=(DEPTH, 6 * D_MODEL // tn),
        in_specs=[
            pl.BlockSpec((rows, D_MODEL), lambda i, j: (0, 0)),
            pl.BlockSpec((None, D_MODEL, tn), lambda i, j: (i, 0, j)),
            pl.BlockSpec((None, 1, tn), lambda i, j: (i, 0, j)),
        ],
        out_specs=pl.BlockSpec((None, rows, tn), lambda i, j: (i, 0, j)),
        out_shape=jax.ShapeDtypeStruct((DEPTH, rows, 6 * D_MODEL), F32),
        compiler_params=_cparams("parallel", "parallel"),
        name="adaln_mod",
    )(c_all, w_mod, b_mod[:, None, :])


def _norm_mod_kernel(x_ref, g_ref, sc_ref, sh_ref, o_ref):
    x = x_ref[...]
    y = x * lax.rsqrt(jnp.mean(x * x, axis=-1, keepdims=True) + RMS_EPS) * g_ref[...]
    o_ref[...] = (y * (1.0 + sc_ref[...]) + sh_ref[...]).astype(o_ref.dtype)


def _norm_mod_call(x, g, sc, sh, grp):
    return pl.pallas_call(
        _norm_mod_kernel,
        grid=(grp.tiles,),
        in_specs=[_row_spec(grp.tm, D_MODEL), _full_spec((1, D_MODEL)), _mod_spec(grp), _mod_spec(grp)],
        out_specs=_row_spec(grp.tm, D_MODEL),
        out_shape=jax.ShapeDtypeStruct((grp.rows, D_MODEL), BF16),
        compiler_params=_cparams("parallel"),
        name="norm_mod",
    )(x, g[None, :], sc, sh)


def _norm_mod_router_kernel(x_ref, g_ref, sc_ref, sh_ref, wr_ref, br_ref, o_ref, logit_ref):
    x = x_ref[...]
    y = x * lax.rsqrt(jnp.mean(x * x, axis=-1, keepdims=True) + RMS_EPS) * g_ref[...]
    hn = y * (1.0 + sc_ref[...]) + sh_ref[...]
    o_ref[...] = hn.astype(o_ref.dtype)
    h_hi, h_lo = _split_bf16(hn)
    w_hi, w_lo = _split_bf16(wr_ref[...])
    logit_ref[...] = _dot(h_hi, w_hi) + _dot(h_hi, w_lo) + _dot(h_lo, w_hi) + br_ref[...]


def _norm_mod_router_call(x, g, sc, sh, wr, br, grp):
    wr_pad = jnp.zeros((D_MODEL, LANES), F32).at[:, :N_EXPERTS].set(wr)
    br_pad = jnp.full((1, LANES), -jnp.inf, F32).at[0, :N_EXPERTS].set(br)
    return pl.pallas_call(
        _norm_mod_router_kernel,
        grid=(grp.tiles,),
        in_specs=[_row_spec(grp.tm, D_MODEL), _full_spec((1, D_MODEL)), _mod_spec(grp), _mod_spec(grp),
                  _full_spec((D_MODEL, LANES)), _full_spec((1, LANES))],
        out_specs=[_row_spec(grp.tm, D_MODEL), _row_spec(grp.tm, LANES)],
        out_shape=[jax.ShapeDtypeStruct((grp.rows, D_MODEL), BF16),
                   jax.ShapeDtypeStruct((grp.rows, LANES), F32)],
        compiler_params=_cparams("parallel"),
        name="norm_mod_router",
    )(x, g[None, :], sc, sh, wr_pad, br_pad)


def _final_norm_kernel(x_ref, g_ref, o_ref):
    x = x_ref[...]
    o_ref[...] = x * lax.rsqrt(jnp.mean(x * x, axis=-1, keepdims=True) + RMS_EPS) * g_ref[...]


def _final_norm_call(x, g, grp):
    return pl.pallas_call(
        _final_norm_kernel,
        grid=(grp.tiles,),
        in_specs=[_row_spec(grp.tm, D_MODEL), _full_spec((1, D_MODEL))],
        out_specs=_row_spec(grp.tm, D_MODEL),
        out_shape=jax.ShapeDtypeStruct((grp.rows, D_MODEL), F32),
        compiler_params=_cparams("parallel"),
        name="final_norm",
    )(x, g[None, :])


def _inproj_even_kernel(a_ref, w_ref, q_ref, k_ref, v_ref, kb_ref, vb_ref, pb_ref):
    a = a_ref[...]
    q_ref[...] = (_dot(a, w_ref[:, :QK_A]) * ATTN_SCALE).astype(BF16)
    k = _dot(a, w_ref[:, QK_A:2 * QK_A])
    k_ref[...] = k
    kb_ref[...] = k.astype(BF16)
    v = _dot(a, w_ref[:, 2 * QK_A:A_COLS])
    v_ref[...] = v
    vb_ref[...] = v.astype(BF16)
    half = B_COLS // 2
    pb_ref[:, :half] = _dot(a, w_ref[:, A_COLS:A_COLS + half])
    pb_ref[:, half:] = _dot(a, w_ref[:, A_COLS + half:])


def _inproj_even_call(hn, w, grp):
    tm, rows = grp.tm, grp.rows
    sds = jax.ShapeDtypeStruct
    return pl.pallas_call(
        _inproj_even_kernel,
        grid=(grp.tiles,),
        in_specs=[_row_spec(tm, D_MODEL), _full_spec((D_MODEL, IN_COLS_EVEN))],
        out_specs=[_row_spec(tm, QK_A), _row_spec(tm, QK_A), _row_spec(tm, W_A),
                   _row_spec(tm, QK_A), _row_spec(tm, W_A), _row_spec(tm, B_COLS)],
        out_shape=[sds((rows, QK_A), BF16), sds((rows, QK_A), F32), sds((rows, W_A), F32),
                   sds((rows, QK_A), BF16), sds((rows, W_A), BF16), sds((rows, B_COLS), F32)],
        compiler_params=_cparams("parallel"),
        name="inproj_even",
    )(hn, w)


def _lambda(lq_ref, lk_ref, lam_init):
    t = jnp.sum(lq_ref[...] * lk_ref[...], axis=1, keepdims=True)
    e = jnp.exp(t)
    return e[0:1] - e[1:2] + lam_init


def _attn_prompt_kernel(lq_ref, lk_ref, g_ref, qf_ref, q_ref, k_ref, kf_ref, vt_ref, o_ref,
                        m_scr, l_scr, acc_scr, *, tq, tk, lam_init):
    hp = pl.program_id(1)
    qi = pl.program_id(2)
    lane = lax.broadcasted_iota(jnp.int32, (1, LANES), 1)
    q = q_ref[...]
    q_aug = []
    for i in range(4):
        q_map = jnp.where(lane // DK_A == i, q, jnp.zeros_like(q))
        feat = jnp.broadcast_to(qf_ref[pl.ds(2 * hp + i // 2, 1), :], (tq, LANES)).astype(BF16)
        q_aug.append(jnp.concatenate([q_map, feat], axis=1))
    m_scr[...] = jnp.full(m_scr.shape, NEG_INF, F32)
    l_scr[...] = jnp.zeros(l_scr.shape, F32)
    acc_scr[...] = jnp.zeros(acc_scr.shape, F32)
    row = lax.broadcasted_iota(jnp.int32, (tk, 1), 0)
    col = lax.broadcasted_iota(jnp.int32, (1, tq), 1)
    q0 = qi * tq

    def tile(kt, masked):
        k0 = pl.multiple_of(kt * tk, tk)
        k_aug = jnp.concatenate([k_ref[pl.ds(k0, tk), :], kf_ref[pl.ds(k0, tk), :]], axis=1)
        vt = vt_ref[kt]
        if masked:
            keep = (k0 + row) <= (q0 + col)
        for i in range(4):
            h = i // 2
            s = _dot_nt(k_aug, q_aug[i])
            if masked:
                s = jnp.where(keep, s, NEG_INF)
            m_old = m_scr[i]
            m_new = jnp.maximum(m_old, jnp.max(s, axis=0, keepdims=True))
            alpha = jnp.exp(m_old - m_new)
            p = jnp.exp(s - m_new)
            l_scr[i] = alpha * l_scr[i] + jnp.sum(p, axis=0, keepdims=True)
            acc_scr[i] = alpha * acc_scr[i] + _dot(vt[h * DV_A:(h + 1) * DV_A, :], p.astype(BF16))
            m_scr[i] = m_new

    n_full = q0 // tk

    def body(kt, carry):
        tile(kt, False)
        return carry

    lax.fori_loop(0, n_full, body, 0)
    tile(n_full, True)

    lam = _lambda(lq_ref, lk_ref, lam_init)
    heads = []
    for h in range(2):
        oh = acc_scr[2 * h] / l_scr[2 * h] - lam * (acc_scr[2 * h + 1] / l_scr[2 * h + 1])
        ms = jnp.mean(oh * oh, axis=0, keepdims=True)
        heads.append(oh * lax.rsqrt(ms + RMS_EPS))
    o = jnp.concatenate(heads, axis=0).T
    o_ref[...] = (o * g_ref[...] * (1.0 - lam_init)).astype(o_ref.dtype)


def _attn_prompt_call(q, k, v, slopes, lq, lk, g2, n, t, lam_init):
    tq, tk = ATTN_TQ, min(ATTN_TK, t)
    tq = min(tq, tk)
    kern = functools.partial(_attn_prompt_kernel, tq=tq, tk=tk, lam_init=lam_init)
    pos = jnp.arange(t)
    k_feat = jnp.zeros((t, LANES), F32).at[:, 0].set(pos // 256).at[:, 1].set(pos % 256).astype(BF16)
    q_feat = jnp.zeros((H_A, LANES), F32).at[:, 0].set(256.0 * slopes).at[:, 1].set(slopes)
    v_t = v.reshape(n, t // tk, tk, W_A).transpose(0, 1, 3, 2)
    qspec = pl.BlockSpec((None, tq, LANES), lambda b, hp, qi: (b, qi, hp))
    return pl.pallas_call(
        kern,
        grid=(n, H_A // 2, t // tq),
        in_specs=[_full_spec((2, DK_A)), _full_spec((2, DK_A)), _full_spec((1, LANES)),
                  _full_spec((H_A, LANES)), qspec,
                  pl.BlockSpec((None, t, LANES), lambda b, hp, qi: (b, 0, hp)),
                  _full_spec((t, LANES)),
                  pl.BlockSpec((None, t // tk, LANES, tk), lambda b, hp, qi: (b, 0, hp, 0))],
        out_specs=qspec,
        out_shape=jax.ShapeDtypeStruct((n, t, W_A), BF16),
        scratch_shapes=[pltpu.VMEM((4, 1, tq), F32), pltpu.VMEM((4, 1, tq), F32),
                        pltpu.VMEM((4, DV_A, tq), F32)],
        compiler_params=_cparams("parallel", "parallel", "parallel"),
        name="diff_attn_prompt",
    )(lq, lk, g2, q_feat, q.reshape(n, t, QK_A), k.reshape(n, t, QK_A), k_feat, v_t)


def _attn_sample_kernel(pt_ref, lq_ref, lk_ref, g_ref, q_ref, kn_ref, vn_ref, *rest,
                        pages, steps, past_len, t_new, lam_init):
    k_pages, v_pages = rest[:pages], rest[pages:2 * pages]
    o_ref, m_scr, l_scr, acc_scr = rest[2 * pages:]
    step = pl.program_id(1)
    hr = 2 * t_new
    row = lax.broadcasted_iota(jnp.int32, (H_A * hr, 1), 0)
    slope = jnp.exp2(-(row // hr + 1).astype(F32))

    @pl.when(step == 0)
    def _():
        m_scr[...] = jnp.full(m_scr.shape, NEG_INF, F32)
        l_scr[...] = jnp.zeros(l_scr.shape, F32)
        acc_scr[...] = jnp.zeros(acc_scr.shape, F32)

    def update(scores, values_t, keep):
        s = jnp.concatenate(scores, axis=0)
        if keep is not None:
            s = jnp.where(keep, s, NEG_INF)
        m_old = m_scr[...]
        m_new = jnp.maximum(m_old, jnp.max(s, axis=1, keepdims=True))
        alpha = jnp.exp(m_old - m_new)
        p = jnp.exp(s - m_new)
        l_scr[...] = alpha * l_scr[...] + jnp.sum(p, axis=1, keepdims=True)
        p = p.astype(BF16)
        pv = jnp.concatenate([_dot_nt(p[h * hr:(h + 1) * hr], values_t[h]) for h in range(H_A)], axis=0)
        acc_scr[...] = alpha * acc_scr[...] + pv
        m_scr[...] = m_new

    col = lax.broadcasted_iota(jnp.int32, (1, pages * PAGE_SIZE), 1)
    k_rel = (step * (pages * PAGE_SIZE) - past_len + col).astype(F32)
    scores, values_t = [], []
    for h in range(H_A):
        k_t = jnp.concatenate([kp[h] for kp in k_pages], axis=1).astype(BF16)
        values_t.append(jnp.concatenate([vp[h] for vp in v_pages], axis=1).astype(BF16))
        scores.append(_dot(q_ref[h], k_t))
    update([s + slope[h * hr:(h + 1) * hr] * k_rel for h, s in enumerate(scores)], values_t, None)

    @pl.when(step == steps - 1)
    def _():
        new_col = lax.broadcasted_iota(jnp.int32, (1, PAGE_SIZE), 1)
        keep = new_col <= row % t_new
        bias = slope * new_col.astype(F32)
        update([_dot(q_ref[h], kn_ref[h]) + bias[h * hr:(h + 1) * hr] for h in range(H_A)],
               [vn_ref[h] for h in range(H_A)], keep)
        lam = _lambda(lq_ref, lk_ref, lam_init)
        a = acc_scr[...] / l_scr[...]
        for h in range(H_A):
            oh = a[h * hr:h * hr + t_new] - lam * a[h * hr + t_new:(h + 1) * hr]
            ms = jnp.mean(oh * oh, axis=1, keepdims=True)
            o_ref[h] = oh * lax.rsqrt(ms + RMS_EPS) * g_ref[...] * (1.0 - lam_init)


def _attn_sample_call(q, k_new, v_new, cache_k, cache_v, page_table, j, lq, lk, g1, n, t, lam_init):
    n_pages = page_table.shape[1]
    pages = PAGES_PER_STEP
    steps = n_pages // pages
    n_pool = cache_k.shape[1]
    ck = cache_k.transpose(0, 1, 3, 4, 2)
    cv = cache_v.transpose(0, 1, 3, 4, 2)
    kern = functools.partial(_attn_sample_kernel, pages=pages, steps=steps,
                             past_len=n_pages * PAGE_SIZE, t_new=t, lam_init=lam_init)
    qh = jnp.einsum('nthmd,km->nhktmd', q.reshape(n, t, H_A, 2, DK_A), jnp.eye(2, dtype=q.dtype))
    qh = qh.reshape(n, H_A, 2 * t, 2 * DK_A)

    def new_page(x):
        x = x.reshape(n, t, H_A, DV_A).transpose(0, 2, 3, 1)
        return jnp.pad(x, ((0, 0), (0, 0), (0, 0), (0, PAGE_SIZE - t)))

    def page_spec(i):
        return pl.BlockSpec((None, None, H_A, DV_A, PAGE_SIZE),
                            lambda b, s, pt: (j, pt[b, s * pages + i], 0, 0, 0))

    small = lambda shape: pl.BlockSpec(shape, lambda b, s, pt: (0,) * len(shape))
    head_spec = lambda r: pl.BlockSpec((None, H_A, r, DV_A), lambda b, s, pt: (b, 0, 0, 0))
    new_spec = pl.BlockSpec((None, H_A, DV_A, PAGE_SIZE), lambda b, s, pt: (b, 0, 0, 0))
    grid_spec = pltpu.PrefetchScalarGridSpec(
        num_scalar_prefetch=1,
        grid=(n, steps),
        in_specs=[small((2, DK_A)), small((2, DK_A)), small((1, DV_A)),
                  head_spec(2 * t), new_spec, new_spec]
                 + [page_spec(i) for i in range(pages)] * 2,
        out_specs=head_spec(t),
        scratch_shapes=[pltpu.VMEM((H_A * 2 * t, 1), F32), pltpu.VMEM((H_A * 2 * t, 1), F32),
                        pltpu.VMEM((H_A * 2 * t, DV_A), F32)],
    )
    o = pl.pallas_call(
        kern,
        grid_spec=grid_spec,
        out_shape=jax.ShapeDtypeStruct((n, H_A, t, DV_A), F32),
        compiler_params=_cparams("parallel", "arbitrary"),
        name="diff_attn_paged",
    )(page_table, lq, lk, g1, qh, new_page(k_new), new_page(v_new), *([ck] * pages), *([cv] * pages))
    return o.transpose(0, 2, 1, 3).reshape(n * t, W_A).astype(BF16)


def _rwkv_prep_kernel(pb_ref, first_ref, mu_ref, wl_ref, w0_ref, a0_ref, kk_w_ref, ka_w_ref, rk_ref,
                      seg_ref, r_o, w_o, k_o, v_o, kk_o, b_o, bonus_o, g_o, *, period):
    pb = pb_ref[...]
    tm = pb.shape[0]
    row = lax.broadcasted_iota(jnp.int32, (tm, 1), 0)
    prev = jnp.where(row % period == 0, first_ref[...], pltpu.roll(pb, 1, 0))
    xb = pb + (prev - pb) * mu_ref[...]
    r = xb[:, :W_B]
    kb = xb[:, W_B:2 * W_B]
    vb = xb[:, 2 * W_B:3 * W_B]
    z = xb[:, 3 * W_B:]
    ll = lax.broadcasted_iota(jnp.int32, (1, LORA_COLS), 1)
    zz = jnp.where(ll < LORA_W, jnp.tanh(z), jnp.where(ll < LORA_W + LORA_A, z, jax.nn.sigmoid(z)))
    lo = _dot(zz.astype(BF16), wl_ref[...])
    decay = jnp.exp(-DECAY_SCALE * jax.nn.sigmoid(w0_ref[...] + lo[:, :W_B]))
    a = jax.nn.sigmoid(a0_ref[...] + lo[:, W_B:2 * W_B])
    seg = seg_ref[...]
    kk = kb * kk_w_ref[...]
    kk = kk / jnp.maximum(jnp.sqrt(_segsum(kk * kk, seg)), 1e-12)
    k_adj = kb * (1.0 + (a - 1.0) * ka_w_ref[...])
    r_o[...] = r
    w_o[...] = decay
    k_o[...] = k_adj
    v_o[...] = vb
    kk_o[...] = kk
    b_o[...] = kk * a
    bonus_o[...] = _segsum(r * k_adj * rk_ref[...], seg) * vb
    g_o[...] = lo[:, 2 * W_B:]


def _rwkv_prep_call(pb, first, p, j, seg, grp):
    tm, rows = grp.tm, grp.rows
    period = tm if grp.mod_rows == 1 else grp.t
    first_rows = first.shape[1]
    vec = lambda x: x[j].reshape(1, -1)
    wl = jnp.zeros((LORA_COLS, 3 * W_B), F32)
    wl = wl.at[:LORA_W, :W_B].set(p['w_up'][j])
    wl = wl.at[LORA_W:LORA_W + LORA_A, W_B:2 * W_B].set(p['a_up'][j])
    wl = wl.at[LORA_W + LORA_A:, 2 * W_B:].set(p['g_up'][j])
    outs = [jax.ShapeDtypeStruct((rows, W_B), F32)] * 8
    return pl.pallas_call(
        functools.partial(_rwkv_prep_kernel, period=period),
        grid=(grp.tiles,),
        in_specs=[_row_spec(tm, B_COLS),
                  pl.BlockSpec((None, first_rows, B_COLS), lambda i: (i, 0, 0)),
                  _full_spec((1, B_COLS)), _full_spec((LORA_COLS, 3 * W_B)),
                  _full_spec((1, W_B)), _full_spec((1, W_B)), _full_spec((1, W_B)),
                  _full_spec((1, W_B)), _full_spec((1, W_B)), _full_spec((W_B, W_B))],
        out_specs=[_row_spec(tm, W_B)] * 8,
        out_shape=outs,
        compiler_params=_cparams("parallel"),
        name="rwkv_prep",
    )(pb, first, vec(p['mu_b']), wl.astype(BF16), vec(p['w0']), vec(p['a0']), vec(p['k_k']),
      vec(p['k_a']), vec(p['r_k']), seg)


def _rwkv_scan_kernel(kk_ref, w_ref, b_ref, k_ref, r_ref, v_ref, s0_ref, y_ref, s_ref, *, tc):
    @pl.when(pl.program_id(1) == 0)
    def _():
        s_ref[...] = s0_ref[...]

    n_acc = 4

    def step(t, carry):
        parts = [None] * n_acc
        for k in range(N_B):
            term = s_ref[k] * kk_ref[t, pl.ds(k, 1), :]
            parts[k % n_acc] = term if parts[k % n_acc] is None else parts[k % n_acc] + term
        sa = (parts[0] + parts[1]) + (parts[2] + parts[3])
        v_t = v_ref[t]
        parts = [None] * n_acc
        for k in range(N_B):
            s_new = (s_ref[k] * w_ref[t, pl.ds(k, 1), :] - sa * b_ref[t, pl.ds(k, 1), :]
                     + v_t * k_ref[t, pl.ds(k, 1), :])
            s_ref[k] = s_new
            term = s_new * r_ref[t, pl.ds(k, 1), :]
            parts[k % n_acc] = term if parts[k % n_acc] is None else parts[k % n_acc] + term
        y_ref[t] = (parts[0] + parts[1]) + (parts[2] + parts[3])
        return carry

    lax.fori_loop(0, tc, step, 0)


def _to_chain_layout(x, n, t):
    groups = n * H_B // CHAINS
    x = x.reshape(n, t, H_B, N_B).transpose(1, 3, 0, 2).reshape(t, N_B, groups, CHAINS)
    x = x.transpose(2, 0, 1, 3)
    return jnp.concatenate([x, x], axis=-1)


def _value_to_chain_layout(v, n, t):
    groups = n * H_B // CHAINS
    v = v.reshape(n, t, H_B, N_B).transpose(1, 3, 0, 2).reshape(t, 2, N_B // 2, groups, CHAINS)
    return v.transpose(3, 0, 2, 1, 4).reshape(groups, t, N_B // 2, 2 * CHAINS)


def _value_from_chain_layout(y, n, t):
    groups = n * H_B // CHAINS
    y = y.reshape(groups, t, N_B // 2, 2, CHAINS).transpose(1, 3, 2, 0, 4)
    y = y.reshape(t, N_B, n, H_B).transpose(2, 0, 3, 1)
    return y.reshape(n * t, W_B)


def _state_to_chain_layout(s, n):
    groups = n * H_B // CHAINS
    s = s.reshape(groups, CHAINS, 2, N_B // 2, N_B)
    return s.transpose(0, 4, 3, 2, 1).reshape(groups, N_B, N_B // 2, 2 * CHAINS)


def _state_from_chain_layout(s, n):
    groups = n * H_B // CHAINS
    s = s.reshape(groups, N_B, N_B // 2, 2, CHAINS).transpose(0, 4, 3, 2, 1)
    return s.reshape(n, H_B, N_B, N_B)


def _chain_in_kernel(x_ref, o_ref, z_scr, *, per_key, tt):
    n_seq = x_ref.shape[0]
    for s in range(n_seq):
        z_scr[s] = x_ref[s].T
    for j in range(o_ref.shape[0] // tt):
        rows = []
        for half in range(2):
            start = j if per_key else half * (N_B // 2) + j
            rows += [z_scr[s, pl.ds(start, H_B, stride=N_B), :] for s in range(n_seq)]
        tile = jnp.concatenate(rows, axis=0).T
        o_ref[pl.ds(j, tt, stride=o_ref.shape[0] // tt), :] = tile


def _chain_in_call(x, n, t, per_key):
    tt = LANES
    rows = N_B if per_key else N_B // 2
    out = pl.pallas_call(
        functools.partial(_chain_in_kernel, per_key=per_key, tt=tt),
        grid=(t // tt,),
        in_specs=[pl.BlockSpec((n, tt, W_B), lambda i: (0, i, 0))],
        out_specs=pl.BlockSpec((tt * rows, 2 * CHAINS), lambda i: (i, 0)),
        out_shape=jax.ShapeDtypeStruct((t * rows, 2 * CHAINS), F32),
        scratch_shapes=[pltpu.VMEM((n, W_B, tt), F32)],
        compiler_params=_cparams("parallel"),
        name="chain_layout_in",
    )(x.reshape(n, t, W_B))
    return out.reshape(1, t, rows, 2 * CHAINS)


def _chain_out_kernel(y_ref, o_ref, z_scr, *, tt):
    n_seq = o_ref.shape[0]
    rows = N_B // 2
    for j in range(rows):
        tile = y_ref[pl.ds(j, tt, stride=rows), :].T
        for half in range(2):
            for s in range(n_seq):
                r0 = (half * n_seq + s) * H_B
                z_scr[s, pl.ds(half * rows + j, H_B, stride=N_B), :] = tile[r0:r0 + H_B]
    for s in range(n_seq):
        o_ref[s] = z_scr[s].T


def _chain_out_call(y, n, t):
    tt = LANES
    rows = N_B // 2
    out = pl.pallas_call(
        functools.partial(_chain_out_kernel, tt=tt),
        grid=(t // tt,),
        in_specs=[pl.BlockSpec((tt * rows, 2 * CHAINS), lambda i: (i, 0))],
        out_specs=pl.BlockSpec((n, tt, W_B), lambda i: (0, i, 0)),
        out_shape=jax.ShapeDtypeStruct((n, t, W_B), F32),
        scratch_shapes=[pltpu.VMEM((n, W_B, tt), F32)],
        compiler_params=_cparams("parallel"),
        name="chain_layout_out",
    )(y.reshape(t * rows, 2 * CHAINS))
    return out.reshape(n * t, W_B)


def _rwkv_scan_call(kk, w, b, k, r, v, s0, n, t):
    groups = n * H_B // CHAINS
    tc = min(SCAN_TC, t)
    vec_spec = pl.BlockSpec((None, tc, N_B, 2 * CHAINS), lambda g, i: (g, i, 0, 0))
    val_spec = pl.BlockSpec((None, tc, N_B // 2, 2 * CHAINS), lambda g, i: (g, i, 0, 0))
    st_spec = pl.BlockSpec((None, N_B, N_B // 2, 2 * CHAINS), lambda g, i: (g, 0, 0, 0))
    in_kernel_layout = groups == 1 and t % LANES == 0
    if in_kernel_layout:
        vecs = [_chain_in_call(x, n, t, True) for x in (kk, w, b, k, r)]
        val = _chain_in_call(v, n, t, False)
    else:
        vecs = [_to_chain_layout(x, n, t) for x in (kk, w, b, k, r)]
        val = _value_to_chain_layout(v, n, t)
    y, s_fin = pl.pallas_call(
        functools.partial(_rwkv_scan_kernel, tc=tc),
        grid=(groups, t // tc),
        in_specs=[vec_spec] * 5 + [val_spec, st_spec],
        out_specs=[val_spec, st_spec],
        out_shape=[jax.ShapeDtypeStruct((groups, t, N_B // 2, 2 * CHAINS), F32),
                   jax.ShapeDtypeStruct((groups, N_B, N_B // 2, 2 * CHAINS), F32)],
        compiler_params=_cparams("parallel", "arbitrary"),
        name="rwkv_scan",
    )(*vecs, val, _state_to_chain_layout(s0, n))
    y = _chain_out_call(y, n, t) if in_kernel_layout else _value_from_chain_layout(y, n, t)
    return y, _state_from_chain_layout(s_fin, n)


def _rwkv_post_kernel(y_ref, bonus_ref, g_ref, lg_ref, lb_ref, seg_ref, o_ref):
    y = y_ref[...]
    seg = seg_ref[...]
    yc = y - _segsum(y, seg) * (1.0 / N_B)
    var = _segsum(yc * yc, seg) * (1.0 / N_B)
    yn = yc * lax.rsqrt(var + LNX_EPS) * lg_ref[...] + lb_ref[...]
    o_ref[...] = ((yn + bonus_ref[...]) * g_ref[...]).astype(o_ref.dtype)


def _rwkv_post_call(y, bonus, g, lg, lb, seg, grp):
    tm = grp.tm
    return pl.pallas_call(
        _rwkv_post_kernel,
        grid=(grp.tiles,),
        in_specs=[_row_spec(tm, W_B)] * 3 + [_full_spec((1, W_B)), _full_spec((1, W_B)),
                                              _full_spec((W_B, W_B))],
        out_specs=_row_spec(tm, W_B),
        out_shape=jax.ShapeDtypeStruct((grp.rows, W_B), BF16),
        compiler_params=_cparams("parallel"),
        name="rwkv_post",
    )(y, bonus, g, lg[None, :], lb[None, :], seg)


def _outproj_even_kernel(oa_ref, ob_ref, w_ref, x_ref, g_ref, o_ref):
    mix = _dot(oa_ref[...], w_ref[:W_A, :]) + _dot(ob_ref[...], w_ref[W_A:, :])
    o_ref[...] = x_ref[...] + g_ref[...] * mix


def _outproj_even_call(oa, ob, w, x, gate, grp):
    tm = grp.tm
    return pl.pallas_call(
        _outproj_even_kernel,
        grid=(grp.tiles,),
        in_specs=[_row_spec(tm, W_A), _row_spec(tm, W_B), _full_spec((W_A + W_B, D_MODEL)),
                  _row_spec(tm, D_MODEL), _mod_spec(grp)],
        out_specs=_row_spec(tm, D_MODEL),
        out_shape=jax.ShapeDtypeStruct((grp.rows, D_MODEL), F32),
        compiler_params=_cparams("parallel"),
        name="outproj_even",
    )(oa, ob, w, x, gate)


def _swiglu_up_kernel(a_ref, wg_ref, wu_ref, o_ref):
    a = a_ref[...]
    o_ref[...] = (_silu(_dot(a, wg_ref[...])) * _dot(a, wu_ref[...])).astype(o_ref.dtype)


def _swiglu_up_call(hn, wg, wu, grp):
    tm = grp.tm
    tn = D_FF // 2
    return pl.pallas_call(
        _swiglu_up_kernel,
        grid=(D_FF // tn, grp.tiles),
        in_specs=[pl.BlockSpec((tm, D_MODEL), lambda c, i: (i, 0)),
                  pl.BlockSpec((D_MODEL, tn), lambda c, i: (0, c)),
                  pl.BlockSpec((D_MODEL, tn), lambda c, i: (0, c))],
        out_specs=pl.BlockSpec((tm, tn), lambda c, i: (i, c)),
        out_shape=jax.ShapeDtypeStruct((grp.rows, D_FF), BF16),
        compiler_params=_cparams("parallel", "parallel"),
        name="swiglu_up",
    )(hn, wg, wu)


def _down_res_kernel(h_ref, w_ref, x_ref, g_ref, o_ref):
    o_ref[...] = x_ref[...] + g_ref[...] * _dot(h_ref[...], w_ref[...])


def _down_res_call(h, w, x, gate, grp):
    tm = grp.tm
    kdim = h.shape[1]
    return pl.pallas_call(
        _down_res_kernel,
        grid=(grp.tiles,),
        in_specs=[_row_spec(tm, kdim), _full_spec((kdim, D_MODEL)), _row_spec(tm, D_MODEL), _mod_spec(grp)],
        out_specs=_row_spec(tm, D_MODEL),
        out_shape=jax.ShapeDtypeStruct((grp.rows, D_MODEL), F32),
        compiler_params=_cparams("parallel"),
        name="down_res",
    )(h, w, x, gate)


def _gelu(x):
    return 0.5 * x * (1.0 + lax.erf(x * INV_SQRT2))


def _gmlp_u_kernel(a_ref, w_ref, o_ref):
    o_ref[...] = _gelu(_dot(a_ref[...], w_ref[...]))


def _gmlp_v_kernel(a_ref, w_ref, lg_ref, lb_ref, vb_ref, vf_ref):
    v = _gelu(_dot(a_ref[...], w_ref[...]))
    vc = v - jnp.mean(v, axis=-1, keepdims=True)
    vn = vc * lax.rsqrt(jnp.mean(vc * vc, axis=-1, keepdims=True) + LN_EPS) * lg_ref[...] + lb_ref[...]
    vb_ref[...] = vn.astype(BF16)
    vf_ref[...] = vn


def _gmlp_in_call(hn, w, lg, lb, grp):
    tm, rows = grp.tm, grp.rows
    wspec = lambda half: pl.BlockSpec((D_MODEL, E_C), lambda i: (0, half))
    u = pl.pallas_call(
        _gmlp_u_kernel,
        grid=(grp.tiles,),
        in_specs=[_row_spec(tm, D_MODEL), wspec(0)],
        out_specs=_row_spec(tm, E_C),
        out_shape=jax.ShapeDtypeStruct((rows, E_C), F32),
        compiler_params=_cparams("parallel"),
        name="gmlp_u",
    )(hn, w)
    vb, vf = pl.pallas_call(
        _gmlp_v_kernel,
        grid=(grp.tiles,),
        in_specs=[_row_spec(tm, D_MODEL), wspec(1), _full_spec((1, E_C)), _full_spec((1, E_C))],
        out_specs=[_row_spec(tm, E_C), _row_spec(tm, E_C)],
        out_shape=[jax.ShapeDtypeStruct((rows, E_C), BF16), jax.ShapeDtypeStruct((rows, E_C), F32)],
        compiler_params=_cparams("parallel"),
        name="gmlp_v",
    )(hn, w, lg[None, :], lb[None, :])
    return u, vb, vf


def _gmlp_out_kernel(u_ref, v_ref, ws_ref, bs_ref, wo_ref, x_ref, g_ref, o_ref, z_scr, *, lm):
    tm = u_ref.shape[0]
    for c in range(tm // lm):
        r0 = c * lm
        for gi in range(G_C):
            c0 = gi * CG
            mixed = _dot(ws_ref[gi], v_ref[r0:r0 + lm, c0:c0 + CG]) + bs_ref[:, gi:gi + 1]
            z_scr[r0:r0 + lm, c0:c0 + CG] = (u_ref[r0:r0 + lm, c0:c0 + CG] * mixed).astype(BF16)
    o_ref[...] = x_ref[...] + g_ref[...] * _dot(z_scr[...], wo_ref[...])


def _gmlp_out_call(u, vb, ws, bs, wo, x, gate, grp):
    tm = grp.tm
    lm = ws.shape[1]
    return pl.pallas_call(
        functools.partial(_gmlp_out_kernel, lm=lm),
        grid=(grp.tiles,),
        in_specs=[_row_spec(tm, E_C), _row_spec(tm, E_C), _full_spec((G_C, lm, lm)), _full_spec((lm, G_C)),
                  _full_spec((E_C, D_MODEL)), _row_spec(tm, D_MODEL), _mod_spec(grp)],
        out_specs=_row_spec(tm, D_MODEL),
        out_shape=jax.ShapeDtypeStruct((grp.rows, D_MODEL), F32),
        scratch_shapes=[pltpu.VMEM((tm, E_C), BF16)],
        compiler_params=_cparams("parallel"),
        name="gmlp_out",
    )(u, vb, ws, bs, wo, x, gate)


def _spatial_weights(w_s, b_s, grp):
    length = min(grp.t, CHUNK)
    ws = jnp.tril(w_s[:, :length, :length])
    bs = b_s[:, :length]
    if length < CHUNK:
        reps = grp.tm // length
        ws = jnp.einsum('ab,gij->gaibj', jnp.eye(reps, dtype=ws.dtype), ws).reshape(
            G_C, reps * length, reps * length)
        bs = jnp.tile(bs, (1, reps))
    return ws.astype(BF16), bs.T


def _moe_kernel(a_ref, logit_ref, wg_ref, wu_ref, wd_ref, x_ref, g_ref, o_ref, gate_scr, acc_scr):
    e = pl.program_id(1)
    lane = lax.broadcasted_iota(jnp.int32, (1, LANES), 1)
    a = a_ref[...]

    @pl.when(e == 0)
    def _():
        logits = logit_ref[...]
        m1 = jnp.max(logits, axis=1, keepdims=True)
        i1 = jnp.min(jnp.where(logits == m1, lane, LANES), axis=1, keepdims=True)
        rest = jnp.where(lane == i1, -jnp.inf, logits)
        m2 = jnp.max(rest, axis=1, keepdims=True)
        i2 = jnp.min(jnp.where(rest == m2, lane, LANES), axis=1, keepdims=True)
        e2 = jnp.exp(m2 - m1)
        gate_scr[...] = jnp.where(lane == i1, 1.0 / (1.0 + e2), jnp.where(lane == i2, e2 / (1.0 + e2), 0.0))
        acc_scr[...] = jnp.zeros(acc_scr.shape, F32)

    h = (_silu(_dot(a, wg_ref[...])) * _dot(a, wu_ref[...])).astype(BF16)
    y = _dot(h, wd_ref[...])
    gate_e = jnp.sum(jnp.where(lane == e, gate_scr[...], 0.0), axis=1, keepdims=True)
    acc_scr[...] += gate_e * y

    @pl.when(e == N_EXPERTS - 1)
    def _():
        o_ref[...] = x_ref[...] + g_ref[...] * acc_scr[...]


def _moe_call(hn, logits, wg, wu, wd, j, x, gate, grp):
    tm = grp.tm
    tps = grp.tiles_per_seq
    row = lambda cols: pl.BlockSpec((tm, cols), lambda i, e: (i, 0))
    return pl.pallas_call(
        _moe_kernel,
        grid=(grp.tiles, N_EXPERTS),
        in_specs=[row(D_MODEL), row(LANES),
                  pl.BlockSpec((None, None, D_MODEL, D_FF_E), lambda i, e: (j, e, 0, 0)),
                  pl.BlockSpec((None, None, D_MODEL, D_FF_E), lambda i, e: (j, e, 0, 0)),
                  pl.BlockSpec((None, None, D_FF_E, D_MODEL), lambda i, e: (j, e, 0, 0)),
                  row(D_MODEL),
                  pl.BlockSpec((None, grp.mod_rows, D_MODEL), lambda i, e: (i // tps, 0, 0))],
        out_specs=row(D_MODEL),
        out_shape=jax.ShapeDtypeStruct((grp.rows, D_MODEL), F32),
        scratch_shapes=[pltpu.VMEM((tm, LANES), F32), pltpu.VMEM((tm, D_MODEL), F32)],
        compiler_params=_cparams("parallel", "arbitrary"),
        name="moe_experts",
    )(hn, logits, wg, wu, wd, x, gate)


def _forward(x, mods, p, wb, grp, cache):
    n, t = grp.n, grp.t
    seg = jnp.kron(jnp.eye(H_B, dtype=F32), jnp.ones((N_B, N_B), F32)).astype(BF16)
    slopes = jnp.exp2(-8.0 * jnp.arange(1, H_A + 1, dtype=F32) / H_A)
    new_k, new_v, new_wkv, new_shift, new_cv = [], [], [], [], []
    for i in range(DEPTH):
        j = i // 2
        sh1, sc1, g1, sh2, sc2, g2 = [_expand_mod(m, grp) for m in mods[i]]
        hn = _norm_mod_call(x, p['norm1_g'][i], sc1, sh1, grp)
        if i % 2 == 0:
            q, k, v, kb, vb, pb = _inproj_even_call(hn, wb['w_in_e'][j], grp)
            lam_init = 0.8 - 0.6 * math.exp(-0.3 * i)
            lq = jnp.stack([p['lam_q1'][j], p['lam_q2'][j]])
            lk = jnp.stack([p['lam_k1'][j], p['lam_k2'][j]])
            if cache is None:
                g2h = jnp.tile(p['subln_g'][j], 2)[None, :]
                oa = _attn_prompt_call(q, kb, vb, slopes, lq, lk, g2h, n, t, lam_init)
                wkv0 = jnp.zeros((n, H_B, N_B, N_B), F32)
                last = pb.reshape(grp.tiles, grp.tm, B_COLS)[:, -1, :]
                first = jnp.concatenate([jnp.zeros((1, B_COLS), F32), last[:-1]], axis=0)
                starts = (jnp.arange(grp.tiles) % grp.tiles_per_seq == 0)[:, None]
                first = jnp.where(starts, 0.0, first)[:, None, :]
            else:
                cache_k, cache_v, page_table, state_wkv, state_shift = cache
                oa = _attn_sample_call(q, kb, vb, cache_k, cache_v, page_table, j, lq, lk,
                                       p['subln_g'][j][None, :], n, t, lam_init)
                wkv0 = state_wkv[j]
                first = jnp.repeat(state_shift[j], t, axis=0)[None]
            r, w, kv, vv, kk, b, bonus, g = _rwkv_prep_call(pb, first, p, j, seg, grp)
            y, s_fin = _rwkv_scan_call(kk, w, b, kv, r, vv, wkv0, n, t)
            ob = _rwkv_post_call(y, bonus, g, p['lnx_g'][j], p['lnx_b'][j], seg, grp)
            new_k.append(k.reshape(n, t, H_A, 2 * DK_A))
            new_v.append(v.reshape(n, t, H_A, DV_A))
            new_wkv.append(s_fin)
            new_shift.append(pb.reshape(n, t, B_COLS)[:, -1])
            x = _outproj_even_call(oa.reshape(grp.rows, W_A), ob, wb['w_out_e'][j], x, g1, grp)
            hn = _norm_mod_call(x, p['norm2_g'][i], sc2, sh2, grp)
            h = _swiglu_up_call(hn, wb['ffn_gate'][j], wb['ffn_up'][j], grp)
            x = _down_res_call(h, wb['ffn_down'][j], x, g2, grp)
        else:
            u, vb, vf = _gmlp_in_call(hn, wb['w_in_o'][j], p['lnv_g'][j], p['lnv_b'][j], grp)
            new_cv.append(vf.reshape(n, t, E_C))
            ws, bs = _spatial_weights(p['w_s'][j], p['b_s'][j], grp)
            x = _gmlp_out_call(u, vb, ws, bs, wb['w_out_o'][j], x, g1, grp)
            hn, logits = _norm_mod_router_call(x, p['norm2_g'][i], sc2, sh2, p['router_w'][j],
                                               p['router_b'][j], grp)
            x = _moe_call(hn, logits, wb['exp_gate'], wb['exp_up'], wb['exp_down'], j, x, g2, grp)
    y = _final_norm_call(x, p['final_g'], grp)
    return (y.reshape(n, t, D_MODEL), jnp.stack(new_k), jnp.stack(new_v), jnp.stack(new_wkv),
            jnp.stack(new_shift), jnp.stack(new_cv))


_BF16_WEIGHTS = ('w_in_e', 'w_out_e', 'ffn_gate', 'ffn_up', 'ffn_down', 'w_in_o', 'w_out_o',
                 'exp_gate', 'exp_up', 'exp_down')


def _run(x_prompt, x_sample, cache_k, cache_v, state_wkv, state_shift, page_table, c_prompt, c_sample, p):
    n_p, t_p, _ = x_prompt.shape
    n_s, t_s, _ = x_sample.shape
    grp_p = _Group(n_p, t_p, min(ROW_TILE, t_p))
    grp_s = _Group(n_s, t_s, n_s * t_s)
    wb = {name: p[name].astype(BF16) for name in _BF16_WEIGHTS}
    mod = _mod_call(jnp.concatenate([c_prompt, c_sample], axis=0), p['w_mod'], p['b_mod'])
    mods_p = [jnp.split(mod[i, :n_p], 6, axis=-1) for i in range(DEPTH)]
    mods_s = [jnp.split(mod[i, n_p:], 6, axis=-1) for i in range(DEPTH)]
    y_p, k_p, v_p, wkv_p, sh_p, _ = _forward(x_prompt.reshape(n_p * t_p, D_MODEL), mods_p, p, wb, grp_p, None)
    y_s, k_s, v_s, wkv_s, sh_s, cv_s = _forward(
        x_sample.reshape(n_s * t_s, D_MODEL), mods_s, p, wb, grp_s,
        (cache_k, cache_v, page_table, state_wkv, state_shift))
    return (y_p, y_s, k_p, v_p, wkv_p, sh_p, k_s, v_s, wkv_s, sh_s, cv_s)


def kernel(x_prompt, x_sample, cache_k, cache_v, state_wkv, state_shift, page_table, c_prompt, c_sample,
           w_mod, b_mod, norm1_g, norm2_g, final_g, w_in_e, w_out_e, lam_q1, lam_k1, lam_q2, lam_k2,
           subln_g, mu_b, w0, w_up, a0, a_up, g_up, k_k, k_a, r_k, lnx_g, lnx_b, w_in_o, lnv_g, lnv_b,
           w_s, b_s, w_out_o, ffn_gate, ffn_up, ffn_down, router_w, router_b, exp_gate, exp_up, exp_down):
    p = dict(w_mod=w_mod, b_mod=b_mod, norm1_g=norm1_g, norm2_g=norm2_g, final_g=final_g,
             w_in_e=w_in_e, w_out_e=w_out_e, lam_q1=lam_q1, lam_k1=lam_k1, lam_q2=lam_q2,
             lam_k2=lam_k2, subln_g=subln_g, mu_b=mu_b, w0=w0, w_up=w_up, a0=a0, a_up=a_up,
             g_up=g_up, k_k=k_k, k_a=k_a, r_k=r_k, lnx_g=lnx_g, lnx_b=lnx_b, w_in_o=w_in_o,
             lnv_g=lnv_g, lnv_b=lnv_b, w_s=w_s, b_s=b_s, w_out_o=w_out_o, ffn_gate=ffn_gate,
             ffn_up=ffn_up, ffn_down=ffn_down, router_w=router_w, router_b=router_b,
             exp_gate=exp_gate, exp_up=exp_up, exp_down=exp_down)
    return _run(x_prompt, x_sample, cache_k, cache_v, state_wkv, state_shift, page_table,
                c_prompt, c_sample, p)
```

```python
import functools
import math
from typing import NamedTuple

import jax
import jax.numpy as jnp
from jax import lax
from jax.experimental import pallas as pl
from jax.experimental.pallas import tpu as pltpu

F32 = jnp.float32
BF16 = jnp.bfloat16

D_MODEL = 1024
DEPTH = 4
H_A, DK_A, DV_A = 8, 32, 64
QK_A = H_A * 2 * DK_A
W_A = H_A * DV_A
A_COLS = 2 * QK_A + W_A
H_B, N_B = 8, 64
W_B = H_B * N_B
LORA_W, LORA_A, LORA_G = 64, 64, 128
LORA_COLS = LORA_W + LORA_A + LORA_G
B_COLS = 3 * W_B + LORA_COLS
IN_COLS_EVEN = A_COLS + B_COLS
CHUNK, G_C = 128, 8
E_C = 2 * D_MODEL
CG = E_C // G_C
D_FF = 2816
N_EXPERTS, D_FF_E = 8, 1408
PAGE_SIZE = 128
RMS_EPS, LN_EPS, LNX_EPS = 1e-6, 1e-5, 64e-5
NEG_INF = -1e30
ATTN_SCALE = DK_A ** -0.5
DECAY_SCALE = math.exp(-0.5)
INV_SQRT2 = 2.0 ** -0.5

LANES = 128
SUBLANES = 8
MIB = 1 << 20
VMEM_LIMIT = 56 * MIB

ROW_TILE = 256
ATTN_TQ, ATTN_TK = 256, 256
PAGES_PER_STEP = 8
MOE_TILE = 512
MOE_CHUNK = 160
SCAN_TC = 16
CHAINS = 64


class _Group(NamedTuple):
    n: int
    t: int
    tm: int

    @property
    def rows(self):
        return self.n * self.t

    @property
    def tiles(self):
        return self.rows // self.tm

    @property
    def tiles_per_seq(self):
        return max(self.t // self.tm, 1)

    @property
    def mod_rows(self):
        return 1 if self.t >= self.tm else self.tm


def _cparams(*sem):
    return pltpu.CompilerParams(dimension_semantics=sem, vmem_limit_bytes=VMEM_LIMIT)


def _row_spec(tm, cols):
    return pl.BlockSpec((tm, cols), lambda i: (i, 0))


def _full_spec(shape):
    zeros = (0,) * len(shape)
    return pl.BlockSpec(shape, lambda *_: zeros)


def _mod_spec(grp):
    tps = grp.tiles_per_seq
    return pl.BlockSpec((None, grp.mod_rows, D_MODEL), lambda i: (i // tps, 0, 0))


def _expand_mod(m, grp):
    if grp.mod_rows == 1:
        return m[:, None, :]
    return jnp.repeat(m, grp.t, axis=0)[None]


def _dot(a, b):
    return jnp.dot(a, b, preferred_element_type=F32)


def _dot_nt(a, b):
    return lax.dot_general(a, b, (((1,), (1,)), ((), ())), preferred_element_type=F32)


def _split_bf16(x):
    hi = x.astype(BF16)
    lo = (x - hi.astype(F32)).astype(BF16)
    return hi, lo


def _segsum(x, seg):
    hi, lo = _split_bf16(x)
    return _dot(hi, seg) + _dot(lo, seg)


def _silu(x):
    return x * jax.nn.sigmoid(x)


def _mod_kernel(c_ref, w_ref, b_ref, o_ref):
    a_hi, a_lo = _split_bf16(_silu(c_ref[...]))
    w_hi, w_lo = _split_bf16(w_ref[...])
    o_ref[...] = _dot(a_hi, w_hi) + _dot(a_hi, w_lo) + _dot(a_lo, w_hi) + b_ref[...]


def _mod_call(c_all, w_mod, b_mod):
    rows = c_all.shape[0]
    tn = D_MODEL
    return pl.pallas_call(
        _mod_kernel,
        grid=(DEPTH, 6 * D_MODEL // tn),
        in_specs=[
            pl.BlockSpec((rows, D_MODEL), lambda i, j: (0, 0)),
            pl.BlockSpec((None, D_MODEL, tn), lambda i, j: (i, 0, j)),
            pl.BlockSpec((None, 1, tn), lambda i, j: (i, 0, j)),
        ],
        out_specs=pl.BlockSpec((None, rows, tn), lambda i, j: (i, 0, j)),
        out_shape=jax.ShapeDtypeStruct((DEPTH, rows, 6 * D_MODEL), F32),
        compiler_params=_cparams("parallel", "parallel"),
        name="adaln_mod",
    )(c_all, w_mod, b_mod[:, None, :])


def _norm_mod_kernel(x_ref, g_ref, sc_ref, sh_ref, o_ref):
    x = x_ref[...]
    y = x * lax.rsqrt(jnp.mean(x * x, axis=-1, keepdims=True) + RMS_EPS) * g_ref[...]
    o_ref[...] = (y * (1.0 + sc_ref[...]) + sh_ref[...]).astype(o_ref.dtype)


def _norm_mod_call(x, g, sc, sh, grp):
    return pl.pallas_call(
        _norm_mod_kernel,
        grid=(grp.tiles,),
        in_specs=[_row_spec(grp.tm, D_MODEL), _full_spec((1, D_MODEL)), _mod_spec(grp), _mod_spec(grp)],
        out_specs=_row_spec(grp.tm, D_MODEL),
        out_shape=jax.ShapeDtypeStruct((grp.rows, D_MODEL), BF16),
        compiler_params=_cparams("parallel"),
        name="norm_mod",
    )(x, g[None, :], sc, sh)


def _norm_mod_router_kernel(x_ref, g_ref, sc_ref, sh_ref, wr_ref, br_ref, o_ref, logit_ref):
    x = x_ref[...]
    y = x * lax.rsqrt(jnp.mean(x * x, axis=-1, keepdims=True) + RMS_EPS) * g_ref[...]
    hn = y * (1.0 + sc_ref[...]) + sh_ref[...]
    o_ref[...] = hn.astype(o_ref.dtype)
    h_hi, h_lo = _split_bf16(hn)
    w_hi, w_lo = _split_bf16(wr_ref[...])
    logit_ref[...] = _dot(h_hi, w_hi) + _dot(h_hi, w_lo) + _dot(h_lo, w_hi) + br_ref[...]


def _norm_mod_router_call(x, g, sc, sh, wr, br, grp):
    wr_pad = jnp.zeros((D_MODEL, LANES), F32).at[:, :N_EXPERTS].set(wr)
    br_pad = jnp.full((1, LANES), -jnp.inf, F32).at[0, :N_EXPERTS].set(br)
    return pl.pallas_call(
        _norm_mod_router_kernel,
        grid=(grp.tiles,),
        in_specs=[_row_spec(grp.tm, D_MODEL), _full_spec((1, D_MODEL)), _mod_spec(grp), _mod_spec(grp),
                  _full_spec((D_MODEL, LANES)), _full_spec((1, LANES))],
        out_specs=[_row_spec(grp.tm, D_MODEL), _row_spec(grp.tm, LANES)],
        out_shape=[jax.ShapeDtypeStruct((grp.rows, D_MODEL), BF16),
                   jax.ShapeDtypeStruct((grp.rows, LANES), F32)],
        compiler_params=_cparams("parallel"),
        name="norm_mod_router",
    )(x, g[None, :], sc, sh, wr_pad, br_pad)


def _final_norm_kernel(x_ref, g_ref, o_ref):
    x = x_ref[...]
    o_ref[...] = x * lax.rsqrt(jnp.mean(x * x, axis=-1, keepdims=True) + RMS_EPS) * g_ref[...]


def _final_norm_call(x, g, grp):
    return pl.pallas_call(
        _final_norm_kernel,
        grid=(grp.tiles,),
        in_specs=[_row_spec(grp.tm, D_MODEL), _full_spec((1, D_MODEL))],
        out_specs=_row_spec(grp.tm, D_MODEL),
        out_shape=jax.ShapeDtypeStruct((grp.rows, D_MODEL), F32),
        compiler_params=_cparams("parallel"),
        name="final_norm",
    )(x, g[None, :])


def _inproj_even_kernel(a_ref, w_ref, q_ref, k_ref, v_ref, kb_ref, vb_ref, pb_ref):
    a = a_ref[...]
    q_ref[...] = (_dot(a, w_ref[:, :QK_A]) * ATTN_SCALE).astype(BF16)
    k = _dot(a, w_ref[:, QK_A:2 * QK_A])
    k_ref[...] = k
    kb_ref[...] = k.astype(BF16)
    v = _dot(a, w_ref[:, 2 * QK_A:A_COLS])
    v_ref[...] = v
    vb_ref[...] = v.astype(BF16)
    half = B_COLS // 2
    pb_ref[:, :half] = _dot(a, w_ref[:, A_COLS:A_COLS + half])
    pb_ref[:, half:] = _dot(a, w_ref[:, A_COLS + half:])


def _inproj_even_call(hn, w, grp):
    tm, rows = grp.tm, grp.rows
    sds = jax.ShapeDtypeStruct
    return pl.pallas_call(
        _inproj_even_kernel,
        grid=(grp.tiles,),
        in_specs=[_row_spec(tm, D_MODEL), _full_spec((D_MODEL, IN_COLS_EVEN))],
        out_specs=[_row_spec(tm, QK_A), _row_spec(tm, QK_A), _row_spec(tm, W_A),
                   _row_spec(tm, QK_A), _row_spec(tm, W_A), _row_spec(tm, B_COLS)],
        out_shape=[sds((rows, QK_A), BF16), sds((rows, QK_A), F32), sds((rows, W_A), F32),
                   sds((rows, QK_A), BF16), sds((rows, W_A), BF16), sds((rows, B_COLS), F32)],
        compiler_params=_cparams("parallel"),
        name="inproj_even",
    )(hn, w)


def _lambda(lq_ref, lk_ref, lam_init):
    t = jnp.sum(lq_ref[...] * lk_ref[...], axis=1, keepdims=True)
    e = jnp.exp(t)
    return e[0:1] - e[1:2] + lam_init


def _attn_prompt_kernel(lq_ref, lk_ref, g_ref, qf_ref, q_ref, k_ref, kf_ref, vt_ref, o_ref,
                        m_scr, l_scr, acc_scr, *, tq, tk, lam_init):
    hp = pl.program_id(1)
    qi = pl.program_id(2)
    lane = lax.broadcasted_iota(jnp.int32, (1, LANES), 1)
    q = q_ref[...]
    q_aug = []
    for i in range(4):
        q_map = jnp.where(lane // DK_A == i, q, jnp.zeros_like(q))
        feat = jnp.broadcast_to(qf_ref[pl.ds(2 * hp + i // 2, 1), :], (tq, LANES)).astype(BF16)
        q_aug.append(jnp.concatenate([q_map, feat], axis=1))
    m_scr[...] = jnp.full(m_scr.shape, NEG_INF, F32)
    l_scr[...] = jnp.zeros(l_scr.shape, F32)
    acc_scr[...] = jnp.zeros(acc_scr.shape, F32)
    row = lax.broadcasted_iota(jnp.int32, (tk, 1), 0)
    col = lax.broadcasted_iota(jnp.int32, (1, tq), 1)
    q0 = qi * tq

    def scores(kt):
        k0 = pl.multiple_of(kt * tk, tk)
        k_aug = jnp.concatenate([k_ref[pl.ds(k0, tk), :], kf_ref[pl.ds(k0, tk), :]], axis=1)
        return tuple(_dot_nt(k_aug, q_aug[i]) for i in range(4))

    def softmax_pv(kt, s_maps, masked):
        vt = vt_ref[kt]
        if masked:
            keep = (kt * tk + row) <= (q0 + col)
        for i in range(4):
            h = i // 2
            s = jnp.where(keep, s_maps[i], NEG_INF) if masked else s_maps[i]
            m_old = m_scr[i]
            m_new = jnp.maximum(m_old, jnp.max(s, axis=0, keepdims=True))
            alpha = jnp.exp(m_old - m_new)
            p = jnp.exp(s - m_new)
            l_scr[i] = alpha * l_scr[i] + jnp.sum(p, axis=0, keepdims=True)
            acc_scr[i] = alpha * acc_scr[i] + _dot(vt[h * DV_A:(h + 1) * DV_A, :], p.astype(BF16))
            m_scr[i] = m_new

    n_full = q0 // tk

    def body(kt, s_maps):
        s_next = scores(kt + 1)
        softmax_pv(kt, s_maps, False)
        return s_next

    s_last = lax.fori_loop(0, n_full, body, scores(0))
    softmax_pv(n_full, s_last, True)

    lam = _lambda(lq_ref, lk_ref, lam_init)
    heads = []
    for h in range(2):
        oh = acc_scr[2 * h] / l_scr[2 * h] - lam * (acc_scr[2 * h + 1] / l_scr[2 * h + 1])
        ms = jnp.mean(oh * oh, axis=0, keepdims=True)
        heads.append(oh * lax.rsqrt(ms + RMS_EPS))
    o = jnp.concatenate(heads, axis=0).T
    o_ref[...] = (o * g_ref[...] * (1.0 - lam_init)).astype(o_ref.dtype)


def _attn_prompt_call(q, k, v, slopes, lq, lk, g2, n, t, lam_init):
    tq, tk = ATTN_TQ, min(ATTN_TK, t)
    tq = min(tq, tk)
    kern = functools.partial(_attn_prompt_kernel, tq=tq, tk=tk, lam_init=lam_init)
    pos = jnp.arange(t)
    k_feat = jnp.zeros((t, LANES), F32).at[:, 0].set(pos // 256).at[:, 1].set(pos % 256).astype(BF16)
    q_feat = jnp.zeros((H_A, LANES), F32).at[:, 0].set(256.0 * slopes).at[:, 1].set(slopes)
    v_t = v.reshape(n, t // tk, tk, W_A).transpose(0, 1, 3, 2)
    qspec = pl.BlockSpec((None, tq, LANES), lambda b, hp, qi: (b, qi, hp))
    return pl.pallas_call(
        kern,
        grid=(n, H_A // 2, t // tq),
        in_specs=[_full_spec((2, DK_A)), _full_spec((2, DK_A)), _full_spec((1, LANES)),
                  _full_spec((H_A, LANES)), qspec,
                  pl.BlockSpec((None, t, LANES), lambda b, hp, qi: (b, 0, hp)),
                  _full_spec((t, LANES)),
                  pl.BlockSpec((None, t // tk, LANES, tk), lambda b, hp, qi: (b, 0, hp, 0))],
        out_specs=qspec,
        out_shape=jax.ShapeDtypeStruct((n, t, W_A), BF16),
        scratch_shapes=[pltpu.VMEM((4, 1, tq), F32), pltpu.VMEM((4, 1, tq), F32),
                        pltpu.VMEM((4, DV_A, tq), F32)],
        compiler_params=_cparams("parallel", "parallel", "parallel"),
        name="diff_attn_prompt",
    )(lq, lk, g2, q_feat, q.reshape(n, t, QK_A), k.reshape(n, t, QK_A), k_feat, v_t)


def _attn_sample_kernel(pt_ref, lq_ref, lk_ref, g_ref, q_ref, kn_ref, vn_ref, *rest,
                        pages, steps, past_len, t_new, lam_init):
    k_pages, v_pages = rest[:pages], rest[pages:2 * pages]
    o_ref, m_scr, l_scr, acc_scr = rest[2 * pages:]
    step = pl.program_id(1)
    hr = 2 * t_new
    row = lax.broadcasted_iota(jnp.int32, (H_A * hr, 1), 0)
    slope = jnp.exp2(-(row // hr + 1).astype(F32))

    @pl.when(step == 0)
    def _():
        m_scr[...] = jnp.full(m_scr.shape, NEG_INF, F32)
        l_scr[...] = jnp.zeros(l_scr.shape, F32)
        acc_scr[...] = jnp.zeros(acc_scr.shape, F32)

    def update(scores, values_t, keep):
        s = jnp.concatenate(scores, axis=0)
        if keep is not None:
            s = jnp.where(keep, s, NEG_INF)
        m_old = m_scr[...]
        m_new = jnp.maximum(m_old, jnp.max(s, axis=1, keepdims=True))
        alpha = jnp.exp(m_old - m_new)
        p = jnp.exp(s - m_new)
        l_scr[...] = alpha * l_scr[...] + jnp.sum(p, axis=1, keepdims=True)
        p = p.astype(BF16)
        pv = jnp.concatenate([_dot_nt(p[h * hr:(h + 1) * hr], values_t[h]) for h in range(H_A)], axis=0)
        acc_scr[...] = alpha * acc_scr[...] + pv
        m_scr[...] = m_new

    col = lax.broadcasted_iota(jnp.int32, (1, pages * PAGE_SIZE), 1)
    k_rel = (step * (pages * PAGE_SIZE) - past_len + col).astype(F32)
    scores, values_t = [], []
    for h in range(H_A):
        k_t = jnp.concatenate([kp[h] for kp in k_pages], axis=1).astype(BF16)
        values_t.append(jnp.concatenate([vp[h] for vp in v_pages], axis=1).astype(BF16))
        scores.append(_dot(q_ref[h], k_t))
    update([s + slope[h * hr:(h + 1) * hr] * k_rel for h, s in enumerate(scores)], values_t, None)

    @pl.when(step == steps - 1)
    def _():
        new_col = lax.broadcasted_iota(jnp.int32, (1, PAGE_SIZE), 1)
        keep = new_col <= row % t_new
        bias = slope * new_col.astype(F32)
        update([_dot(q_ref[h], kn_ref[h]) + bias[h * hr:(h + 1) * hr] for h in range(H_A)],
               [vn_ref[h] for h in range(H_A)], keep)
        lam = _lambda(lq_ref, lk_ref, lam_init)
        a = acc_scr[...] / l_scr[...]
        for h in range(H_A):
            oh = a[h * hr:h * hr + t_new] - lam * a[h * hr + t_new:(h + 1) * hr]
            ms = jnp.mean(oh * oh, axis=1, keepdims=True)
            o_ref[h] = oh * lax.rsqrt(ms + RMS_EPS) * g_ref[...] * (1.0 - lam_init)


def _attn_sample_call(q, k_new, v_new, cache_k, cache_v, page_table, j, lq, lk, g1, n, t, lam_init):
    n_pages = page_table.shape[1]
    pages = PAGES_PER_STEP
    steps = n_pages // pages
    n_pool = cache_k.shape[1]
    ck = cache_k.transpose(0, 1, 3, 4, 2)
    cv = cache_v.transpose(0, 1, 3, 4, 2)
    kern = functools.partial(_attn_sample_kernel, pages=pages, steps=steps,
                             past_len=n_pages * PAGE_SIZE, t_new=t, lam_init=lam_init)
    qh = jnp.einsum('nthmd,km->nhktmd', q.reshape(n, t, H_A, 2, DK_A), jnp.eye(2, dtype=q.dtype))
    qh = qh.reshape(n, H_A, 2 * t, 2 * DK_A)

    def new_page(x):
        x = x.reshape(n, t, H_A, DV_A).transpose(0, 2, 3, 1)
        return jnp.pad(x, ((0, 0), (0, 0), (0, 0), (0, PAGE_SIZE - t)))

    def page_spec(i):
        return pl.BlockSpec((None, None, H_A, DV_A, PAGE_SIZE),
                            lambda b, s, pt: (j, pt[b, s * pages + i], 0, 0, 0))

    small = lambda shape: pl.BlockSpec(shape, lambda b, s, pt: (0,) * len(shape))
    head_spec = lambda r: pl.BlockSpec((None, H_A, r, DV_A), lambda b, s, pt: (b, 0, 0, 0))
    new_spec = pl.BlockSpec((None, H_A, DV_A, PAGE_SIZE), lambda b, s, pt: (b, 0, 0, 0))
    grid_spec = pltpu.PrefetchScalarGridSpec(
        num_scalar_prefetch=1,
        grid=(n, steps),
        in_specs=[small((2, DK_A)), small((2, DK_A)), small((1, DV_A)),
                  head_spec(2 * t), new_spec, new_spec]
                 + [page_spec(i) for i in range(pages)] * 2,
        out_specs=head_spec(t),
        scratch_shapes=[pltpu.VMEM((H_A * 2 * t, 1), F32), pltpu.VMEM((H_A * 2 * t, 1), F32),
                        pltpu.VMEM((H_A * 2 * t, DV_A), F32)],
    )
    o = pl.pallas_call(
        kern,
        grid_spec=grid_spec,
        out_shape=jax.ShapeDtypeStruct((n, H_A, t, DV_A), F32),
        compiler_params=_cparams("parallel", "arbitrary"),
        name="diff_attn_paged",
    )(page_table, lq, lk, g1, qh, new_page(k_new), new_page(v_new), *([ck] * pages), *([cv] * pages))
    return o.transpose(0, 2, 1, 3).reshape(n * t, W_A).astype(BF16)


def _rwkv_prep_kernel(pb_ref, first_ref, mu_ref, wl_ref, w0_ref, a0_ref, kk_w_ref, ka_w_ref, rk_ref,
                      seg_ref, r_o, w_o, k_o, v_o, kk_o, b_o, bonus_o, g_o, *, period):
    pb = pb_ref[...]
    tm = pb.shape[0]
    row = lax.broadcasted_iota(jnp.int32, (tm, 1), 0)
    prev = jnp.where(row % period == 0, first_ref[...], pltpu.roll(pb, 1, 0))
    xb = pb + (prev - pb) * mu_ref[...]
    r = xb[:, :W_B]
    kb = xb[:, W_B:2 * W_B]
    vb = xb[:, 2 * W_B:3 * W_B]
    z = xb[:, 3 * W_B:]
    ll = lax.broadcasted_iota(jnp.int32, (1, LORA_COLS), 1)
    zz = jnp.where(ll < LORA_W, jnp.tanh(z), jnp.where(ll < LORA_W + LORA_A, z, jax.nn.sigmoid(z)))
    lo = _dot(zz.astype(BF16), wl_ref[...])
    decay = jnp.exp(-DECAY_SCALE * jax.nn.sigmoid(w0_ref[...] + lo[:, :W_B]))
    a = jax.nn.sigmoid(a0_ref[...] + lo[:, W_B:2 * W_B])
    seg = seg_ref[...]
    kk = kb * kk_w_ref[...]
    kk = kk / jnp.maximum(jnp.sqrt(_segsum(kk * kk, seg)), 1e-12)
    k_adj = kb * (1.0 + (a - 1.0) * ka_w_ref[...])
    r_o[...] = r
    w_o[...] = decay
    k_o[...] = k_adj
    v_o[...] = vb
    kk_o[...] = kk
    b_o[...] = kk * a
    bonus_o[...] = _segsum(r * k_adj * rk_ref[...], seg) * vb
    g_o[...] = lo[:, 2 * W_B:]


def _rwkv_prep_call(pb, first, p, j, seg, grp):
    tm, rows = grp.tm, grp.rows
    period = tm if grp.mod_rows == 1 else grp.t
    first_rows = first.shape[1]
    vec = lambda x: x[j].reshape(1, -1)
    wl = jnp.zeros((LORA_COLS, 3 * W_B), F32)
    wl = wl.at[:LORA_W, :W_B].set(p['w_up'][j])
    wl = wl.at[LORA_W:LORA_W + LORA_A, W_B:2 * W_B].set(p['a_up'][j])
    wl = wl.at[LORA_W + LORA_A:, 2 * W_B:].set(p['g_up'][j])
    outs = [jax.ShapeDtypeStruct((rows, W_B), F32)] * 8
    return pl.pallas_call(
        functools.partial(_rwkv_prep_kernel, period=period),
        grid=(grp.tiles,),
        in_specs=[_row_spec(tm, B_COLS),
                  pl.BlockSpec((None, first_rows, B_COLS), lambda i: (i, 0, 0)),
                  _full_spec((1, B_COLS)), _full_spec((LORA_COLS, 3 * W_B)),
                  _full_spec((1, W_B)), _full_spec((1, W_B)), _full_spec((1, W_B)),
                  _full_spec((1, W_B)), _full_spec((1, W_B)), _full_spec((W_B, W_B))],
        out_specs=[_row_spec(tm, W_B)] * 8,
        out_shape=outs,
        compiler_params=_cparams("parallel"),
        name="rwkv_prep",
    )(pb, first, vec(p['mu_b']), wl.astype(BF16), vec(p['w0']), vec(p['a0']), vec(p['k_k']),
      vec(p['k_a']), vec(p['r_k']), seg)


def _rwkv_scan_kernel(kk_ref, w_ref, b_ref, k_ref, r_ref, v_ref, s0_ref, y_ref, s_ref, *, tc):
    @pl.when(pl.program_id(1) == 0)
    def _():
        s_ref[...] = s0_ref[...]

    n_acc = 4

    def step(t, carry):
        parts = [None] * n_acc
        for k in range(N_B):
            term = s_ref[k] * kk_ref[t, pl.ds(k, 1), :]
            parts[k % n_acc] = term if parts[k % n_acc] is None else parts[k % n_acc] + term
        sa = (parts[0] + parts[1]) + (parts[2] + parts[3])
        v_t = v_ref[t]
        parts = [None] * n_acc
        for k in range(N_B):
            s_new = (s_ref[k] * w_ref[t, pl.ds(k, 1), :] - sa * b_ref[t, pl.ds(k, 1), :]
                     + v_t * k_ref[t, pl.ds(k, 1), :])
            s_ref[k] = s_new
            term = s_new * r_ref[t, pl.ds(k, 1), :]
            parts[k % n_acc] = term if parts[k % n_acc] is None else parts[k % n_acc] + term
        y_ref[t] = (parts[0] + parts[1]) + (parts[2] + parts[3])
        return carry

    lax.fori_loop(0, tc, step, 0)


def _to_chain_layout(x, n, t):
    groups = n * H_B // CHAINS
    x = x.reshape(n, t, H_B, N_B).transpose(1, 3, 0, 2).reshape(t, N_B, groups, CHAINS)
    x = x.transpose(2, 0, 1, 3)
    return jnp.concatenate([x, x], axis=-1)


def _value_to_chain_layout(v, n, t):
    groups = n * H_B // CHAINS
    v = v.reshape(n, t, H_B, N_B).transpose(1, 3, 0, 2).reshape(t, 2, N_B // 2, groups, CHAINS)
    return v.transpose(3, 0, 2, 1, 4).reshape(groups, t, N_B // 2, 2 * CHAINS)


def _value_from_chain_layout(y, n, t):
    groups = n * H_B // CHAINS
    y = y.reshape(groups, t, N_B // 2, 2, CHAINS).transpose(1, 3, 2, 0, 4)
    y = y.reshape(t, N_B, n, H_B).transpose(2, 0, 3, 1)
    return y.reshape(n * t, W_B)


def _state_to_chain_layout(s, n):
    groups = n * H_B // CHAINS
    s = s.reshape(groups, CHAINS, 2, N_B // 2, N_B)
    return s.transpose(0, 4, 3, 2, 1).reshape(groups, N_B, N_B // 2, 2 * CHAINS)


def _state_from_chain_layout(s, n):
    groups = n * H_B // CHAINS
    s = s.reshape(groups, N_B, N_B // 2, 2, CHAINS).transpose(0, 4, 3, 2, 1)
    return s.reshape(n, H_B, N_B, N_B)


def _chain_in_kernel(x_ref, o_ref, z_scr, *, per_key, tt):
    n_seq = x_ref.shape[0]
    for s in range(n_seq):
        z_scr[s] = x_ref[s].T
    for j in range(o_ref.shape[0] // tt):
        rows = []
        for half in range(2):
            start = j if per_key else half * (N_B // 2) + j
            rows += [z_scr[s, pl.ds(start, H_B, stride=N_B), :] for s in range(n_seq)]
        tile = jnp.concatenate(rows, axis=0).T
        o_ref[pl.ds(j, tt, stride=o_ref.shape[0] // tt), :] = tile


def _chain_in_call(x, n, t, per_key):
    tt = LANES
    rows = N_B if per_key else N_B // 2
    out = pl.pallas_call(
        functools.partial(_chain_in_kernel, per_key=per_key, tt=tt),
        grid=(t // tt,),
        in_specs=[pl.BlockSpec((n, tt, W_B), lambda i: (0, i, 0))],
        out_specs=pl.BlockSpec((tt * rows, 2 * CHAINS), lambda i: (i, 0)),
        out_shape=jax.ShapeDtypeStruct((t * rows, 2 * CHAINS), F32),
        scratch_shapes=[pltpu.VMEM((n, W_B, tt), F32)],
        compiler_params=_cparams("parallel"),
        name="chain_layout_in",
    )(x.reshape(n, t, W_B))
    return out.reshape(1, t, rows, 2 * CHAINS)


def _chain_out_kernel(y_ref, o_ref, z_scr, *, tt):
    n_seq = o_ref.shape[0]
    rows = N_B // 2
    for j in range(rows):
        tile = y_ref[pl.ds(j, tt, stride=rows), :].T
        for half in range(2):
            for s in range(n_seq):
                r0 = (half * n_seq + s) * H_B
                z_scr[s, pl.ds(half * rows + j, H_B, stride=N_B), :] = tile[r0:r0 + H_B]
    for s in range(n_seq):
        o_ref[s] = z_scr[s].T


def _chain_out_call(y, n, t):
    tt = LANES
    rows = N_B // 2
    out = pl.pallas_call(
        functools.partial(_chain_out_kernel, tt=tt),
        grid=(t // tt,),
        in_specs=[pl.BlockSpec((tt * rows, 2 * CHAINS), lambda i: (i, 0))],
        out_specs=pl.BlockSpec((n, tt, W_B), lambda i: (0, i, 0)),
        out_shape=jax.ShapeDtypeStruct((n, t, W_B), F32),
        scratch_shapes=[pltpu.VMEM((n, W_B, tt), F32)],
        compiler_params=_cparams("parallel"),
        name="chain_layout_out",
    )(y.reshape(t * rows, 2 * CHAINS))
    return out.reshape(n * t, W_B)


def _rwkv_scan_call(kk, w, b, k, r, v, s0, n, t):
    groups = n * H_B // CHAINS
    tc = min(SCAN_TC, t)
    vec_spec = pl.BlockSpec((None, tc, N_B, 2 * CHAINS), lambda g, i: (g, i, 0, 0))
    val_spec = pl.BlockSpec((None, tc, N_B // 2, 2 * CHAINS), lambda g, i: (g, i, 0, 0))
    st_spec = pl.BlockSpec((None, N_B, N_B // 2, 2 * CHAINS), lambda g, i: (g, 0, 0, 0))
    in_kernel_layout = groups == 1 and t % LANES == 0
    if in_kernel_layout:
        vecs = [_chain_in_call(x, n, t, True) for x in (kk, w, b, k, r)]
        val = _chain_in_call(v, n, t, False)
    else:
        vecs = [_to_chain_layout(x, n, t) for x in (kk, w, b, k, r)]
        val = _value_to_chain_layout(v, n, t)
    y, s_fin = pl.pallas_call(
        functools.partial(_rwkv_scan_kernel, tc=tc),
        grid=(groups, t // tc),
        in_specs=[vec_spec] * 5 + [val_spec, st_spec],
        out_specs=[val_spec, st_spec],
        out_shape=[jax.ShapeDtypeStruct((groups, t, N_B // 2, 2 * CHAINS), F32),
                   jax.ShapeDtypeStruct((groups, N_B, N_B // 2, 2 * CHAINS), F32)],
        compiler_params=_cparams("parallel", "arbitrary"),
        name="rwkv_scan",
    )(*vecs, val, _state_to_chain_layout(s0, n))
    y = _chain_out_call(y, n, t) if in_kernel_layout else _value_from_chain_layout(y, n, t)
    return y, _state_from_chain_layout(s_fin, n)


def _rwkv_post_kernel(y_ref, bonus_ref, g_ref, lg_ref, lb_ref, seg_ref, o_ref):
    y = y_ref[...]
    seg = seg_ref[...]
    yc = y - _segsum(y, seg) * (1.0 / N_B)
    var = _segsum(yc * yc, seg) * (1.0 / N_B)
    yn = yc * lax.rsqrt(var + LNX_EPS) * lg_ref[...] + lb_ref[...]
    o_ref[...] = ((yn + bonus_ref[...]) * g_ref[...]).astype(o_ref.dtype)


def _rwkv_post_call(y, bonus, g, lg, lb, seg, grp):
    tm = grp.tm
    return pl.pallas_call(
        _rwkv_post_kernel,
        grid=(grp.tiles,),
        in_specs=[_row_spec(tm, W_B)] * 3 + [_full_spec((1, W_B)), _full_spec((1, W_B)),
                                              _full_spec((W_B, W_B))],
        out_specs=_row_spec(tm, W_B),
        out_shape=jax.ShapeDtypeStruct((grp.rows, W_B), BF16),
        compiler_params=_cparams("parallel"),
        name="rwkv_post",
    )(y, bonus, g, lg[None, :], lb[None, :], seg)


def _outproj_even_kernel(oa_ref, ob_ref, w_ref, x_ref, g_ref, o_ref):
    mix = _dot(oa_ref[...], w_ref[:W_A, :]) + _dot(ob_ref[...], w_ref[W_A:, :])
    o_ref[...] = x_ref[...] + g_ref[...] * mix


def _outproj_even_call(oa, ob, w, x, gate, grp):
    tm = grp.tm
    return pl.pallas_call(
        _outproj_even_kernel,
        grid=(grp.tiles,),
        in_specs=[_row_spec(tm, W_A), _row_spec(tm, W_B), _full_spec((W_A + W_B, D_MODEL)),
                  _row_spec(tm, D_MODEL), _mod_spec(grp)],
        out_specs=_row_spec(tm, D_MODEL),
        out_shape=jax.ShapeDtypeStruct((grp.rows, D_MODEL), F32),
        compiler_params=_cparams("parallel"),
        name="outproj_even",
    )(oa, ob, w, x, gate)


def _swiglu_up_kernel(a_ref, wg_ref, wu_ref, o_ref):
    a = a_ref[...]
    o_ref[...] = (_silu(_dot(a, wg_ref[...])) * _dot(a, wu_ref[...])).astype(o_ref.dtype)


def _swiglu_up_call(hn, wg, wu, grp):
    tm = grp.tm
    tn = D_FF // 2
    return pl.pallas_call(
        _swiglu_up_kernel,
        grid=(D_FF // tn, grp.tiles),
        in_specs=[pl.BlockSpec((tm, D_MODEL), lambda c, i: (i, 0)),
                  pl.BlockSpec((D_MODEL, tn), lambda c, i: (0, c)),
                  pl.BlockSpec((D_MODEL, tn), lambda c, i: (0, c))],
        out_specs=pl.BlockSpec((tm, tn), lambda c, i: (i, c)),
        out_shape=jax.ShapeDtypeStruct((grp.rows, D_FF), BF16),
        compiler_params=_cparams("parallel", "parallel"),
        name="swiglu_up",
    )(hn, wg, wu)


def _down_res_kernel(h_ref, w_ref, x_ref, g_ref, o_ref):
    o_ref[...] = x_ref[...] + g_ref[...] * _dot(h_ref[...], w_ref[...])


def _down_res_call(h, w, x, gate, grp):
    tm = grp.tm
    kdim = h.shape[1]
    return pl.pallas_call(
        _down_res_kernel,
        grid=(grp.tiles,),
        in_specs=[_row_spec(tm, kdim), _full_spec((kdim, D_MODEL)), _row_spec(tm, D_MODEL), _mod_spec(grp)],
        out_specs=_row_spec(tm, D_MODEL),
        out_shape=jax.ShapeDtypeStruct((grp.rows, D_MODEL), F32),
        compiler_params=_cparams("parallel"),
        name="down_res",
    )(h, w, x, gate)


def _gelu(x):
    return 0.5 * x * (1.0 + lax.erf(x * INV_SQRT2))


def _gmlp_u_kernel(a_ref, w_ref, o_ref):
    o_ref[...] = _gelu(_dot(a_ref[...], w_ref[...]))


def _gmlp_v_kernel(a_ref, w_ref, lg_ref, lb_ref, vb_ref, vf_ref):
    v = _gelu(_dot(a_ref[...], w_ref[...]))
    vc = v - jnp.mean(v, axis=-1, keepdims=True)
    vn = vc * lax.rsqrt(jnp.mean(vc * vc, axis=-1, keepdims=True) + LN_EPS) * lg_ref[...] + lb_ref[...]
    vb_ref[...] = vn.astype(BF16)
    vf_ref[...] = vn


def _gmlp_in_call(hn, w, lg, lb, grp):
    tm, rows = grp.tm, grp.rows
    wspec = lambda half: pl.BlockSpec((D_MODEL, E_C), lambda i: (0, half))
    u = pl.pallas_call(
        _gmlp_u_kernel,
        grid=(grp.tiles,),
        in_specs=[_row_spec(tm, D_MODEL), wspec(0)],
        out_specs=_row_spec(tm, E_C),
        out_shape=jax.ShapeDtypeStruct((rows, E_C), F32),
        compiler_params=_cparams("parallel"),
        name="gmlp_u",
    )(hn, w)
    vb, vf = pl.pallas_call(
        _gmlp_v_kernel,
        grid=(grp.tiles,),
        in_specs=[_row_spec(tm, D_MODEL), wspec(1), _full_spec((1, E_C)), _full_spec((1, E_C))],
        out_specs=[_row_spec(tm, E_C), _row_spec(tm, E_C)],
        out_shape=[jax.ShapeDtypeStruct((rows, E_C), BF16), jax.ShapeDtypeStruct((rows, E_C), F32)],
        compiler_params=_cparams("parallel"),
        name="gmlp_v",
    )(hn, w, lg[None, :], lb[None, :])
    return u, vb, vf


def _gmlp_out_kernel(u_ref, v_ref, ws_ref, bs_ref, wo_ref, x_ref, g_ref, o_ref, z_scr, *, lm):
    tm = u_ref.shape[0]
    for c in range(tm // lm):
        r0 = c * lm
        for gi in range(G_C):
            c0 = gi * CG
            mixed = _dot(ws_ref[gi], v_ref[r0:r0 + lm, c0:c0 + CG]) + bs_ref[:, gi:gi + 1]
            z_scr[r0:r0 + lm, c0:c0 + CG] = (u_ref[r0:r0 + lm, c0:c0 + CG] * mixed).astype(BF16)
    o_ref[...] = x_ref[...] + g_ref[...] * _dot(z_scr[...], wo_ref[...])


def _gmlp_out_call(u, vb, ws, bs, wo, x, gate, grp):
    tm = grp.tm
    lm = ws.shape[1]
    return pl.pallas_call(
        functools.partial(_gmlp_out_kernel, lm=lm),
        grid=(grp.tiles,),
        in_specs=[_row_spec(tm, E_C), _row_spec(tm, E_C), _full_spec((G_C, lm, lm)), _full_spec((lm, G_C)),
                  _full_spec((E_C, D_MODEL)), _row_spec(tm, D_MODEL), _mod_spec(grp)],
        out_specs=_row_spec(tm, D_MODEL),
        out_shape=jax.ShapeDtypeStruct((grp.rows, D_MODEL), F32),
        scratch_shapes=[pltpu.VMEM((tm, E_C), BF16)],
        compiler_params=_cparams("parallel"),
        name="gmlp_out",
    )(u, vb, ws, bs, wo, x, gate)


def _spatial_weights(w_s, b_s, grp):
    length = min(grp.t, CHUNK)
    ws = jnp.tril(w_s[:, :length, :length])
    bs = b_s[:, :length]
    if length < CHUNK:
        reps = grp.tm // length
        ws = jnp.einsum('ab,gij->gaibj', jnp.eye(reps, dtype=ws.dtype), ws).reshape(
            G_C, reps * length, reps * length)
        bs = jnp.tile(bs, (1, reps))
    return ws.astype(BF16), bs.T


def _moe_kernel(a_ref, logit_ref, tri_ref, wg_ref, wu_ref, wd_ref, x_ref, g_ref, o_ref,
                gate_scr, rank_scr, gate_t_scr, rank_t_scr, acc_scr, *, tc):
    e = pl.program_id(1)
    lane = lax.broadcasted_iota(jnp.int32, (1, LANES), 1)

    @pl.when(e == 0)
    def _():
        logits = logit_ref[...]
        m1 = jnp.max(logits, axis=1, keepdims=True)
        i1 = jnp.min(jnp.where(logits == m1, lane, LANES), axis=1, keepdims=True)
        rest = jnp.where(lane == i1, -jnp.inf, logits)
        m2 = jnp.max(rest, axis=1, keepdims=True)
        i2 = jnp.min(jnp.where(rest == m2, lane, LANES), axis=1, keepdims=True)
        e2 = jnp.exp(m2 - m1)
        gate = jnp.where(lane == i1, 1.0 / (1.0 + e2), jnp.where(lane == i2, e2 / (1.0 + e2), 0.0))
        picked = gate != 0.0
        rank = _dot(tri_ref[...], jnp.where(picked, 1.0, 0.0).astype(BF16))
        rank = jnp.where(picked, rank, -1.0)
        gate_scr[...] = gate
        rank_scr[...] = rank
        gate_t_scr[...] = gate.T
        rank_t_scr[...] = rank.T
        acc_scr[...] = jnp.zeros(acc_scr.shape, F32)

    gate_col = jnp.sum(jnp.where(lane == e, gate_scr[...], 0.0), axis=1, keepdims=True)
    rank_col = jnp.sum(jnp.where(lane == e, rank_scr[...], 0.0), axis=1, keepdims=True)
    gate_row = gate_t_scr[pl.ds(e, 1), :]
    rank_row = rank_t_scr[pl.ds(e, 1), :]
    count = jnp.sum(jnp.where(rank_row >= 0.0, 1, 0))
    slot_col = lax.broadcasted_iota(jnp.int32, (tc, 1), 0).astype(F32)
    slot_row = lax.broadcasted_iota(jnp.int32, (1, tc), 1).astype(F32)

    def chunk(c, carry):
        base = (c * tc).astype(F32)
        take = rank_row == base + slot_col
        xg = _dot(jnp.where(take, 1.0, 0.0).astype(BF16), a_ref[...]).astype(BF16)
        h = (_silu(_dot(xg, wg_ref[...])) * _dot(xg, wu_ref[...])).astype(BF16)
        y = _dot(h, wd_ref[...])
        gate_c = jnp.sum(jnp.where(take, gate_row, 0.0), axis=1, keepdims=True)
        put = jnp.where(rank_col == base + slot_row, 1.0, 0.0).astype(BF16)
        acc_scr[...] += _dot(put, (gate_c * y).astype(BF16))
        return carry

    lax.fori_loop(0, (count + tc - 1) // tc, chunk, 0)

    @pl.when(e == N_EXPERTS - 1)
    def _():
        o_ref[...] = x_ref[...] + g_ref[...] * acc_scr[...]


def _moe_call(hn, logits, wg, wu, wd, j, x, gate, grp):
    tm = grp.tm
    tps = grp.tiles_per_seq
    tc = min(MOE_CHUNK, tm)
    tri = jnp.tril(jnp.ones((tm, tm), F32), -1).astype(BF16)
    row = lambda cols: pl.BlockSpec((tm, cols), lambda i, e: (i, 0))
    return pl.pallas_call(
        functools.partial(_moe_kernel, tc=tc),
        grid=(grp.tiles, N_EXPERTS),
        in_specs=[row(D_MODEL), row(LANES), pl.BlockSpec((tm, tm), lambda i, e: (0, 0)),
                  pl.BlockSpec((None, None, D_MODEL, D_FF_E), lambda i, e: (j, e, 0, 0)),
                  pl.BlockSpec((None, None, D_MODEL, D_FF_E), lambda i, e: (j, e, 0, 0)),
                  pl.BlockSpec((None, None, D_FF_E, D_MODEL), lambda i, e: (j, e, 0, 0)),
                  row(D_MODEL),
                  pl.BlockSpec((None, grp.mod_rows, D_MODEL), lambda i, e: (i // tps, 0, 0))],
        out_specs=row(D_MODEL),
        out_shape=jax.ShapeDtypeStruct((grp.rows, D_MODEL), F32),
        scratch_shapes=[pltpu.VMEM((tm, LANES), F32), pltpu.VMEM((tm, LANES), F32),
                        pltpu.VMEM((LANES, tm), F32), pltpu.VMEM((LANES, tm), F32),
                        pltpu.VMEM((tm, D_MODEL), F32)],
        compiler_params=_cparams("parallel", "arbitrary"),
        name="moe_experts",
    )(hn, logits, tri, wg, wu, wd, x, gate)


def _forward(x, mods, p, wb, grp, cache):
    n, t = grp.n, grp.t
    seg = jnp.kron(jnp.eye(H_B, dtype=F32), jnp.ones((N_B, N_B), F32)).astype(BF16)
    slopes = jnp.exp2(-8.0 * jnp.arange(1, H_A + 1, dtype=F32) / H_A)
    new_k, new_v, new_wkv, new_shift, new_cv = [], [], [], [], []
    for i in range(DEPTH):
        j = i // 2
        sh1, sc1, g1, sh2, sc2, g2 = [_expand_mod(m, grp) for m in mods[i]]
        hn = _norm_mod_call(x, p['norm1_g'][i], sc1, sh1, grp)
        if i % 2 == 0:
            q, k, v, kb, vb, pb = _inproj_even_call(hn, wb['w_in_e'][j], grp)
            lam_init = 0.8 - 0.6 * math.exp(-0.3 * i)
            lq = jnp.stack([p['lam_q1'][j], p['lam_q2'][j]])
            lk = jnp.stack([p['lam_k1'][j], p['lam_k2'][j]])
            if cache is None:
                g2h = jnp.tile(p['subln_g'][j], 2)[None, :]
                oa = _attn_prompt_call(q, kb, vb, slopes, lq, lk, g2h, n, t, lam_init)
                wkv0 = jnp.zeros((n, H_B, N_B, N_B), F32)
                last = pb.reshape(grp.tiles, grp.tm, B_COLS)[:, -1, :]
                first = jnp.concatenate([jnp.zeros((1, B_COLS), F32), last[:-1]], axis=0)
                starts = (jnp.arange(grp.tiles) % grp.tiles_per_seq == 0)[:, None]
                first = jnp.where(starts, 0.0, first)[:, None, :]
            else:
                cache_k, cache_v, page_table, state_wkv, state_shift = cache
                oa = _attn_sample_call(q, kb, vb, cache_k, cache_v, page_table, j, lq, lk,
                                       p['subln_g'][j][None, :], n, t, lam_init)
                wkv0 = state_wkv[j]
                first = jnp.repeat(state_shift[j], t, axis=0)[None]
            r, w, kv, vv, kk, b, bonus, g = _rwkv_prep_call(pb, first, p, j, seg, grp)
            y, s_fin = _rwkv_scan_call(kk, w, b, kv, r, vv, wkv0, n, t)
            ob = _rwkv_post_call(y, bonus, g, p['lnx_g'][j], p['lnx_b'][j], seg, grp)
            new_k.append(k.reshape(n, t, H_A, 2 * DK_A))
            new_v.append(v.reshape(n, t, H_A, DV_A))
            new_wkv.append(s_fin)
            new_shift.append(pb.reshape(n, t, B_COLS)[:, -1])
            x = _outproj_even_call(oa.reshape(grp.rows, W_A), ob, wb['w_out_e'][j], x, g1, grp)
            hn = _norm_mod_call(x, p['norm2_g'][i], sc2, sh2, grp)
            h = _swiglu_up_call(hn, wb['ffn_gate'][j], wb['ffn_up'][j], grp)
            x = _down_res_call(h, wb['ffn_down'][j], x, g2, grp)
        else:
            u, vb, vf = _gmlp_in_call(hn, wb['w_in_o'][j], p['lnv_g'][j], p['lnv_b'][j], grp)
            new_cv.append(vf.reshape(n, t, E_C))
            ws, bs = _spatial_weights(p['w_s'][j], p['b_s'][j], grp)
            x = _gmlp_out_call(u, vb, ws, bs, wb['w_out_o'][j], x, g1, grp)
            hn, logits = _norm_mod_router_call(x, p['norm2_g'][i], sc2, sh2, p['router_w'][j],
                                               p['router_b'][j], grp)
            moe_grp = grp if grp.mod_rows > 1 else _Group(n, t, min(MOE_TILE, t))
            x = _moe_call(hn, logits, wb['exp_gate'], wb['exp_up'], wb['exp_down'], j, x, g2, moe_grp)
    y = _final_norm_call(x, p['final_g'], grp)
    return (y.reshape(n, t, D_MODEL), jnp.stack(new_k), jnp.stack(new_v), jnp.stack(new_wkv),
            jnp.stack(new_shift), jnp.stack(new_cv))


_BF16_WEIGHTS = ('w_in_e', 'w_out_e', 'ffn_gate', 'ffn_up', 'ffn_down', 'w_in_o', 'w_out_o',
                 'exp_gate', 'exp_up', 'exp_down')


def _run(x_prompt, x_sample, cache_k, cache_v, state_wkv, state_shift, page_table, c_prompt, c_sample, p):
    n_p, t_p, _ = x_prompt.shape
    n_s, t_s, _ = x_sample.shape
    grp_p = _Group(n_p, t_p, min(ROW_TILE, t_p))
    grp_s = _Group(n_s, t_s, n_s * t_s)
    wb = {name: p[name].astype(BF16) for name in _BF16_WEIGHTS}
    mod = _mod_call(jnp.concatenate([c_prompt, c_sample], axis=0), p['w_mod'], p['b_mod'])
    mods_p = [jnp.split(mod[i, :n_p], 6, axis=-1) for i in range(DEPTH)]
    mods_s = [jnp.split(mod[i, n_p:], 6, axis=-1) for i in range(DEPTH)]
    y_p, k_p, v_p, wkv_p, sh_p, _ = _forward(x_prompt.reshape(n_p * t_p, D_MODEL), mods_p, p, wb, grp_p, None)
    y_s, k_s, v_s, wkv_s, sh_s, cv_s = _forward(
        x_sample.reshape(n_s * t_s, D_MODEL), mods_s, p, wb, grp_s,
        (cache_k, cache_v, page_table, state_wkv, state_shift))
    return (y_p, y_s, k_p, v_p, wkv_p, sh_p, k_s, v_s, wkv_s, sh_s, cv_s)


def kernel(x_prompt, x_sample, cache_k, cache_v, state_wkv, state_shift, page_table, c_prompt, c_sample,
           w_mod, b_mod, norm1_g, norm2_g, final_g, w_in_e, w_out_e, lam_q1, lam_k1, lam_q2, lam_k2,
           subln_g, mu_b, w0, w_up, a0, a_up, g_up, k_k, k_a, r_k, lnx_g, lnx_b, w_in_o, lnv_g, lnv_b,
           w_s, b_s, w_out_o, ffn_gate, ffn_up, ffn_down, router_w, router_b, exp_gate, exp_up, exp_down):
    p = dict(w_mod=w_mod, b_mod=b_mod, norm1_g=norm1_g, norm2_g=norm2_g, final_g=final_g,
             w_in_e=w_in_e, w_out_e=w_out_e, lam_q1=lam_q1, lam_k1=lam_k1, lam_q2=lam_q2,
             lam_k2=lam_k2, subln_g=subln_g, mu_b=mu_b, w0=w0, w_up=w_up, a0=a0, a_up=a_up,
             g_up=g_up, k_k=k_k, k_a=k_a, r_k=r_k, lnx_g=lnx_g, lnx_b=lnx_b, w_in_o=w_in_o,
             lnv_g=lnv_g, lnv_b=lnv_b, w_s=w_s, b_s=b_s, w_out_o=w_out_o, ffn_gate=ffn_gate,
             ffn_up=ffn_up, ffn_down=ffn_down, router_w=router_w, router_b=router_b,
             exp_gate=exp_gate, exp_up=exp_up, exp_down=exp_down)
    return _run(x_prompt, x_sample, cache_k, cache_v, state_wkv, state_shift, page_table,
                c_prompt, c_sample, p)
```

```python
import functools
import math
from typing import NamedTuple

import jax
import jax.numpy as jnp
from jax import lax
from jax.experimental import pallas as pl
from jax.experimental.pallas import tpu as pltpu

F32 = jnp.float32
BF16 = jnp.bfloat16

D_MODEL = 1024
DEPTH = 4
H_A, DK_A, DV_A = 8, 32, 64
QK_A = H_A * 2 * DK_A
W_A = H_A * DV_A
A_COLS = 2 * QK_A + W_A
H_B, N_B = 8, 64
W_B = H_B * N_B
LORA_W, LORA_A, LORA_G = 64, 64, 128
LORA_COLS = LORA_W + LORA_A + LORA_G
B_COLS = 3 * W_B + LORA_COLS
IN_COLS_EVEN = A_COLS + B_COLS
CHUNK, G_C = 128, 8
E_C = 2 * D_MODEL
CG = E_C // G_C
D_FF = 2816
N_EXPERTS, D_FF_E = 8, 1408
PAGE_SIZE = 128
RMS_EPS, LN_EPS, LNX_EPS = 1e-6, 1e-5, 64e-5
NEG_INF = -1e30
ATTN_SCALE = DK_A ** -0.5
DECAY_SCALE = math.exp(-0.5)
INV_SQRT2 = 2.0 ** -0.5

LANES = 128
SUBLANES = 8
MIB = 1 << 20
VMEM_LIMIT = 56 * MIB

ROW_TILE = 256
ATTN_TQ, ATTN_TK = 256, 256
PAGES_PER_STEP = 8
MOE_TILE = 512
MOE_CHUNK = 160
SCAN_TC = 32
CHAINS = 64


class _Group(NamedTuple):
    n: int
    t: int
    tm: int

    @property
    def rows(self):
        return self.n * self.t

    @property
    def tiles(self):
        return self.rows // self.tm

    @property
    def tiles_per_seq(self):
        return max(self.t // self.tm, 1)

    @property
    def mod_rows(self):
        return 1 if self.t >= self.tm else self.tm


def _cparams(*sem):
    return pltpu.CompilerParams(dimension_semantics=sem, vmem_limit_bytes=VMEM_LIMIT)


def _row_spec(tm, cols):
    return pl.BlockSpec((tm, cols), lambda i: (i, 0))


def _full_spec(shape):
    zeros = (0,) * len(shape)
    return pl.BlockSpec(shape, lambda *_: zeros)


def _mod_spec(grp):
    tps = grp.tiles_per_seq
    return pl.BlockSpec((None, grp.mod_rows, D_MODEL), lambda i: (i // tps, 0, 0))


def _expand_mod(m, grp):
    if grp.mod_rows == 1:
        return m[:, None, :]
    return jnp.repeat(m, grp.t, axis=0)[None]


def _dot(a, b):
    return jnp.dot(a, b, preferred_element_type=F32)


def _dot_nt(a, b):
    return lax.dot_general(a, b, (((1,), (1,)), ((), ())), preferred_element_type=F32)


def _split_bf16(x):
    hi = x.astype(BF16)
    lo = (x - hi.astype(F32)).astype(BF16)
    return hi, lo


def _segsum(x, seg):
    hi, lo = _split_bf16(x)
    return _dot(hi, seg) + _dot(lo, seg)


def _silu(x):
    return x * jax.nn.sigmoid(x)


def _mod_kernel(c_ref, w_ref, b_ref, o_ref):
    a_hi, a_lo = _split_bf16(_silu(c_ref[...]))
    w_hi, w_lo = _split_bf16(w_ref[...])
    o_ref[...] = _dot(a_hi, w_hi) + _dot(a_hi, w_lo) + _dot(a_lo, w_hi) + b_ref[...]


def _mod_call(c_all, w_mod, b_mod):
    rows = c_all.shape[0]
    tn = D_MODEL
    return pl.pallas_call(
        _mod_kernel,
        grid=(DEPTH, 6 * D_MODEL // tn),
        in_specs=[
            pl.BlockSpec((rows, D_MODEL), lambda i, j: (0, 0)),
            pl.BlockSpec((None, D_MODEL, tn), lambda i, j: (i, 0, j)),
            pl.BlockSpec((None, 1, tn), lambda i, j: (i, 0, j)),
        ],
        out_specs=pl.BlockSpec((None, rows, tn), lambda i, j: (i, 0, j)),
        out_shape=jax.ShapeDtypeStruct((DEPTH, rows, 6 * D_MODEL), F32),
        compiler_params=_cparams("parallel", "parallel"),
        name="adaln_mod",
    )(c_all, w_mod, b_mod[:, None, :])


def _norm_mod_kernel(x_ref, g_ref, sc_ref, sh_ref, o_ref):
    x = x_ref[...]
    y = x * lax.rsqrt(jnp.mean(x * x, axis=-1, keepdims=True) + RMS_EPS) * g_ref[...]
    o_ref[...] = (y * (1.0 + sc_ref[...]) + sh_ref[...]).astype(o_ref.dtype)


def _norm_mod_call(x, g, sc, sh, grp):
    return pl.pallas_call(
        _norm_mod_kernel,
        grid=(grp.tiles,),
        in_specs=[_row_spec(grp.tm, D_MODEL), _full_spec((1, D_MODEL)), _mod_spec(grp), _mod_spec(grp)],
        out_specs=_row_spec(grp.tm, D_MODEL),
        out_shape=jax.ShapeDtypeStruct((grp.rows, D_MODEL), BF16),
        compiler_params=_cparams("parallel"),
        name="norm_mod",
    )(x, g[None, :], sc, sh)


def _norm_mod_router_kernel(x_ref, g_ref, sc_ref, sh_ref, wr_ref, br_ref, o_ref, logit_ref):
    x = x_ref[...]
    y = x * lax.rsqrt(jnp.mean(x * x, axis=-1, keepdims=True) + RMS_EPS) * g_ref[...]
    hn = y * (1.0 + sc_ref[...]) + sh_ref[...]
    o_ref[...] = hn.astype(o_ref.dtype)
    h_hi, h_lo = _split_bf16(hn)
    w_hi, w_lo = _split_bf16(wr_ref[...])
    logit_ref[...] = _dot(h_hi, w_hi) + _dot(h_hi, w_lo) + _dot(h_lo, w_hi) + br_ref[...]


def _norm_mod_router_call(x, g, sc, sh, wr, br, grp):
    wr_pad = jnp.zeros((D_MODEL, LANES), F32).at[:, :N_EXPERTS].set(wr)
    br_pad = jnp.full((1, LANES), -jnp.inf, F32).at[0, :N_EXPERTS].set(br)
    return pl.pallas_call(
        _norm_mod_router_kernel,
        grid=(grp.tiles,),
        in_specs=[_row_spec(grp.tm, D_MODEL), _full_spec((1, D_MODEL)), _mod_spec(grp), _mod_spec(grp),
                  _full_spec((D_MODEL, LANES)), _full_spec((1, LANES))],
        out_specs=[_row_spec(grp.tm, D_MODEL), _row_spec(grp.tm, LANES)],
        out_shape=[jax.ShapeDtypeStruct((grp.rows, D_MODEL), BF16),
                   jax.ShapeDtypeStruct((grp.rows, LANES), F32)],
        compiler_params=_cparams("parallel"),
        name="norm_mod_router",
    )(x, g[None, :], sc, sh, wr_pad, br_pad)


def _inproj_even_kernel(a_ref, w_ref, q_ref, k_ref, v_ref, kb_ref, vb_ref, pb_ref):
    a = a_ref[...]
    q_ref[...] = (_dot(a, w_ref[:, :QK_A]) * ATTN_SCALE).astype(BF16)
    k = _dot(a, w_ref[:, QK_A:2 * QK_A])
    k_ref[...] = k
    kb_ref[...] = k.astype(BF16)
    v = _dot(a, w_ref[:, 2 * QK_A:A_COLS])
    v_ref[...] = v
    vb_ref[...] = v.astype(BF16)
    half = B_COLS // 2
    pb_ref[:, :half] = _dot(a, w_ref[:, A_COLS:A_COLS + half])
    pb_ref[:, half:] = _dot(a, w_ref[:, A_COLS + half:])


def _inproj_even_call(hn, w, grp):
    tm, rows = grp.tm, grp.rows
    sds = jax.ShapeDtypeStruct
    return pl.pallas_call(
        _inproj_even_kernel,
        grid=(grp.tiles,),
        in_specs=[_row_spec(tm, D_MODEL), _full_spec((D_MODEL, IN_COLS_EVEN))],
        out_specs=[_row_spec(tm, QK_A), _row_spec(tm, QK_A), _row_spec(tm, W_A),
                   _row_spec(tm, QK_A), _row_spec(tm, W_A), _row_spec(tm, B_COLS)],
        out_shape=[sds((rows, QK_A), BF16), sds((rows, QK_A), F32), sds((rows, W_A), F32),
                   sds((rows, QK_A), BF16), sds((rows, W_A), BF16), sds((rows, B_COLS), F32)],
        compiler_params=_cparams("parallel"),
        name="inproj_even",
    )(hn, w)


def _lambda(lq_ref, lk_ref, lam_init):
    t = jnp.sum(lq_ref[...] * lk_ref[...], axis=1, keepdims=True)
    e = jnp.exp(t)
    return e[0:1] - e[1:2] + lam_init


def _attn_prompt_kernel(lq_ref, lk_ref, g_ref, qf_ref, q_ref, k_ref, kf_ref, vt_ref, o_ref,
                        m_scr, l_scr, acc_scr, *, tq, tk, lam_init):
    hp = pl.program_id(1)
    qi = pl.program_id(2)
    lane = lax.broadcasted_iota(jnp.int32, (1, LANES), 1)
    q = q_ref[...]
    q_aug = []
    for i in range(4):
        q_map = jnp.where(lane // DK_A == i, q, jnp.zeros_like(q))
        feat = jnp.broadcast_to(qf_ref[pl.ds(2 * hp + i // 2, 1), :], (tq, LANES)).astype(BF16)
        q_aug.append(jnp.concatenate([q_map, feat], axis=1))
    m_scr[...] = jnp.full(m_scr.shape, NEG_INF, F32)
    l_scr[...] = jnp.zeros(l_scr.shape, F32)
    acc_scr[...] = jnp.zeros(acc_scr.shape, F32)
    row = lax.broadcasted_iota(jnp.int32, (tk, 1), 0)
    col = lax.broadcasted_iota(jnp.int32, (1, tq), 1)
    q0 = qi * tq

    def scores(kt):
        k0 = pl.multiple_of(kt * tk, tk)
        k_aug = jnp.concatenate([k_ref[pl.ds(k0, tk), :], kf_ref[pl.ds(k0, tk), :]], axis=1)
        return tuple(_dot_nt(k_aug, q_aug[i]) for i in range(4))

    def softmax_pv(kt, s_maps, masked):
        vt = vt_ref[kt]
        if masked:
            keep = (kt * tk + row) <= (q0 + col)
        for i in range(4):
            h = i // 2
            s = jnp.where(keep, s_maps[i], NEG_INF) if masked else s_maps[i]
            m_old = m_scr[i]
            m_new = jnp.maximum(m_old, jnp.max(s, axis=0, keepdims=True))
            alpha = jnp.exp(m_old - m_new)
            p = jnp.exp(s - m_new)
            l_scr[i] = alpha * l_scr[i] + jnp.sum(p, axis=0, keepdims=True)
            acc_scr[i] = alpha * acc_scr[i] + _dot(vt[h * DV_A:(h + 1) * DV_A, :], p.astype(BF16))
            m_scr[i] = m_new

    n_full = q0 // tk

    def body(kt, s_maps):
        s_next = scores(kt + 1)
        softmax_pv(kt, s_maps, False)
        return s_next

    s_last = lax.fori_loop(0, n_full, body, scores(0))
    softmax_pv(n_full, s_last, True)

    lam = _lambda(lq_ref, lk_ref, lam_init)
    heads = []
    for h in range(2):
        oh = acc_scr[2 * h] / l_scr[2 * h] - lam * (acc_scr[2 * h + 1] / l_scr[2 * h + 1])
        ms = jnp.mean(oh * oh, axis=0, keepdims=True)
        heads.append(oh * lax.rsqrt(ms + RMS_EPS))
    o = jnp.concatenate(heads, axis=0).T
    o_ref[...] = (o * g_ref[...] * (1.0 - lam_init)).astype(o_ref.dtype)


def _attn_prompt_call(q, k, v, slopes, lq, lk, g2, n, t, lam_init):
    tq, tk = ATTN_TQ, min(ATTN_TK, t)
    tq = min(tq, tk)
    kern = functools.partial(_attn_prompt_kernel, tq=tq, tk=tk, lam_init=lam_init)
    pos = jnp.arange(t)
    k_feat = jnp.zeros((t, LANES), F32).at[:, 0].set(pos // 256).at[:, 1].set(pos % 256).astype(BF16)
    q_feat = jnp.zeros((H_A, LANES), F32).at[:, 0].set(256.0 * slopes).at[:, 1].set(slopes)
    v_t = v.reshape(n, t // tk, tk, W_A).transpose(0, 1, 3, 2)
    qspec = pl.BlockSpec((None, tq, LANES), lambda b, hp, qi: (b, qi, hp))
    return pl.pallas_call(
        kern,
        grid=(n, H_A // 2, t // tq),
        in_specs=[_full_spec((2, DK_A)), _full_spec((2, DK_A)), _full_spec((1, LANES)),
                  _full_spec((H_A, LANES)), qspec,
                  pl.BlockSpec((None, t, LANES), lambda b, hp, qi: (b, 0, hp)),
                  _full_spec((t, LANES)),
                  pl.BlockSpec((None, t // tk, LANES, tk), lambda b, hp, qi: (b, 0, hp, 0))],
        out_specs=qspec,
        out_shape=jax.ShapeDtypeStruct((n, t, W_A), BF16),
        scratch_shapes=[pltpu.VMEM((4, 1, tq), F32), pltpu.VMEM((4, 1, tq), F32),
                        pltpu.VMEM((4, DV_A, tq), F32)],
        compiler_params=_cparams("parallel", "parallel", "parallel"),
        name="diff_attn_prompt",
    )(lq, lk, g2, q_feat, q.reshape(n, t, QK_A), k.reshape(n, t, QK_A), k_feat, v_t)


def _attn_sample_kernel(pt_ref, lq_ref, lk_ref, g_ref, q_ref, kn_ref, vn_ref, *rest,
                        pages, steps, past_len, t_new, lam_init):
    k_pages, v_pages = rest[:pages], rest[pages:2 * pages]
    o_ref, m_scr, l_scr, acc_scr = rest[2 * pages:]
    step = pl.program_id(1)
    hr = 2 * t_new
    row = lax.broadcasted_iota(jnp.int32, (H_A * hr, 1), 0)
    slope = jnp.exp2(-(row // hr + 1).astype(F32))

    @pl.when(step == 0)
    def _():
        m_scr[...] = jnp.full(m_scr.shape, NEG_INF, F32)
        l_scr[...] = jnp.zeros(l_scr.shape, F32)
        acc_scr[...] = jnp.zeros(acc_scr.shape, F32)

    def update(scores, values_t, keep):
        s = jnp.concatenate(scores, axis=0)
        if keep is not None:
            s = jnp.where(keep, s, NEG_INF)
        m_old = m_scr[...]
        m_new = jnp.maximum(m_old, jnp.max(s, axis=1, keepdims=True))
        alpha = jnp.exp(m_old - m_new)
        p = jnp.exp(s - m_new)
        l_scr[...] = alpha * l_scr[...] + jnp.sum(p, axis=1, keepdims=True)
        p = p.astype(BF16)
        pv = jnp.concatenate([_dot_nt(p[h * hr:(h + 1) * hr], values_t[h]) for h in range(H_A)], axis=0)
        acc_scr[...] = alpha * acc_scr[...] + pv
        m_scr[...] = m_new

    col = lax.broadcasted_iota(jnp.int32, (1, pages * PAGE_SIZE), 1)
    k_rel = (step * (pages * PAGE_SIZE) - past_len + col).astype(F32)
    scores, values_t = [], []
    for h in range(H_A):
        k_t = jnp.concatenate([kp[h] for kp in k_pages], axis=1).astype(BF16)
        values_t.append(jnp.concatenate([vp[h] for vp in v_pages], axis=1).astype(BF16))
        scores.append(_dot(q_ref[h], k_t))
    update([s + slope[h * hr:(h + 1) * hr] * k_rel for h, s in enumerate(scores)], values_t, None)

    @pl.when(step == steps - 1)
    def _():
        new_col = lax.broadcasted_iota(jnp.int32, (1, PAGE_SIZE), 1)
        keep = new_col <= row % t_new
        bias = slope * new_col.astype(F32)
        update([_dot(q_ref[h], kn_ref[h]) + bias[h * hr:(h + 1) * hr] for h in range(H_A)],
               [vn_ref[h] for h in range(H_A)], keep)
        lam = _lambda(lq_ref, lk_ref, lam_init)
        a = acc_scr[...] / l_scr[...]
        for h in range(H_A):
            oh = a[h * hr:h * hr + t_new] - lam * a[h * hr + t_new:(h + 1) * hr]
            ms = jnp.mean(oh * oh, axis=1, keepdims=True)
            o_ref[h] = oh * lax.rsqrt(ms + RMS_EPS) * g_ref[...] * (1.0 - lam_init)


def _attn_sample_call(q, k_new, v_new, cache_k, cache_v, page_table, j, lq, lk, g1, n, t, lam_init):
    n_pages = page_table.shape[1]
    pages = PAGES_PER_STEP
    steps = n_pages // pages
    n_pool = cache_k.shape[1]
    ck = cache_k.transpose(0, 1, 3, 4, 2)
    cv = cache_v.transpose(0, 1, 3, 4, 2)
    kern = functools.partial(_attn_sample_kernel, pages=pages, steps=steps,
                             past_len=n_pages * PAGE_SIZE, t_new=t, lam_init=lam_init)
    qh = jnp.einsum('nthmd,km->nhktmd', q.reshape(n, t, H_A, 2, DK_A), jnp.eye(2, dtype=q.dtype))
    qh = qh.reshape(n, H_A, 2 * t, 2 * DK_A)

    def new_page(x):
        x = x.reshape(n, t, H_A, DV_A).transpose(0, 2, 3, 1)
        return jnp.pad(x, ((0, 0), (0, 0), (0, 0), (0, PAGE_SIZE - t)))

    def page_spec(i):
        return pl.BlockSpec((None, None, H_A, DV_A, PAGE_SIZE),
                            lambda b, s, pt: (j, pt[b, s * pages + i], 0, 0, 0))

    small = lambda shape: pl.BlockSpec(shape, lambda b, s, pt: (0,) * len(shape))
    head_spec = lambda r: pl.BlockSpec((None, H_A, r, DV_A), lambda b, s, pt: (b, 0, 0, 0))
    new_spec = pl.BlockSpec((None, H_A, DV_A, PAGE_SIZE), lambda b, s, pt: (b, 0, 0, 0))
    grid_spec = pltpu.PrefetchScalarGridSpec(
        num_scalar_prefetch=1,
        grid=(n, steps),
        in_specs=[small((2, DK_A)), small((2, DK_A)), small((1, DV_A)),
                  head_spec(2 * t), new_spec, new_spec]
                 + [page_spec(i) for i in range(pages)] * 2,
        out_specs=head_spec(t),
        scratch_shapes=[pltpu.VMEM((H_A * 2 * t, 1), F32), pltpu.VMEM((H_A * 2 * t, 1), F32),
                        pltpu.VMEM((H_A * 2 * t, DV_A), F32)],
    )
    o = pl.pallas_call(
        kern,
        grid_spec=grid_spec,
        out_shape=jax.ShapeDtypeStruct((n, H_A, t, DV_A), F32),
        compiler_params=_cparams("parallel", "arbitrary"),
        name="diff_attn_paged",
    )(page_table, lq, lk, g1, qh, new_page(k_new), new_page(v_new), *([ck] * pages), *([cv] * pages))
    return o.transpose(0, 2, 1, 3).reshape(n * t, W_A).astype(BF16)


def _rwkv_prep_kernel(pb_ref, first_ref, mu_ref, wl_ref, w0_ref, a0_ref, kk_w_ref, ka_w_ref, rk_ref,
                      seg_ref, r_o, w_o, k_o, v_o, kk_o, b_o, bonus_o, g_o, *, period):
    pb = pb_ref[...]
    tm = pb.shape[0]
    row = lax.broadcasted_iota(jnp.int32, (tm, 1), 0)
    prev = jnp.where(row % period == 0, first_ref[...], pltpu.roll(pb, 1, 0))
    xb = pb + (prev - pb) * mu_ref[...]
    r = xb[:, :W_B]
    kb = xb[:, W_B:2 * W_B]
    vb = xb[:, 2 * W_B:3 * W_B]
    z = xb[:, 3 * W_B:]
    ll = lax.broadcasted_iota(jnp.int32, (1, LORA_COLS), 1)
    zz = jnp.where(ll < LORA_W, jnp.tanh(z), jnp.where(ll < LORA_W + LORA_A, z, jax.nn.sigmoid(z)))
    lo = _dot(zz.astype(BF16), wl_ref[...])
    decay = jnp.exp(-DECAY_SCALE * jax.nn.sigmoid(w0_ref[...] + lo[:, :W_B]))
    a = jax.nn.sigmoid(a0_ref[...] + lo[:, W_B:2 * W_B])
    seg = seg_ref[...]
    kk = kb * kk_w_ref[...]
    kk = kk / jnp.maximum(jnp.sqrt(_segsum(kk * kk, seg)), 1e-12)
    k_adj = kb * (1.0 + (a - 1.0) * ka_w_ref[...])
    r_o[...] = r
    w_o[...] = decay
    k_o[...] = k_adj
    v_o[...] = vb
    kk_o[...] = kk
    b_o[...] = kk * a
    bonus_o[...] = _segsum(r * k_adj * rk_ref[...], seg) * vb
    g_o[...] = lo[:, 2 * W_B:]


def _rwkv_prep_call(pb, first, p, j, seg, grp):
    tm, rows = grp.tm, grp.rows
    period = tm if grp.mod_rows == 1 else grp.t
    first_rows = first.shape[1]
    vec = lambda x: x[j].reshape(1, -1)
    wl = jnp.zeros((LORA_COLS, 3 * W_B), F32)
    wl = wl.at[:LORA_W, :W_B].set(p['w_up'][j])
    wl = wl.at[LORA_W:LORA_W + LORA_A, W_B:2 * W_B].set(p['a_up'][j])
    wl = wl.at[LORA_W + LORA_A:, 2 * W_B:].set(p['g_up'][j])
    outs = [jax.ShapeDtypeStruct((rows, W_B), F32)] * 8
    return pl.pallas_call(
        functools.partial(_rwkv_prep_kernel, period=period),
        grid=(grp.tiles,),
        in_specs=[_row_spec(tm, B_COLS),
                  pl.BlockSpec((None, first_rows, B_COLS), lambda i: (i, 0, 0)),
                  _full_spec((1, B_COLS)), _full_spec((LORA_COLS, 3 * W_B)),
                  _full_spec((1, W_B)), _full_spec((1, W_B)), _full_spec((1, W_B)),
                  _full_spec((1, W_B)), _full_spec((1, W_B)), _full_spec((W_B, W_B))],
        out_specs=[_row_spec(tm, W_B)] * 8,
        out_shape=outs,
        compiler_params=_cparams("parallel"),
        name="rwkv_prep",
    )(pb, first, vec(p['mu_b']), wl.astype(BF16), vec(p['w0']), vec(p['a0']), vec(p['k_k']),
      vec(p['k_a']), vec(p['r_k']), seg)


def _rwkv_scan_kernel(kk_ref, w_ref, b_ref, k_ref, r_ref, v_ref, s0_ref, y_ref, s_ref, *, tc):
    @pl.when(pl.program_id(1) == 0)
    def _():
        s_ref[...] = s0_ref[...]

    n_acc = 4

    def step(t, carry):
        parts = [None] * n_acc
        for k in range(N_B):
            term = s_ref[k] * kk_ref[k, pl.ds(t, 1), :]
            parts[k % n_acc] = term if parts[k % n_acc] is None else parts[k % n_acc] + term
        sa = (parts[0] + parts[1]) + (parts[2] + parts[3])
        v_t = v_ref[t]
        parts = [None] * n_acc
        for k in range(N_B):
            s_new = (s_ref[k] * w_ref[k, pl.ds(t, 1), :] - sa * b_ref[k, pl.ds(t, 1), :]
                     + v_t * k_ref[k, pl.ds(t, 1), :])
            s_ref[k] = s_new
            term = s_new * r_ref[k, pl.ds(t, 1), :]
            parts[k % n_acc] = term if parts[k % n_acc] is None else parts[k % n_acc] + term
        y_ref[t] = (parts[0] + parts[1]) + (parts[2] + parts[3])
        return carry

    lax.fori_loop(0, tc, step, 0)


def _to_chain_layout(x, n, t):
    groups = n * H_B // CHAINS
    x = x.reshape(n, t, H_B, N_B).transpose(3, 1, 0, 2).reshape(N_B, t, groups, CHAINS)
    x = x.transpose(2, 0, 1, 3)
    return jnp.concatenate([x, x], axis=-1)


def _value_to_chain_layout(v, n, t):
    groups = n * H_B // CHAINS
    v = v.reshape(n, t, H_B, N_B).transpose(1, 3, 0, 2).reshape(t, 2, N_B // 2, groups, CHAINS)
    return v.transpose(3, 0, 2, 1, 4).reshape(groups, t, N_B // 2, 2 * CHAINS)


def _value_from_chain_layout(y, n, t):
    groups = n * H_B // CHAINS
    y = y.reshape(groups, t, N_B // 2, 2, CHAINS).transpose(1, 3, 2, 0, 4)
    y = y.reshape(t, N_B, n, H_B).transpose(2, 0, 3, 1)
    return y.reshape(n * t, W_B)


def _state_to_chain_layout(s, n):
    groups = n * H_B // CHAINS
    s = s.reshape(groups, CHAINS, 2, N_B // 2, N_B)
    return s.transpose(0, 4, 3, 2, 1).reshape(groups, N_B, N_B // 2, 2 * CHAINS)


def _state_from_chain_layout(s, n):
    groups = n * H_B // CHAINS
    s = s.reshape(groups, N_B, N_B // 2, 2, CHAINS).transpose(0, 4, 3, 2, 1)
    return s.reshape(n, H_B, N_B, N_B)


def _chain_in_kernel(x_ref, o_ref, z_scr, *, per_key, tt):
    n_seq = x_ref.shape[0]
    rows_out = N_B if per_key else N_B // 2
    for s in range(n_seq):
        z_scr[s] = x_ref[s].T
    for j in range(rows_out):
        rows = []
        for half in range(2):
            start = j if per_key else half * (N_B // 2) + j
            rows += [z_scr[s, pl.ds(start, H_B, stride=N_B), :] for s in range(n_seq)]
        tile = jnp.concatenate(rows, axis=0).T
        if per_key:
            o_ref[j] = tile
        else:
            o_ref[pl.ds(j, tt, stride=rows_out), :] = tile


def _chain_in_call(x, n, t, per_key):
    tt = LANES
    if per_key:
        out_spec = pl.BlockSpec((N_B, tt, 2 * CHAINS), lambda i: (0, i, 0))
        out_shape = jax.ShapeDtypeStruct((N_B, t, 2 * CHAINS), F32)
    else:
        out_spec = pl.BlockSpec((tt * (N_B // 2), 2 * CHAINS), lambda i: (i, 0))
        out_shape = jax.ShapeDtypeStruct((t * (N_B // 2), 2 * CHAINS), F32)
    out = pl.pallas_call(
        functools.partial(_chain_in_kernel, per_key=per_key, tt=tt),
        grid=(t // tt,),
        in_specs=[pl.BlockSpec((n, tt, W_B), lambda i: (0, i, 0))],
        out_specs=out_spec,
        out_shape=out_shape,
        scratch_shapes=[pltpu.VMEM((n, W_B, tt), F32)],
        compiler_params=_cparams("parallel"),
        name="chain_layout_in",
    )(x.reshape(n, t, W_B))
    return out[None] if per_key else out.reshape(1, t, N_B // 2, 2 * CHAINS)


def _chain_out_kernel(y_ref, o_ref, z_scr, *, tt):
    n_seq = o_ref.shape[0]
    rows = N_B // 2
    for j in range(rows):
        tile = y_ref[pl.ds(j, tt, stride=rows), :].T
        for half in range(2):
            for s in range(n_seq):
                r0 = (half * n_seq + s) * H_B
                z_scr[s, pl.ds(half * rows + j, H_B, stride=N_B), :] = tile[r0:r0 + H_B]
    for s in range(n_seq):
        o_ref[s] = z_scr[s].T


def _chain_out_call(y, n, t):
    tt = LANES
    rows = N_B // 2
    out = pl.pallas_call(
        functools.partial(_chain_out_kernel, tt=tt),
        grid=(t // tt,),
        in_specs=[pl.BlockSpec((tt * rows, 2 * CHAINS), lambda i: (i, 0))],
        out_specs=pl.BlockSpec((n, tt, W_B), lambda i: (0, i, 0)),
        out_shape=jax.ShapeDtypeStruct((n, t, W_B), F32),
        scratch_shapes=[pltpu.VMEM((n, W_B, tt), F32)],
        compiler_params=_cparams("parallel"),
        name="chain_layout_out",
    )(y.reshape(t * rows, 2 * CHAINS))
    return out.reshape(n * t, W_B)


def _rwkv_scan_call(kk, w, b, k, r, v, s0, n, t):
    groups = n * H_B // CHAINS
    tc = min(SCAN_TC, t)
    vec_spec = pl.BlockSpec((None, N_B, tc, 2 * CHAINS), lambda g, i: (g, 0, i, 0))
    val_spec = pl.BlockSpec((None, tc, N_B // 2, 2 * CHAINS), lambda g, i: (g, i, 0, 0))
    st_spec = pl.BlockSpec((None, N_B, N_B // 2, 2 * CHAINS), lambda g, i: (g, 0, 0, 0))
    in_kernel_layout = groups == 1 and t % LANES == 0
    if in_kernel_layout:
        vecs = [_chain_in_call(x, n, t, True) for x in (kk, w, b, k, r)]
        val = _chain_in_call(v, n, t, False)
    else:
        vecs = [_to_chain_layout(x, n, t) for x in (kk, w, b, k, r)]
        val = _value_to_chain_layout(v, n, t)
    y, s_fin = pl.pallas_call(
        functools.partial(_rwkv_scan_kernel, tc=tc),
        grid=(groups, t // tc),
        in_specs=[vec_spec] * 5 + [val_spec, st_spec],
        out_specs=[val_spec, st_spec],
        out_shape=[jax.ShapeDtypeStruct((groups, t, N_B // 2, 2 * CHAINS), F32),
                   jax.ShapeDtypeStruct((groups, N_B, N_B // 2, 2 * CHAINS), F32)],
        compiler_params=_cparams("parallel", "arbitrary"),
        name="rwkv_scan",
    )(*vecs, val, _state_to_chain_layout(s0, n))
    y = _chain_out_call(y, n, t) if in_kernel_layout else _value_from_chain_layout(y, n, t)
    return y, _state_from_chain_layout(s_fin, n)


def _rwkv_post_kernel(y_ref, bonus_ref, g_ref, lg_ref, lb_ref, seg_ref, o_ref):
    y = y_ref[...]
    seg = seg_ref[...]
    yc = y - _segsum(y, seg) * (1.0 / N_B)
    var = _segsum(yc * yc, seg) * (1.0 / N_B)
    yn = yc * lax.rsqrt(var + LNX_EPS) * lg_ref[...] + lb_ref[...]
    o_ref[...] = ((yn + bonus_ref[...]) * g_ref[...]).astype(o_ref.dtype)


def _rwkv_post_call(y, bonus, g, lg, lb, seg, grp):
    tm = grp.tm
    return pl.pallas_call(
        _rwkv_post_kernel,
        grid=(grp.tiles,),
        in_specs=[_row_spec(tm, W_B)] * 3 + [_full_spec((1, W_B)), _full_spec((1, W_B)),
                                              _full_spec((W_B, W_B))],
        out_specs=_row_spec(tm, W_B),
        out_shape=jax.ShapeDtypeStruct((grp.rows, W_B), BF16),
        compiler_params=_cparams("parallel"),
        name="rwkv_post",
    )(y, bonus, g, lg[None, :], lb[None, :], seg)


def _norm_mod(x, g, sc, sh):
    y = x * lax.rsqrt(jnp.mean(x * x, axis=-1, keepdims=True) + RMS_EPS) * g
    return y * (1.0 + sc) + sh


def _norm_specs(grp):
    return [_full_spec((1, D_MODEL)), _mod_spec(grp), _mod_spec(grp)]


def _residual_out(grp, hn_dtype):
    return dict(out_specs=[_row_spec(grp.tm, D_MODEL), _row_spec(grp.tm, D_MODEL)],
                out_shape=[jax.ShapeDtypeStruct((grp.rows, D_MODEL), F32),
                           jax.ShapeDtypeStruct((grp.rows, D_MODEL), hn_dtype)])


def _outproj_even_kernel(oa_ref, ob_ref, w_ref, x_ref, g_ref, ng_ref, nsc_ref, nsh_ref, o_ref, hn_ref):
    mix = _dot(oa_ref[...], w_ref[:W_A, :]) + _dot(ob_ref[...], w_ref[W_A:, :])
    x = x_ref[...] + g_ref[...] * mix
    o_ref[...] = x
    hn_ref[...] = _norm_mod(x, ng_ref[...], nsc_ref[...], nsh_ref[...]).astype(hn_ref.dtype)


def _outproj_even_call(oa, ob, w, x, gate, norm, grp):
    tm = grp.tm
    ng, nsc, nsh, hn_dtype = norm
    return pl.pallas_call(
        _outproj_even_kernel,
        grid=(grp.tiles,),
        in_specs=[_row_spec(tm, W_A), _row_spec(tm, W_B), _full_spec((W_A + W_B, D_MODEL)),
                  _row_spec(tm, D_MODEL), _mod_spec(grp)] + _norm_specs(grp),
        compiler_params=_cparams("parallel"),
        name="outproj_even",
        **_residual_out(grp, hn_dtype),
    )(oa, ob, w, x, gate, ng[None, :], nsc, nsh)


def _swiglu_up_kernel(a_ref, wg_ref, wu_ref, o_ref):
    a = a_ref[...]
    o_ref[...] = (_silu(_dot(a, wg_ref[...])) * _dot(a, wu_ref[...])).astype(o_ref.dtype)


def _swiglu_up_call(hn, wg, wu, grp):
    tm = grp.tm
    tn = D_FF // 2
    return pl.pallas_call(
        _swiglu_up_kernel,
        grid=(D_FF // tn, grp.tiles),
        in_specs=[pl.BlockSpec((tm, D_MODEL), lambda c, i: (i, 0)),
                  pl.BlockSpec((D_MODEL, tn), lambda c, i: (0, c)),
                  pl.BlockSpec((D_MODEL, tn), lambda c, i: (0, c))],
        out_specs=pl.BlockSpec((tm, tn), lambda c, i: (i, c)),
        out_shape=jax.ShapeDtypeStruct((grp.rows, D_FF), BF16),
        compiler_params=_cparams("parallel", "parallel"),
        name="swiglu_up",
    )(hn, wg, wu)


def _down_res_kernel(h_ref, w_ref, x_ref, g_ref, ng_ref, nsc_ref, nsh_ref, o_ref, hn_ref):
    x = x_ref[...] + g_ref[...] * _dot(h_ref[...], w_ref[...])
    o_ref[...] = x
    hn_ref[...] = _norm_mod(x, ng_ref[...], nsc_ref[...], nsh_ref[...]).astype(hn_ref.dtype)


def _down_res_call(h, w, x, gate, norm, grp):
    tm = grp.tm
    kdim = h.shape[1]
    ng, nsc, nsh, hn_dtype = norm
    return pl.pallas_call(
        _down_res_kernel,
        grid=(grp.tiles,),
        in_specs=[_row_spec(tm, kdim), _full_spec((kdim, D_MODEL)), _row_spec(tm, D_MODEL), _mod_spec(grp)]
                 + _norm_specs(grp),
        compiler_params=_cparams("parallel"),
        name="down_res",
        **_residual_out(grp, hn_dtype),
    )(h, w, x, gate, ng[None, :], nsc, nsh)


def _gelu(x):
    return 0.5 * x * (1.0 + lax.erf(x * INV_SQRT2))


def _gmlp_u_kernel(a_ref, w_ref, o_ref):
    o_ref[...] = _gelu(_dot(a_ref[...], w_ref[...]))


def _gmlp_v_kernel(a_ref, w_ref, lg_ref, lb_ref, vb_ref, vf_ref):
    v = _gelu(_dot(a_ref[...], w_ref[...]))
    vc = v - jnp.mean(v, axis=-1, keepdims=True)
    vn = vc * lax.rsqrt(jnp.mean(vc * vc, axis=-1, keepdims=True) + LN_EPS) * lg_ref[...] + lb_ref[...]
    vb_ref[...] = vn.astype(BF16)
    vf_ref[...] = vn


def _gmlp_in_call(hn, w, lg, lb, grp):
    tm, rows = grp.tm, grp.rows
    wspec = lambda half: pl.BlockSpec((D_MODEL, E_C), lambda i: (0, half))
    u = pl.pallas_call(
        _gmlp_u_kernel,
        grid=(grp.tiles,),
        in_specs=[_row_spec(tm, D_MODEL), wspec(0)],
        out_specs=_row_spec(tm, E_C),
        out_shape=jax.ShapeDtypeStruct((rows, E_C), F32),
        compiler_params=_cparams("parallel"),
        name="gmlp_u",
    )(hn, w)
    vb, vf = pl.pallas_call(
        _gmlp_v_kernel,
        grid=(grp.tiles,),
        in_specs=[_row_spec(tm, D_MODEL), wspec(1), _full_spec((1, E_C)), _full_spec((1, E_C))],
        out_specs=[_row_spec(tm, E_C), _row_spec(tm, E_C)],
        out_shape=[jax.ShapeDtypeStruct((rows, E_C), BF16), jax.ShapeDtypeStruct((rows, E_C), F32)],
        compiler_params=_cparams("parallel"),
        name="gmlp_v",
    )(hn, w, lg[None, :], lb[None, :])
    return u, vb, vf


def _gmlp_out_kernel(u_ref, v_ref, ws_ref, bs_ref, wo_ref, x_ref, g_ref, o_ref, z_scr, *, lm):
    tm = u_ref.shape[0]
    for c in range(tm // lm):
        r0 = c * lm
        for gi in range(G_C):
            c0 = gi * CG
            mixed = _dot(ws_ref[gi], v_ref[r0:r0 + lm, c0:c0 + CG]) + bs_ref[:, gi:gi + 1]
            z_scr[r0:r0 + lm, c0:c0 + CG] = (u_ref[r0:r0 + lm, c0:c0 + CG] * mixed).astype(BF16)
    o_ref[...] = x_ref[...] + g_ref[...] * _dot(z_scr[...], wo_ref[...])


def _gmlp_out_call(u, vb, ws, bs, wo, x, gate, grp):
    tm = grp.tm
    lm = ws.shape[1]
    return pl.pallas_call(
        functools.partial(_gmlp_out_kernel, lm=lm),
        grid=(grp.tiles,),
        in_specs=[_row_spec(tm, E_C), _row_spec(tm, E_C), _full_spec((G_C, lm, lm)), _full_spec((lm, G_C)),
                  _full_spec((E_C, D_MODEL)), _row_spec(tm, D_MODEL), _mod_spec(grp)],
        out_specs=_row_spec(tm, D_MODEL),
        out_shape=jax.ShapeDtypeStruct((grp.rows, D_MODEL), F32),
        scratch_shapes=[pltpu.VMEM((tm, E_C), BF16)],
        compiler_params=_cparams("parallel"),
        name="gmlp_out",
    )(u, vb, ws, bs, wo, x, gate)


def _spatial_weights(w_s, b_s, grp):
    length = min(grp.t, CHUNK)
    ws = jnp.tril(w_s[:, :length, :length])
    bs = b_s[:, :length]
    if length < CHUNK:
        reps = grp.tm // length
        ws = jnp.einsum('ab,gij->gaibj', jnp.eye(reps, dtype=ws.dtype), ws).reshape(
            G_C, reps * length, reps * length)
        bs = jnp.tile(bs, (1, reps))
    return ws.astype(BF16), bs.T


def _moe_kernel(a_ref, logit_ref, tri_ref, wg_ref, wu_ref, wd_ref, x_ref, g_ref, ng_ref, nsc_ref, nsh_ref,
                o_ref, hn_ref, gate_scr, rank_scr, gate_t_scr, rank_t_scr, acc_scr, *, tc):
    e = pl.program_id(1)
    lane = lax.broadcasted_iota(jnp.int32, (1, LANES), 1)

    @pl.when(e == 0)
    def _():
        logits = logit_ref[...]
        m1 = jnp.max(logits, axis=1, keepdims=True)
        i1 = jnp.min(jnp.where(logits == m1, lane, LANES), axis=1, keepdims=True)
        rest = jnp.where(lane == i1, -jnp.inf, logits)
        m2 = jnp.max(rest, axis=1, keepdims=True)
        i2 = jnp.min(jnp.where(rest == m2, lane, LANES), axis=1, keepdims=True)
        e2 = jnp.exp(m2 - m1)
        gate = jnp.where(lane == i1, 1.0 / (1.0 + e2), jnp.where(lane == i2, e2 / (1.0 + e2), 0.0))
        picked = gate != 0.0
        rank = _dot(tri_ref[...], jnp.where(picked, 1.0, 0.0).astype(BF16))
        rank = jnp.where(picked, rank, -1.0)
        gate_scr[...] = gate
        rank_scr[...] = rank
        gate_t_scr[...] = gate.T
        rank_t_scr[...] = rank.T
        acc_scr[...] = jnp.zeros(acc_scr.shape, F32)

    gate_col = jnp.sum(jnp.where(lane == e, gate_scr[...], 0.0), axis=1, keepdims=True)
    rank_col = jnp.sum(jnp.where(lane == e, rank_scr[...], 0.0), axis=1, keepdims=True)
    gate_row = gate_t_scr[pl.ds(e, 1), :]
    rank_row = rank_t_scr[pl.ds(e, 1), :]
    count = jnp.sum(jnp.where(rank_row >= 0.0, 1, 0))
    slot_col = lax.broadcasted_iota(jnp.int32, (tc, 1), 0).astype(F32)
    slot_row = lax.broadcasted_iota(jnp.int32, (1, tc), 1).astype(F32)

    def chunk(c, carry):
        base = (c * tc).astype(F32)
        take = rank_row == base + slot_col
        xg = _dot(jnp.where(take, 1.0, 0.0).astype(BF16), a_ref[...]).astype(BF16)
        h = (_silu(_dot(xg, wg_ref[...])) * _dot(xg, wu_ref[...])).astype(BF16)
        y = _dot(h, wd_ref[...])
        gate_c = jnp.sum(jnp.where(take, gate_row, 0.0), axis=1, keepdims=True)
        put = jnp.where(rank_col == base + slot_row, 1.0, 0.0).astype(BF16)
        acc_scr[...] += _dot(put, (gate_c * y).astype(BF16))
        return carry

    lax.fori_loop(0, (count + tc - 1) // tc, chunk, 0)

    @pl.when(e == N_EXPERTS - 1)
    def _():
        x = x_ref[...] + g_ref[...] * acc_scr[...]
        o_ref[...] = x
        hn_ref[...] = _norm_mod(x, ng_ref[...], nsc_ref[...], nsh_ref[...]).astype(hn_ref.dtype)


def _moe_call(hn, logits, wg, wu, wd, j, x, gate, norm, grp):
    tm = grp.tm
    tps = grp.tiles_per_seq
    tc = min(MOE_CHUNK, tm)
    ng, nsc, nsh, hn_dtype = norm
    mod = pl.BlockSpec((None, grp.mod_rows, D_MODEL), lambda i, e: (i // tps, 0, 0))
    tri = jnp.tril(jnp.ones((tm, tm), F32), -1).astype(BF16)
    row = lambda cols: pl.BlockSpec((tm, cols), lambda i, e: (i, 0))
    return pl.pallas_call(
        functools.partial(_moe_kernel, tc=tc),
        grid=(grp.tiles, N_EXPERTS),
        in_specs=[row(D_MODEL), row(LANES), pl.BlockSpec((tm, tm), lambda i, e: (0, 0)),
                  pl.BlockSpec((None, None, D_MODEL, D_FF_E), lambda i, e: (j, e, 0, 0)),
                  pl.BlockSpec((None, None, D_MODEL, D_FF_E), lambda i, e: (j, e, 0, 0)),
                  pl.BlockSpec((None, None, D_FF_E, D_MODEL), lambda i, e: (j, e, 0, 0)),
                  row(D_MODEL), mod,
                  pl.BlockSpec((1, D_MODEL), lambda i, e: (0, 0)), mod, mod],
        out_specs=[row(D_MODEL), row(D_MODEL)],
        out_shape=[jax.ShapeDtypeStruct((grp.rows, D_MODEL), F32),
                   jax.ShapeDtypeStruct((grp.rows, D_MODEL), hn_dtype)],
        scratch_shapes=[pltpu.VMEM((tm, LANES), F32), pltpu.VMEM((tm, LANES), F32),
                        pltpu.VMEM((LANES, tm), F32), pltpu.VMEM((LANES, tm), F32),
                        pltpu.VMEM((tm, D_MODEL), F32)],
        compiler_params=_cparams("parallel", "arbitrary"),
        name="moe_experts",
    )(hn, logits, tri, wg, wu, wd, x, gate, ng[None, :], nsc, nsh)


def _forward(x, mods, p, wb, grp, cache):
    n, t = grp.n, grp.t
    seg = jnp.kron(jnp.eye(H_B, dtype=F32), jnp.ones((N_B, N_B), F32)).astype(BF16)
    slopes = jnp.exp2(-8.0 * jnp.arange(1, H_A + 1, dtype=F32) / H_A)
    new_k, new_v, new_wkv, new_shift, new_cv = [], [], [], [], []
    mods = [[_expand_mod(m, grp) for m in layer] for layer in mods]

    def next_norm(i):
        if i + 1 < DEPTH:
            sh, sc = mods[i + 1][0], mods[i + 1][1]
            return p['norm1_g'][i + 1], sc, sh, BF16
        zero = jnp.zeros_like(mods[i][0])
        return p['final_g'], zero, zero, F32

    hn = _norm_mod_call(x, p['norm1_g'][0], mods[0][1], mods[0][0], grp)
    for i in range(DEPTH):
        j = i // 2
        sh1, sc1, g1, sh2, sc2, g2 = mods[i]
        if i % 2 == 0:
            q, k, v, kb, vb, pb = _inproj_even_call(hn, wb['w_in_e'][j], grp)
            lam_init = 0.8 - 0.6 * math.exp(-0.3 * i)
            lq = jnp.stack([p['lam_q1'][j], p['lam_q2'][j]])
            lk = jnp.stack([p['lam_k1'][j], p['lam_k2'][j]])
            if cache is None:
                g2h = jnp.tile(p['subln_g'][j], 2)[None, :]
                oa = _attn_prompt_call(q, kb, vb, slopes, lq, lk, g2h, n, t, lam_init)
                wkv0 = jnp.zeros((n, H_B, N_B, N_B), F32)
                last = pb.reshape(grp.tiles, grp.tm, B_COLS)[:, -1, :]
                first = jnp.concatenate([jnp.zeros((1, B_COLS), F32), last[:-1]], axis=0)
                starts = (jnp.arange(grp.tiles) % grp.tiles_per_seq == 0)[:, None]
                first = jnp.where(starts, 0.0, first)[:, None, :]
            else:
                cache_k, cache_v, page_table, state_wkv, state_shift = cache
                oa = _attn_sample_call(q, kb, vb, cache_k, cache_v, page_table, j, lq, lk,
                                       p['subln_g'][j][None, :], n, t, lam_init)
                wkv0 = state_wkv[j]
                first = jnp.repeat(state_shift[j], t, axis=0)[None]
            r, w, kv, vv, kk, b, bonus, g = _rwkv_prep_call(pb, first, p, j, seg, grp)
            y, s_fin = _rwkv_scan_call(kk, w, b, kv, r, vv, wkv0, n, t)
            ob = _rwkv_post_call(y, bonus, g, p['lnx_g'][j], p['lnx_b'][j], seg, grp)
            new_k.append(k.reshape(n, t, H_A, 2 * DK_A))
            new_v.append(v.reshape(n, t, H_A, DV_A))
            new_wkv.append(s_fin)
            new_shift.append(pb.reshape(n, t, B_COLS)[:, -1])
            x, hn = _outproj_even_call(oa.reshape(grp.rows, W_A), ob, wb['w_out_e'][j], x, g1,
                                       (p['norm2_g'][i], sc2, sh2, BF16), grp)
            h = _swiglu_up_call(hn, wb['ffn_gate'][j], wb['ffn_up'][j], grp)
            x, hn = _down_res_call(h, wb['ffn_down'][j], x, g2, next_norm(i), grp)
        else:
            u, vb, vf = _gmlp_in_call(hn, wb['w_in_o'][j], p['lnv_g'][j], p['lnv_b'][j], grp)
            new_cv.append(vf.reshape(n, t, E_C))
            ws, bs = _spatial_weights(p['w_s'][j], p['b_s'][j], grp)
            x = _gmlp_out_call(u, vb, ws, bs, wb['w_out_o'][j], x, g1, grp)
            hn, logits = _norm_mod_router_call(x, p['norm2_g'][i], sc2, sh2, p['router_w'][j],
                                               p['router_b'][j], grp)
            moe_grp = grp if grp.mod_rows > 1 else _Group(n, t, min(MOE_TILE, t))
            x, hn = _moe_call(hn, logits, wb['exp_gate'], wb['exp_up'], wb['exp_down'], j, x, g2,
                              next_norm(i), moe_grp)
    return (hn.reshape(n, t, D_MODEL), jnp.stack(new_k), jnp.stack(new_v), jnp.stack(new_wkv),
            jnp.stack(new_shift), jnp.stack(new_cv))


_BF16_WEIGHTS = ('w_in_e', 'w_out_e', 'ffn_gate', 'ffn_up', 'ffn_down', 'w_in_o', 'w_out_o',
                 'exp_gate', 'exp_up', 'exp_down')


def _run(x_prompt, x_sample, cache_k, cache_v, state_wkv, state_shift, page_table, c_prompt, c_sample, p):
    n_p, t_p, _ = x_prompt.shape
    n_s, t_s, _ = x_sample.shape
    grp_p = _Group(n_p, t_p, min(ROW_TILE, t_p))
    grp_s = _Group(n_s, t_s, n_s * t_s)
    wb = {name: p[name].astype(BF16) for name in _BF16_WEIGHTS}
    mod = _mod_call(jnp.concatenate([c_prompt, c_sample], axis=0), p['w_mod'], p['b_mod'])
    mods_p = [jnp.split(mod[i, :n_p], 6, axis=-1) for i in range(DEPTH)]
    mods_s = [jnp.split(mod[i, n_p:], 6, axis=-1) for i in range(DEPTH)]
    y_p, k_p, v_p, wkv_p, sh_p, _ = _forward(x_prompt.reshape(n_p * t_p, D_MODEL), mods_p, p, wb, grp_p, None)
    y_s, k_s, v_s, wkv_s, sh_s, cv_s = _forward(
        x_sample.reshape(n_s * t_s, D_MODEL), mods_s, p, wb, grp_s,
        (cache_k, cache_v, page_table, state_wkv, state_shift))
    return (y_p, y_s, k_p, v_p, wkv_p, sh_p, k_s, v_s, wkv_s, sh_s, cv_s)


def kernel(x_prompt, x_sample, cache_k, cache_v, state_wkv, state_shift, page_table, c_prompt, c_sample,
           w_mod, b_mod, norm1_g, norm2_g, final_g, w_in_e, w_out_e, lam_q1, lam_k1, lam_q2, lam_k2,
           subln_g, mu_b, w0, w_up, a0, a_up, g_up, k_k, k_a, r_k, lnx_g, lnx_b, w_in_o, lnv_g, lnv_b,
           w_s, b_s, w_out_o, ffn_gate, ffn_up, ffn_down, router_w, router_b, exp_gate, exp_up, exp_down):
    p = dict(w_mod=w_mod, b_mod=b_mod, norm1_g=norm1_g, norm2_g=norm2_g, final_g=final_g,
             w_in_e=w_in_e, w_out_e=w_out_e, lam_q1=lam_q1, lam_k1=lam_k1, lam_q2=lam_q2,
             lam_k2=lam_k2, subln_g=subln_g, mu_b=mu_b, w0=w0, w_up=w_up, a0=a0, a_up=a_up,
             g_up=g_up, k_k=k_k, k_a=k_a, r_k=r_k, lnx_g=lnx_g, lnx_b=lnx_b, w_in_o=w_in_o,
             lnv_g=lnv_g, lnv_b=lnv_b, w_s=w_s, b_s=b_s, w_out_o=w_out_o, ffn_gate=ffn_gate,
             ffn_up=ffn_up, ffn_down=ffn_down, router_w=router_w, router_b=router_b,
             exp_gate=exp_gate, exp_up=exp_up, exp_down=exp_down)
    return _run(x_prompt, x_sample, cache_k, cache_v, state_wkv, state_shift, page_table,
                c_prompt, c_sample, p)
```

```python
import functools
import math
from typing import NamedTuple

import jax
import jax.numpy as jnp
from jax import lax
from jax.experimental import pallas as pl
from jax.experimental.pallas import tpu as pltpu

F32 = jnp.float32
BF16 = jnp.bfloat16

D_MODEL = 1024
DEPTH = 4
H_A, DK_A, DV_A = 8, 32, 64
QK_A = H_A * 2 * DK_A
W_A = H_A * DV_A
A_COLS = 2 * QK_A + W_A
H_B, N_B = 8, 64
W_B = H_B * N_B
LORA_W, LORA_A, LORA_G = 64, 64, 128
LORA_COLS = LORA_W + LORA_A + LORA_G
B_COLS = 3 * W_B + LORA_COLS
IN_COLS_EVEN = A_COLS + B_COLS
CHUNK, G_C = 128, 8
E_C = 2 * D_MODEL
CG = E_C // G_C
D_FF = 2816
N_EXPERTS, D_FF_E = 8, 1408
PAGE_SIZE = 128
RMS_EPS, LN_EPS, LNX_EPS = 1e-6, 1e-5, 64e-5
NEG_INF = -1e30
ATTN_SCALE = DK_A ** -0.5
DECAY_SCALE = math.exp(-0.5)
INV_SQRT2 = 2.0 ** -0.5

LANES = 128
SUBLANES = 8
MIB = 1 << 20
VMEM_LIMIT = 56 * MIB

ROW_TILE = 256
ATTN_TQ, ATTN_TK = 256, 256
PAGES_PER_STEP = 8
MOE_TILE = 512
MOE_CHUNK = 144
SCAN_TC = 32
CHAINS = 64


class _Group(NamedTuple):
    n: int
    t: int
    tm: int

    @property
    def rows(self):
        return self.n * self.t

    @property
    def tiles(self):
        return self.rows // self.tm

    @property
    def tiles_per_seq(self):
        return max(self.t // self.tm, 1)

    @property
    def mod_rows(self):
        return 1 if self.t >= self.tm else self.tm


def _cparams(*sem):
    return pltpu.CompilerParams(dimension_semantics=sem, vmem_limit_bytes=VMEM_LIMIT)


def _row_spec(tm, cols):
    return pl.BlockSpec((tm, cols), lambda i: (i, 0))


def _full_spec(shape):
    zeros = (0,) * len(shape)
    return pl.BlockSpec(shape, lambda *_: zeros)


def _mod_spec(grp):
    tps = grp.tiles_per_seq
    return pl.BlockSpec((None, grp.mod_rows, D_MODEL), lambda i: (i // tps, 0, 0))


def _expand_mod(m, grp):
    if grp.mod_rows == 1:
        return m[:, None, :]
    return jnp.repeat(m, grp.t, axis=0)[None]


def _dot(a, b):
    return jnp.dot(a, b, preferred_element_type=F32)


def _dot_nt(a, b):
    return lax.dot_general(a, b, (((1,), (1,)), ((), ())), preferred_element_type=F32)


def _split_bf16(x):
    hi = x.astype(BF16)
    lo = (x - hi.astype(F32)).astype(BF16)
    return hi, lo


def _segsum(x, seg):
    hi, lo = _split_bf16(x)
    return _dot(hi, seg) + _dot(lo, seg)


def _silu(x):
    return x * jax.nn.sigmoid(x)


def _mod_kernel(c_ref, w_ref, b_ref, o_ref):
    a_hi, a_lo = _split_bf16(_silu(c_ref[...]))
    w_hi, w_lo = _split_bf16(w_ref[...])
    o_ref[...] = _dot(a_hi, w_hi) + _dot(a_hi, w_lo) + _dot(a_lo, w_hi) + b_ref[...]


def _mod_call(c_all, w_mod, b_mod):
    rows = c_all.shape[0]
    tn = D_MODEL
    return pl.pallas_call(
        _mod_kernel,
        grid=(DEPTH, 6 * D_MODEL // tn),
        in_specs=[
            pl.BlockSpec((rows, D_MODEL), lambda i, j: (0, 0)),
            pl.BlockSpec((None, D_MODEL, tn), lambda i, j: (i, 0, j)),
            pl.BlockSpec((None, 1, tn), lambda i, j: (i, 0, j)),
        ],
        out_specs=pl.BlockSpec((None, rows, tn), lambda i, j: (i, 0, j)),
        out_shape=jax.ShapeDtypeStruct((DEPTH, rows, 6 * D_MODEL), F32),
        compiler_params=_cparams("parallel", "parallel"),
        name="adaln_mod",
    )(c_all, w_mod, b_mod[:, None, :])


def _norm_mod_kernel(x_ref, g_ref, sc_ref, sh_ref, o_ref):
    x = x_ref[...]
    y = x * lax.rsqrt(jnp.mean(x * x, axis=-1, keepdims=True) + RMS_EPS) * g_ref[...]
    o_ref[...] = (y * (1.0 + sc_ref[...]) + sh_ref[...]).astype(o_ref.dtype)


def _norm_mod_call(x, g, sc, sh, grp):
    return pl.pallas_call(
        _norm_mod_kernel,
        grid=(grp.tiles,),
        in_specs=[_row_spec(grp.tm, D_MODEL), _full_spec((1, D_MODEL)), _mod_spec(grp), _mod_spec(grp)],
        out_specs=_row_spec(grp.tm, D_MODEL),
        out_shape=jax.ShapeDtypeStruct((grp.rows, D_MODEL), BF16),
        compiler_params=_cparams("parallel"),
        name="norm_mod",
    )(x, g[None, :], sc, sh)


def _norm_mod_router_kernel(x_ref, g_ref, sc_ref, sh_ref, wr_ref, br_ref, o_ref, logit_ref):
    x = x_ref[...]
    y = x * lax.rsqrt(jnp.mean(x * x, axis=-1, keepdims=True) + RMS_EPS) * g_ref[...]
    hn = y * (1.0 + sc_ref[...]) + sh_ref[...]
    o_ref[...] = hn.astype(o_ref.dtype)
    h_hi, h_lo = _split_bf16(hn)
    w_hi, w_lo = _split_bf16(wr_ref[...])
    logit_ref[...] = _dot(h_hi, w_hi) + _dot(h_hi, w_lo) + _dot(h_lo, w_hi) + br_ref[...]


def _norm_mod_router_call(x, g, sc, sh, wr, br, grp):
    wr_pad = jnp.zeros((D_MODEL, LANES), F32).at[:, :N_EXPERTS].set(wr)
    br_pad = jnp.full((1, LANES), -jnp.inf, F32).at[0, :N_EXPERTS].set(br)
    return pl.pallas_call(
        _norm_mod_router_kernel,
        grid=(grp.tiles,),
        in_specs=[_row_spec(grp.tm, D_MODEL), _full_spec((1, D_MODEL)), _mod_spec(grp), _mod_spec(grp),
                  _full_spec((D_MODEL, LANES)), _full_spec((1, LANES))],
        out_specs=[_row_spec(grp.tm, D_MODEL), _row_spec(grp.tm, LANES)],
        out_shape=[jax.ShapeDtypeStruct((grp.rows, D_MODEL), BF16),
                   jax.ShapeDtypeStruct((grp.rows, LANES), F32)],
        compiler_params=_cparams("parallel"),
        name="norm_mod_router",
    )(x, g[None, :], sc, sh, wr_pad, br_pad)


def _inproj_even_kernel(a_ref, w_ref, q_ref, k_ref, v_ref, kb_ref, vb_ref, pb_ref):
    a = a_ref[...]
    q_ref[...] = (_dot(a, w_ref[:, :QK_A]) * ATTN_SCALE).astype(BF16)
    k = _dot(a, w_ref[:, QK_A:2 * QK_A])
    k_ref[...] = k
    kb_ref[...] = k.astype(BF16)
    v = _dot(a, w_ref[:, 2 * QK_A:A_COLS])
    v_ref[...] = v
    vb_ref[...] = v.astype(BF16)
    half = B_COLS // 2
    pb_ref[:, :half] = _dot(a, w_ref[:, A_COLS:A_COLS + half])
    pb_ref[:, half:] = _dot(a, w_ref[:, A_COLS + half:])


def _inproj_even_call(hn, w, grp):
    tm, rows = grp.tm, grp.rows
    sds = jax.ShapeDtypeStruct
    return pl.pallas_call(
        _inproj_even_kernel,
        grid=(grp.tiles,),
        in_specs=[_row_spec(tm, D_MODEL), _full_spec((D_MODEL, IN_COLS_EVEN))],
        out_specs=[_row_spec(tm, QK_A), _row_spec(tm, QK_A), _row_spec(tm, W_A),
                   _row_spec(tm, QK_A), _row_spec(tm, W_A), _row_spec(tm, B_COLS)],
        out_shape=[sds((rows, QK_A), BF16), sds((rows, QK_A), F32), sds((rows, W_A), F32),
                   sds((rows, QK_A), BF16), sds((rows, W_A), BF16), sds((rows, B_COLS), F32)],
        compiler_params=_cparams("parallel"),
        name="inproj_even",
    )(hn, w)


def _lambda(lq_ref, lk_ref, lam_init):
    t = jnp.sum(lq_ref[...] * lk_ref[...], axis=1, keepdims=True)
    e = jnp.exp(t)
    return e[0:1] - e[1:2] + lam_init


def _attn_prompt_kernel(lq_ref, lk_ref, g_ref, qf_ref, q_ref, k_ref, kf_ref, vt_ref, o_ref,
                        m_scr, l_scr, acc_scr, *, tq, tk, lam_init):
    hp = pl.program_id(1)
    qi = pl.program_id(2)
    lane = lax.broadcasted_iota(jnp.int32, (1, LANES), 1)
    q = q_ref[...]
    q_aug = []
    for i in range(4):
        q_map = jnp.where(lane // DK_A == i, q, jnp.zeros_like(q))
        feat = jnp.broadcast_to(qf_ref[pl.ds(2 * hp + i // 2, 1), :], (tq, LANES)).astype(BF16)
        q_aug.append(jnp.concatenate([q_map, feat], axis=1))
    m_scr[...] = jnp.full(m_scr.shape, NEG_INF, F32)
    l_scr[...] = jnp.zeros(l_scr.shape, F32)
    acc_scr[...] = jnp.zeros(acc_scr.shape, F32)
    row = lax.broadcasted_iota(jnp.int32, (tk, 1), 0)
    col = lax.broadcasted_iota(jnp.int32, (1, tq), 1)
    q0 = qi * tq

    def scores(kt):
        k0 = pl.multiple_of(kt * tk, tk)
        k_aug = jnp.concatenate([k_ref[pl.ds(k0, tk), :], kf_ref[pl.ds(k0, tk), :]], axis=1)
        return tuple(_dot_nt(k_aug, q_aug[i]) for i in range(4))

    def softmax_pv(kt, s_maps, masked):
        vt = vt_ref[kt]
        if masked:
            keep = (kt * tk + row) <= (q0 + col)
        for i in range(4):
            h = i // 2
            s = jnp.where(keep, s_maps[i], NEG_INF) if masked else s_maps[i]
            m_old = m_scr[i]
            m_new = jnp.maximum(m_old, jnp.max(s, axis=0, keepdims=True))
            alpha = jnp.exp(m_old - m_new)
            p = jnp.exp(s - m_new)
            l_scr[i] = alpha * l_scr[i] + jnp.sum(p, axis=0, keepdims=True)
            acc_scr[i] = alpha * acc_scr[i] + _dot(vt[h * DV_A:(h + 1) * DV_A, :], p.astype(BF16))
            m_scr[i] = m_new

    n_full = q0 // tk

    def body(kt, s_maps):
        s_next = scores(kt + 1)
        softmax_pv(kt, s_maps, False)
        return s_next

    s_last = lax.fori_loop(0, n_full, body, scores(0))
    softmax_pv(n_full, s_last, True)

    lam = _lambda(lq_ref, lk_ref, lam_init)
    heads = []
    for h in range(2):
        oh = acc_scr[2 * h] / l_scr[2 * h] - lam * (acc_scr[2 * h + 1] / l_scr[2 * h + 1])
        ms = jnp.mean(oh * oh, axis=0, keepdims=True)
        heads.append(oh * lax.rsqrt(ms + RMS_EPS))
    o = jnp.concatenate(heads, axis=0).T
    o_ref[...] = (o * g_ref[...] * (1.0 - lam_init)).astype(o_ref.dtype)


def _attn_prompt_call(q, k, v, slopes, lq, lk, g2, n, t, lam_init):
    tq, tk = ATTN_TQ, min(ATTN_TK, t)
    tq = min(tq, tk)
    kern = functools.partial(_attn_prompt_kernel, tq=tq, tk=tk, lam_init=lam_init)
    pos = jnp.arange(t)
    k_feat = jnp.zeros((t, LANES), F32).at[:, 0].set(pos // 256).at[:, 1].set(pos % 256).astype(BF16)
    q_feat = jnp.zeros((H_A, LANES), F32).at[:, 0].set(256.0 * slopes).at[:, 1].set(slopes)
    v_t = v.reshape(n, t // tk, tk, W_A).transpose(0, 1, 3, 2)
    qspec = pl.BlockSpec((None, tq, LANES), lambda b, hp, qi: (b, qi, hp))
    return pl.pallas_call(
        kern,
        grid=(n, H_A // 2, t // tq),
        in_specs=[_full_spec((2, DK_A)), _full_spec((2, DK_A)), _full_spec((1, LANES)),
                  _full_spec((H_A, LANES)), qspec,
                  pl.BlockSpec((None, t, LANES), lambda b, hp, qi: (b, 0, hp)),
                  _full_spec((t, LANES)),
                  pl.BlockSpec((None, t // tk, LANES, tk), lambda b, hp, qi: (b, 0, hp, 0))],
        out_specs=qspec,
        out_shape=jax.ShapeDtypeStruct((n, t, W_A), BF16),
        scratch_shapes=[pltpu.VMEM((4, 1, tq), F32), pltpu.VMEM((4, 1, tq), F32),
                        pltpu.VMEM((4, DV_A, tq), F32)],
        compiler_params=_cparams("parallel", "parallel", "parallel"),
        name="diff_attn_prompt",
    )(lq, lk, g2, q_feat, q.reshape(n, t, QK_A), k.reshape(n, t, QK_A), k_feat, v_t)


def _attn_sample_kernel(pt_ref, lq_ref, lk_ref, g_ref, q_ref, kn_ref, vn_ref, *rest,
                        pages, steps, past_len, t_new, lam_init):
    k_pages, v_pages = rest[:pages], rest[pages:2 * pages]
    o_ref, m_scr, l_scr, acc_scr = rest[2 * pages:]
    step = pl.program_id(1)
    hr = 2 * t_new
    row = lax.broadcasted_iota(jnp.int32, (H_A * hr, 1), 0)
    slope = jnp.exp2(-(row // hr + 1).astype(F32))

    @pl.when(step == 0)
    def _():
        m_scr[...] = jnp.full(m_scr.shape, NEG_INF, F32)
        l_scr[...] = jnp.zeros(l_scr.shape, F32)
        acc_scr[...] = jnp.zeros(acc_scr.shape, F32)

    def update(scores, values_t, keep):
        s = jnp.concatenate(scores, axis=0)
        if keep is not None:
            s = jnp.where(keep, s, NEG_INF)
        m_old = m_scr[...]
        m_new = jnp.maximum(m_old, jnp.max(s, axis=1, keepdims=True))
        alpha = jnp.exp(m_old - m_new)
        p = jnp.exp(s - m_new)
        l_scr[...] = alpha * l_scr[...] + jnp.sum(p, axis=1, keepdims=True)
        p = p.astype(BF16)
        pv = jnp.concatenate([_dot_nt(p[h * hr:(h + 1) * hr], values_t[h]) for h in range(H_A)], axis=0)
        acc_scr[...] = alpha * acc_scr[...] + pv
        m_scr[...] = m_new

    col = lax.broadcasted_iota(jnp.int32, (1, pages * PAGE_SIZE), 1)
    k_rel = (step * (pages * PAGE_SIZE) - past_len + col).astype(F32)
    scores, values_t = [], []
    for h in range(H_A):
        k_t = jnp.concatenate([kp[h] for kp in k_pages], axis=1).astype(BF16)
        values_t.append(jnp.concatenate([vp[h] for vp in v_pages], axis=1).astype(BF16))
        scores.append(_dot(q_ref[h], k_t))
    update([s + slope[h * hr:(h + 1) * hr] * k_rel for h, s in enumerate(scores)], values_t, None)

    @pl.when(step == steps - 1)
    def _():
        new_col = lax.broadcasted_iota(jnp.int32, (1, PAGE_SIZE), 1)
        keep = new_col <= row % t_new
        bias = slope * new_col.astype(F32)
        update([_dot(q_ref[h], kn_ref[h]) + bias[h * hr:(h + 1) * hr] for h in range(H_A)],
               [vn_ref[h] for h in range(H_A)], keep)
        lam = _lambda(lq_ref, lk_ref, lam_init)
        a = acc_scr[...] / l_scr[...]
        for h in range(H_A):
            oh = a[h * hr:h * hr + t_new] - lam * a[h * hr + t_new:(h + 1) * hr]
            ms = jnp.mean(oh * oh, axis=1, keepdims=True)
            o_ref[h] = oh * lax.rsqrt(ms + RMS_EPS) * g_ref[...] * (1.0 - lam_init)


def _attn_sample_call(q, k_new, v_new, cache_k, cache_v, page_table, j, lq, lk, g1, n, t, lam_init):
    n_pages = page_table.shape[1]
    pages = PAGES_PER_STEP
    steps = n_pages // pages
    n_pool = cache_k.shape[1]
    ck = cache_k.transpose(0, 1, 3, 4, 2)
    cv = cache_v.transpose(0, 1, 3, 4, 2)
    kern = functools.partial(_attn_sample_kernel, pages=pages, steps=steps,
                             past_len=n_pages * PAGE_SIZE, t_new=t, lam_init=lam_init)
    qh = jnp.einsum('nthmd,km->nhktmd', q.reshape(n, t, H_A, 2, DK_A), jnp.eye(2, dtype=q.dtype))
    qh = qh.reshape(n, H_A, 2 * t, 2 * DK_A)

    def new_page(x):
        x = x.reshape(n, t, H_A, DV_A).transpose(0, 2, 3, 1)
        return jnp.pad(x, ((0, 0), (0, 0), (0, 0), (0, PAGE_SIZE - t)))

    def page_spec(i):
        return pl.BlockSpec((None, None, H_A, DV_A, PAGE_SIZE),
                            lambda b, s, pt: (j, pt[b, s * pages + i], 0, 0, 0))

    small = lambda shape: pl.BlockSpec(shape, lambda b, s, pt: (0,) * len(shape))
    head_spec = lambda r: pl.BlockSpec((None, H_A, r, DV_A), lambda b, s, pt: (b, 0, 0, 0))
    new_spec = pl.BlockSpec((None, H_A, DV_A, PAGE_SIZE), lambda b, s, pt: (b, 0, 0, 0))
    grid_spec = pltpu.PrefetchScalarGridSpec(
        num_scalar_prefetch=1,
        grid=(n, steps),
        in_specs=[small((2, DK_A)), small((2, DK_A)), small((1, DV_A)),
                  head_spec(2 * t), new_spec, new_spec]
                 + [page_spec(i) for i in range(pages)] * 2,
        out_specs=head_spec(t),
        scratch_shapes=[pltpu.VMEM((H_A * 2 * t, 1), F32), pltpu.VMEM((H_A * 2 * t, 1), F32),
                        pltpu.VMEM((H_A * 2 * t, DV_A), F32)],
    )
    o = pl.pallas_call(
        kern,
        grid_spec=grid_spec,
        out_shape=jax.ShapeDtypeStruct((n, H_A, t, DV_A), F32),
        compiler_params=_cparams("parallel", "arbitrary"),
        name="diff_attn_paged",
    )(page_table, lq, lk, g1, qh, new_page(k_new), new_page(v_new), *([ck] * pages), *([cv] * pages))
    return o.transpose(0, 2, 1, 3).reshape(n * t, W_A).astype(BF16)


def _rwkv_prep_kernel(pb_ref, first_ref, mu_ref, wl_ref, w0_ref, a0_ref, kk_w_ref, ka_w_ref, rk_ref,
                      seg_ref, r_o, w_o, k_o, v_o, kk_o, b_o, bonus_o, g_o, *, period):
    pb = pb_ref[...]
    tm = pb.shape[0]
    row = lax.broadcasted_iota(jnp.int32, (tm, 1), 0)
    prev = jnp.where(row % period == 0, first_ref[...], pltpu.roll(pb, 1, 0))
    xb = pb + (prev - pb) * mu_ref[...]
    r = xb[:, :W_B]
    kb = xb[:, W_B:2 * W_B]
    vb = xb[:, 2 * W_B:3 * W_B]
    z = xb[:, 3 * W_B:]
    ll = lax.broadcasted_iota(jnp.int32, (1, LORA_COLS), 1)
    zz = jnp.where(ll < LORA_W, jnp.tanh(z), jnp.where(ll < LORA_W + LORA_A, z, jax.nn.sigmoid(z)))
    lo = _dot(zz.astype(BF16), wl_ref[...])
    decay = jnp.exp(-DECAY_SCALE * jax.nn.sigmoid(w0_ref[...] + lo[:, :W_B]))
    a = jax.nn.sigmoid(a0_ref[...] + lo[:, W_B:2 * W_B])
    seg = seg_ref[...]
    kk = kb * kk_w_ref[...]
    kk = kk / jnp.maximum(jnp.sqrt(_segsum(kk * kk, seg)), 1e-12)
    k_adj = kb * (1.0 + (a - 1.0) * ka_w_ref[...])
    r_o[...] = r
    w_o[...] = decay
    k_o[...] = k_adj
    v_o[...] = vb
    kk_o[...] = kk
    b_o[...] = kk * a
    bonus_o[...] = _segsum(r * k_adj * rk_ref[...], seg) * vb
    g_o[...] = lo[:, 2 * W_B:]


def _rwkv_prep_call(pb, first, p, j, seg, grp):
    tm, rows = grp.tm, grp.rows
    period = tm if grp.mod_rows == 1 else grp.t
    first_rows = first.shape[1]
    vec = lambda x: x[j].reshape(1, -1)
    wl = jnp.zeros((LORA_COLS, 3 * W_B), F32)
    wl = wl.at[:LORA_W, :W_B].set(p['w_up'][j])
    wl = wl.at[LORA_W:LORA_W + LORA_A, W_B:2 * W_B].set(p['a_up'][j])
    wl = wl.at[LORA_W + LORA_A:, 2 * W_B:].set(p['g_up'][j])
    outs = [jax.ShapeDtypeStruct((rows, W_B), F32)] * 8
    return pl.pallas_call(
        functools.partial(_rwkv_prep_kernel, period=period),
        grid=(grp.tiles,),
        in_specs=[_row_spec(tm, B_COLS),
                  pl.BlockSpec((None, first_rows, B_COLS), lambda i: (i, 0, 0)),
                  _full_spec((1, B_COLS)), _full_spec((LORA_COLS, 3 * W_B)),
                  _full_spec((1, W_B)), _full_spec((1, W_B)), _full_spec((1, W_B)),
                  _full_spec((1, W_B)), _full_spec((1, W_B)), _full_spec((W_B, W_B))],
        out_specs=[_row_spec(tm, W_B)] * 8,
        out_shape=outs,
        compiler_params=_cparams("parallel"),
        name="rwkv_prep",
    )(pb, first, vec(p['mu_b']), wl.astype(BF16), vec(p['w0']), vec(p['a0']), vec(p['k_k']),
      vec(p['k_a']), vec(p['r_k']), seg)


def _rwkv_scan_kernel(kk_ref, w_ref, b_ref, k_ref, r_ref, v_ref, s0_ref, y_ref, s_ref, *, tc):
    @pl.when(pl.program_id(1) == 0)
    def _():
        s_ref[...] = s0_ref[...]

    n_acc = 4

    def step(t0, j):
        row = pl.ds(t0 + j, 1)
        parts = [None] * n_acc
        for k in range(N_B):
            term = s_ref[k] * kk_ref[k, row, :]
            parts[k % n_acc] = term if parts[k % n_acc] is None else parts[k % n_acc] + term
        sa = (parts[0] + parts[1]) + (parts[2] + parts[3])
        v_t = v_ref[t0 + j]
        parts = [None] * n_acc
        for k in range(N_B):
            s_new = s_ref[k] * w_ref[k, row, :] - sa * b_ref[k, row, :] + v_t * k_ref[k, row, :]
            s_ref[k] = s_new
            term = s_new * r_ref[k, row, :]
            parts[k % n_acc] = term if parts[k % n_acc] is None else parts[k % n_acc] + term
        y_ref[t0 + j] = (parts[0] + parts[1]) + (parts[2] + parts[3])

    def steps(i, carry):
        t0 = pl.multiple_of(i * SUBLANES, SUBLANES)
        for j in range(SUBLANES):
            step(t0, j)
        return carry

    lax.fori_loop(0, tc // SUBLANES, steps, 0)


def _to_chain_layout(x, n, t):
    groups = n * H_B // CHAINS
    x = x.reshape(n, t, H_B, N_B).transpose(3, 1, 0, 2).reshape(N_B, t, groups, CHAINS)
    x = x.transpose(2, 0, 1, 3)
    return jnp.concatenate([x, x], axis=-1)


def _value_to_chain_layout(v, n, t):
    groups = n * H_B // CHAINS
    v = v.reshape(n, t, H_B, N_B).transpose(1, 3, 0, 2).reshape(t, 2, N_B // 2, groups, CHAINS)
    return v.transpose(3, 0, 2, 1, 4).reshape(groups, t, N_B // 2, 2 * CHAINS)


def _value_from_chain_layout(y, n, t):
    groups = n * H_B // CHAINS
    y = y.reshape(groups, t, N_B // 2, 2, CHAINS).transpose(1, 3, 2, 0, 4)
    y = y.reshape(t, N_B, n, H_B).transpose(2, 0, 3, 1)
    return y.reshape(n * t, W_B)


def _state_to_chain_layout(s, n):
    groups = n * H_B // CHAINS
    s = s.reshape(groups, CHAINS, 2, N_B // 2, N_B)
    return s.transpose(0, 4, 3, 2, 1).reshape(groups, N_B, N_B // 2, 2 * CHAINS)


def _state_from_chain_layout(s, n):
    groups = n * H_B // CHAINS
    s = s.reshape(groups, N_B, N_B // 2, 2, CHAINS).transpose(0, 4, 3, 2, 1)
    return s.reshape(n, H_B, N_B, N_B)


def _chain_in_kernel(x_ref, o_ref, z_scr, *, per_key, tt):
    n_seq = x_ref.shape[0]
    rows_out = N_B if per_key else N_B // 2
    for s in range(n_seq):
        z_scr[s] = x_ref[s].T
    for j in range(rows_out):
        rows = []
        for half in range(2):
            start = j if per_key else half * (N_B // 2) + j
            rows += [z_scr[s, pl.ds(start, H_B, stride=N_B), :] for s in range(n_seq)]
        tile = jnp.concatenate(rows, axis=0).T
        if per_key:
            o_ref[j] = tile
        else:
            o_ref[pl.ds(j, tt, stride=rows_out), :] = tile


def _chain_in_call(x, n, t, per_key):
    tt = LANES
    if per_key:
        out_spec = pl.BlockSpec((N_B, tt, 2 * CHAINS), lambda i: (0, i, 0))
        out_shape = jax.ShapeDtypeStruct((N_B, t, 2 * CHAINS), F32)
    else:
        out_spec = pl.BlockSpec((tt * (N_B // 2), 2 * CHAINS), lambda i: (i, 0))
        out_shape = jax.ShapeDtypeStruct((t * (N_B // 2), 2 * CHAINS), F32)
    out = pl.pallas_call(
        functools.partial(_chain_in_kernel, per_key=per_key, tt=tt),
        grid=(t // tt,),
        in_specs=[pl.BlockSpec((n, tt, W_B), lambda i: (0, i, 0))],
        out_specs=out_spec,
        out_shape=out_shape,
        scratch_shapes=[pltpu.VMEM((n, W_B, tt), F32)],
        compiler_params=_cparams("parallel"),
        name="chain_layout_in",
    )(x.reshape(n, t, W_B))
    return out[None] if per_key else out.reshape(1, t, N_B // 2, 2 * CHAINS)


def _chain_out_kernel(y_ref, o_ref, z_scr, *, tt):
    n_seq = o_ref.shape[0]
    rows = N_B // 2
    for j in range(rows):
        tile = y_ref[pl.ds(j, tt, stride=rows), :].T
        for half in range(2):
            for s in range(n_seq):
                r0 = (half * n_seq + s) * H_B
                z_scr[s, pl.ds(half * rows + j, H_B, stride=N_B), :] = tile[r0:r0 + H_B]
    for s in range(n_seq):
        o_ref[s] = z_scr[s].T


def _chain_out_call(y, n, t):
    tt = LANES
    rows = N_B // 2
    out = pl.pallas_call(
        functools.partial(_chain_out_kernel, tt=tt),
        grid=(t // tt,),
        in_specs=[pl.BlockSpec((tt * rows, 2 * CHAINS), lambda i: (i, 0))],
        out_specs=pl.BlockSpec((n, tt, W_B), lambda i: (0, i, 0)),
        out_shape=jax.ShapeDtypeStruct((n, t, W_B), F32),
        scratch_shapes=[pltpu.VMEM((n, W_B, tt), F32)],
        compiler_params=_cparams("parallel"),
        name="chain_layout_out",
    )(y.reshape(t * rows, 2 * CHAINS))
    return out.reshape(n * t, W_B)


def _rwkv_scan_call(kk, w, b, k, r, v, s0, n, t):
    groups = n * H_B // CHAINS
    tc = min(SCAN_TC, t)
    vec_spec = pl.BlockSpec((None, N_B, tc, 2 * CHAINS), lambda g, i: (g, 0, i, 0))
    val_spec = pl.BlockSpec((None, tc, N_B // 2, 2 * CHAINS), lambda g, i: (g, i, 0, 0))
    st_spec = pl.BlockSpec((None, N_B, N_B // 2, 2 * CHAINS), lambda g, i: (g, 0, 0, 0))
    in_kernel_layout = groups == 1 and t % LANES == 0
    if in_kernel_layout:
        vecs = [_chain_in_call(x, n, t, True) for x in (kk, w, b, k, r)]
        val = _chain_in_call(v, n, t, False)
    else:
        vecs = [_to_chain_layout(x, n, t) for x in (kk, w, b, k, r)]
        val = _value_to_chain_layout(v, n, t)
    y, s_fin = pl.pallas_call(
        functools.partial(_rwkv_scan_kernel, tc=tc),
        grid=(groups, t // tc),
        in_specs=[vec_spec] * 5 + [val_spec, st_spec],
        out_specs=[val_spec, st_spec],
        out_shape=[jax.ShapeDtypeStruct((groups, t, N_B // 2, 2 * CHAINS), F32),
                   jax.ShapeDtypeStruct((groups, N_B, N_B // 2, 2 * CHAINS), F32)],
        compiler_params=_cparams("parallel", "arbitrary"),
        name="rwkv_scan",
    )(*vecs, val, _state_to_chain_layout(s0, n))
    y = _chain_out_call(y, n, t) if in_kernel_layout else _value_from_chain_layout(y, n, t)
    return y, _state_from_chain_layout(s_fin, n)


def _rwkv_post_kernel(y_ref, bonus_ref, g_ref, lg_ref, lb_ref, seg_ref, o_ref):
    y = y_ref[...]
    seg = seg_ref[...]
    yc = y - _segsum(y, seg) * (1.0 / N_B)
    var = _segsum(yc * yc, seg) * (1.0 / N_B)
    yn = yc * lax.rsqrt(var + LNX_EPS) * lg_ref[...] + lb_ref[...]
    o_ref[...] = ((yn + bonus_ref[...]) * g_ref[...]).astype(o_ref.dtype)


def _rwkv_post_call(y, bonus, g, lg, lb, seg, grp):
    tm = grp.tm
    return pl.pallas_call(
        _rwkv_post_kernel,
        grid=(grp.tiles,),
        in_specs=[_row_spec(tm, W_B)] * 3 + [_full_spec((1, W_B)), _full_spec((1, W_B)),
                                              _full_spec((W_B, W_B))],
        out_specs=_row_spec(tm, W_B),
        out_shape=jax.ShapeDtypeStruct((grp.rows, W_B), BF16),
        compiler_params=_cparams("parallel"),
        name="rwkv_post",
    )(y, bonus, g, lg[None, :], lb[None, :], seg)


def _norm_mod(x, g, sc, sh):
    y = x * lax.rsqrt(jnp.mean(x * x, axis=-1, keepdims=True) + RMS_EPS) * g
    return y * (1.0 + sc) + sh


def _norm_specs(grp):
    return [_full_spec((1, D_MODEL)), _mod_spec(grp), _mod_spec(grp)]


def _residual_out(grp, hn_dtype):
    return dict(out_specs=[_row_spec(grp.tm, D_MODEL), _row_spec(grp.tm, D_MODEL)],
                out_shape=[jax.ShapeDtypeStruct((grp.rows, D_MODEL), F32),
                           jax.ShapeDtypeStruct((grp.rows, D_MODEL), hn_dtype)])


def _outproj_even_kernel(oa_ref, ob_ref, w_ref, x_ref, g_ref, ng_ref, nsc_ref, nsh_ref, o_ref, hn_ref):
    mix = _dot(oa_ref[...], w_ref[:W_A, :]) + _dot(ob_ref[...], w_ref[W_A:, :])
    x = x_ref[...] + g_ref[...] * mix
    o_ref[...] = x
    hn_ref[...] = _norm_mod(x, ng_ref[...], nsc_ref[...], nsh_ref[...]).astype(hn_ref.dtype)


def _outproj_even_call(oa, ob, w, x, gate, norm, grp):
    tm = grp.tm
    ng, nsc, nsh, hn_dtype = norm
    return pl.pallas_call(
        _outproj_even_kernel,
        grid=(grp.tiles,),
        in_specs=[_row_spec(tm, W_A), _row_spec(tm, W_B), _full_spec((W_A + W_B, D_MODEL)),
                  _row_spec(tm, D_MODEL), _mod_spec(grp)] + _norm_specs(grp),
        compiler_params=_cparams("parallel"),
        name="outproj_even",
        **_residual_out(grp, hn_dtype),
    )(oa, ob, w, x, gate, ng[None, :], nsc, nsh)


def _swiglu_up_kernel(a_ref, wg_ref, wu_ref, o_ref):
    a = a_ref[...]
    o_ref[...] = (_silu(_dot(a, wg_ref[...])) * _dot(a, wu_ref[...])).astype(o_ref.dtype)


def _swiglu_up_call(hn, wg, wu, grp):
    tm = grp.tm
    tn = D_FF // 2
    return pl.pallas_call(
        _swiglu_up_kernel,
        grid=(D_FF // tn, grp.tiles),
        in_specs=[pl.BlockSpec((tm, D_MODEL), lambda c, i: (i, 0)),
                  pl.BlockSpec((D_MODEL, tn), lambda c, i: (0, c)),
                  pl.BlockSpec((D_MODEL, tn), lambda c, i: (0, c))],
        out_specs=pl.BlockSpec((tm, tn), lambda c, i: (i, c)),
        out_shape=jax.ShapeDtypeStruct((grp.rows, D_FF), BF16),
        compiler_params=_cparams("parallel", "parallel"),
        name="swiglu_up",
    )(hn, wg, wu)


def _down_res_kernel(h_ref, w_ref, x_ref, g_ref, ng_ref, nsc_ref, nsh_ref, o_ref, hn_ref):
    x = x_ref[...] + g_ref[...] * _dot(h_ref[...], w_ref[...])
    o_ref[...] = x
    hn_ref[...] = _norm_mod(x, ng_ref[...], nsc_ref[...], nsh_ref[...]).astype(hn_ref.dtype)


def _down_res_call(h, w, x, gate, norm, grp):
    tm = grp.tm
    kdim = h.shape[1]
    ng, nsc, nsh, hn_dtype = norm
    return pl.pallas_call(
        _down_res_kernel,
        grid=(grp.tiles,),
        in_specs=[_row_spec(tm, kdim), _full_spec((kdim, D_MODEL)), _row_spec(tm, D_MODEL), _mod_spec(grp)]
                 + _norm_specs(grp),
        compiler_params=_cparams("parallel"),
        name="down_res",
        **_residual_out(grp, hn_dtype),
    )(h, w, x, gate, ng[None, :], nsc, nsh)


def _gelu(x):
    return 0.5 * x * (1.0 + lax.erf(x * INV_SQRT2))


def _gmlp_u_kernel(a_ref, w_ref, o_ref):
    o_ref[...] = _gelu(_dot(a_ref[...], w_ref[...]))


def _gmlp_v_kernel(a_ref, w_ref, lg_ref, lb_ref, vb_ref, vf_ref):
    v = _gelu(_dot(a_ref[...], w_ref[...]))
    vc = v - jnp.mean(v, axis=-1, keepdims=True)
    vn = vc * lax.rsqrt(jnp.mean(vc * vc, axis=-1, keepdims=True) + LN_EPS) * lg_ref[...] + lb_ref[...]
    vb_ref[...] = vn.astype(BF16)
    vf_ref[...] = vn


def _gmlp_in_call(hn, w, lg, lb, grp):
    tm, rows = grp.tm, grp.rows
    wspec = lambda half: pl.BlockSpec((D_MODEL, E_C), lambda i: (0, half))
    u = pl.pallas_call(
        _gmlp_u_kernel,
        grid=(grp.tiles,),
        in_specs=[_row_spec(tm, D_MODEL), wspec(0)],
        out_specs=_row_spec(tm, E_C),
        out_shape=jax.ShapeDtypeStruct((rows, E_C), F32),
        compiler_params=_cparams("parallel"),
        name="gmlp_u",
    )(hn, w)
    vb, vf = pl.pallas_call(
        _gmlp_v_kernel,
        grid=(grp.tiles,),
        in_specs=[_row_spec(tm, D_MODEL), wspec(1), _full_spec((1, E_C)), _full_spec((1, E_C))],
        out_specs=[_row_spec(tm, E_C), _row_spec(tm, E_C)],
        out_shape=[jax.ShapeDtypeStruct((rows, E_C), BF16), jax.ShapeDtypeStruct((rows, E_C), F32)],
        compiler_params=_cparams("parallel"),
        name="gmlp_v",
    )(hn, w, lg[None, :], lb[None, :])
    return u, vb, vf


def _gmlp_out_kernel(u_ref, v_ref, ws_ref, bs_ref, wo_ref, x_ref, g_ref, o_ref, z_scr, *, lm):
    tm = u_ref.shape[0]
    for c in range(tm // lm):
        r0 = c * lm
        for gi in range(G_C):
            c0 = gi * CG
            mixed = _dot(ws_ref[gi], v_ref[r0:r0 + lm, c0:c0 + CG]) + bs_ref[:, gi:gi + 1]
            z_scr[r0:r0 + lm, c0:c0 + CG] = (u_ref[r0:r0 + lm, c0:c0 + CG] * mixed).astype(BF16)
    o_ref[...] = x_ref[...] + g_ref[...] * _dot(z_scr[...], wo_ref[...])


def _gmlp_out_call(u, vb, ws, bs, wo, x, gate, grp):
    tm = grp.tm
    lm = ws.shape[1]
    return pl.pallas_call(
        functools.partial(_gmlp_out_kernel, lm=lm),
        grid=(grp.tiles,),
        in_specs=[_row_spec(tm, E_C), _row_spec(tm, E_C), _full_spec((G_C, lm, lm)), _full_spec((lm, G_C)),
                  _full_spec((E_C, D_MODEL)), _row_spec(tm, D_MODEL), _mod_spec(grp)],
        out_specs=_row_spec(tm, D_MODEL),
        out_shape=jax.ShapeDtypeStruct((grp.rows, D_MODEL), F32),
        scratch_shapes=[pltpu.VMEM((tm, E_C), BF16)],
        compiler_params=_cparams("parallel"),
        name="gmlp_out",
    )(u, vb, ws, bs, wo, x, gate)


def _spatial_weights(w_s, b_s, grp):
    length = min(grp.t, CHUNK)
    ws = jnp.tril(w_s[:, :length, :length])
    bs = b_s[:, :length]
    if length < CHUNK:
        reps = grp.tm // length
        ws = jnp.einsum('ab,gij->gaibj', jnp.eye(reps, dtype=ws.dtype), ws).reshape(
            G_C, reps * length, reps * length)
        bs = jnp.tile(bs, (1, reps))
    return ws.astype(BF16), bs.T


def _moe_kernel(a_ref, logit_ref, tri_ref, wg_ref, wu_ref, wd_ref, x_ref, g_ref, ng_ref, nsc_ref, nsh_ref,
                o_ref, hn_ref, gate_scr, rank_scr, gate_t_scr, rank_t_scr, acc_scr, *, tc):
    e = pl.program_id(1)
    lane = lax.broadcasted_iota(jnp.int32, (1, LANES), 1)

    @pl.when(e == 0)
    def _():
        logits = logit_ref[...]
        m1 = jnp.max(logits, axis=1, keepdims=True)
        i1 = jnp.min(jnp.where(logits == m1, lane, LANES), axis=1, keepdims=True)
        rest = jnp.where(lane == i1, -jnp.inf, logits)
        m2 = jnp.max(rest, axis=1, keepdims=True)
        i2 = jnp.min(jnp.where(rest == m2, lane, LANES), axis=1, keepdims=True)
        e2 = jnp.exp(m2 - m1)
        gate = jnp.where(lane == i1, 1.0 / (1.0 + e2), jnp.where(lane == i2, e2 / (1.0 + e2), 0.0))
        picked = gate != 0.0
        rank = _dot(tri_ref[...], jnp.where(picked, 1.0, 0.0).astype(BF16))
        rank = jnp.where(picked, rank, -1.0)
        gate_scr[...] = gate
        rank_scr[...] = rank
        gate_t_scr[...] = gate.T
        rank_t_scr[...] = rank.T
        acc_scr[...] = jnp.zeros(acc_scr.shape, F32)

    gate_col = jnp.sum(jnp.where(lane == e, gate_scr[...], 0.0), axis=1, keepdims=True)
    rank_col = jnp.sum(jnp.where(lane == e, rank_scr[...], 0.0), axis=1, keepdims=True)
    gate_row = gate_t_scr[pl.ds(e, 1), :]
    rank_row = rank_t_scr[pl.ds(e, 1), :]
    count = jnp.sum(jnp.where(rank_row >= 0.0, 1, 0))
    slot_col = lax.broadcasted_iota(jnp.int32, (tc, 1), 0).astype(F32)
    slot_row = lax.broadcasted_iota(jnp.int32, (1, tc), 1).astype(F32)

    def chunk(c, carry):
        base = (c * tc).astype(F32)
        take = rank_row == base + slot_col
        xg = _dot(jnp.where(take, 1.0, 0.0).astype(BF16), a_ref[...]).astype(BF16)
        h = (_silu(_dot(xg, wg_ref[...])) * _dot(xg, wu_ref[...])).astype(BF16)
        y = _dot(h, wd_ref[...])
        gate_c = jnp.sum(jnp.where(take, gate_row, 0.0), axis=1, keepdims=True)
        put = jnp.where(rank_col == base + slot_row, 1.0, 0.0).astype(BF16)
        acc_scr[...] += _dot(put, (gate_c * y).astype(BF16))
        return carry

    lax.fori_loop(0, (count + tc - 1) // tc, chunk, 0)

    @pl.when(e == N_EXPERTS - 1)
    def _():
        x = x_ref[...] + g_ref[...] * acc_scr[...]
        o_ref[...] = x
        hn_ref[...] = _norm_mod(x, ng_ref[...], nsc_ref[...], nsh_ref[...]).astype(hn_ref.dtype)


def _moe_call(hn, logits, wg, wu, wd, j, x, gate, norm, grp):
    tm = grp.tm
    tps = grp.tiles_per_seq
    tc = min(MOE_CHUNK, tm)
    ng, nsc, nsh, hn_dtype = norm
    mod = pl.BlockSpec((None, grp.mod_rows, D_MODEL), lambda i, e: (i // tps, 0, 0))
    tri = jnp.tril(jnp.ones((tm, tm), F32), -1).astype(BF16)
    row = lambda cols: pl.BlockSpec((tm, cols), lambda i, e: (i, 0))
    return pl.pallas_call(
        functools.partial(_moe_kernel, tc=tc),
        grid=(grp.tiles, N_EXPERTS),
        in_specs=[row(D_MODEL), row(LANES), pl.BlockSpec((tm, tm), lambda i, e: (0, 0)),
                  pl.BlockSpec((None, None, D_MODEL, D_FF_E), lambda i, e: (j, e, 0, 0)),
                  pl.BlockSpec((None, None, D_MODEL, D_FF_E), lambda i, e: (j, e, 0, 0)),
                  pl.BlockSpec((None, None, D_FF_E, D_MODEL), lambda i, e: (j, e, 0, 0)),
                  row(D_MODEL), mod,
                  pl.BlockSpec((1, D_MODEL), lambda i, e: (0, 0)), mod, mod],
        out_specs=[row(D_MODEL), row(D_MODEL)],
        out_shape=[jax.ShapeDtypeStruct((grp.rows, D_MODEL), F32),
                   jax.ShapeDtypeStruct((grp.rows, D_MODEL), hn_dtype)],
        scratch_shapes=[pltpu.VMEM((tm, LANES), F32), pltpu.VMEM((tm, LANES), F32),
                        pltpu.VMEM((LANES, tm), F32), pltpu.VMEM((LANES, tm), F32),
                        pltpu.VMEM((tm, D_MODEL), F32)],
        compiler_params=_cparams("parallel", "arbitrary"),
        name="moe_experts",
    )(hn, logits, tri, wg, wu, wd, x, gate, ng[None, :], nsc, nsh)


def _forward(x, mods, p, wb, grp, cache):
    n, t = grp.n, grp.t
    seg = jnp.kron(jnp.eye(H_B, dtype=F32), jnp.ones((N_B, N_B), F32)).astype(BF16)
    slopes = jnp.exp2(-8.0 * jnp.arange(1, H_A + 1, dtype=F32) / H_A)
    new_k, new_v, new_wkv, new_shift, new_cv = [], [], [], [], []
    mods = [[_expand_mod(m, grp) for m in layer] for layer in mods]

    def next_norm(i):
        if i + 1 < DEPTH:
            sh, sc = mods[i + 1][0], mods[i + 1][1]
            return p['norm1_g'][i + 1], sc, sh, BF16
        zero = jnp.zeros_like(mods[i][0])
        return p['final_g'], zero, zero, F32

    hn = _norm_mod_call(x, p['norm1_g'][0], mods[0][1], mods[0][0], grp)
    for i in range(DEPTH):
        j = i // 2
        sh1, sc1, g1, sh2, sc2, g2 = mods[i]
        if i % 2 == 0:
            q, k, v, kb, vb, pb = _inproj_even_call(hn, wb['w_in_e'][j], grp)
            lam_init = 0.8 - 0.6 * math.exp(-0.3 * i)
            lq = jnp.stack([p['lam_q1'][j], p['lam_q2'][j]])
            lk = jnp.stack([p['lam_k1'][j], p['lam_k2'][j]])
            if cache is None:
                g2h = jnp.tile(p['subln_g'][j], 2)[None, :]
                oa = _attn_prompt_call(q, kb, vb, slopes, lq, lk, g2h, n, t, lam_init)
                wkv0 = jnp.zeros((n, H_B, N_B, N_B), F32)
                last = pb.reshape(grp.tiles, grp.tm, B_COLS)[:, -1, :]
                first = jnp.concatenate([jnp.zeros((1, B_COLS), F32), last[:-1]], axis=0)
                starts = (jnp.arange(grp.tiles) % grp.tiles_per_seq == 0)[:, None]
                first = jnp.where(starts, 0.0, first)[:, None, :]
            else:
                cache_k, cache_v, page_table, state_wkv, state_shift = cache
                oa = _attn_sample_call(q, kb, vb, cache_k, cache_v, page_table, j, lq, lk,
                                       p['subln_g'][j][None, :], n, t, lam_init)
                wkv0 = state_wkv[j]
                first = jnp.repeat(state_shift[j], t, axis=0)[None]
            r, w, kv, vv, kk, b, bonus, g = _rwkv_prep_call(pb, first, p, j, seg, grp)
            y, s_fin = _rwkv_scan_call(kk, w, b, kv, r, vv, wkv0, n, t)
            ob = _rwkv_post_call(y, bonus, g, p['lnx_g'][j], p['lnx_b'][j], seg, grp)
            new_k.append(k.reshape(n, t, H_A, 2 * DK_A))
            new_v.append(v.reshape(n, t, H_A, DV_A))
            new_wkv.append(s_fin)
            new_shift.append(pb.reshape(n, t, B_COLS)[:, -1])
            x, hn = _outproj_even_call(oa.reshape(grp.rows, W_A), ob, wb['w_out_e'][j], x, g1,
                                       (p['norm2_g'][i], sc2, sh2, BF16), grp)
            h = _swiglu_up_call(hn, wb['ffn_gate'][j], wb['ffn_up'][j], grp)
            x, hn = _down_res_call(h, wb['ffn_down'][j], x, g2, next_norm(i), grp)
        else:
            u, vb, vf = _gmlp_in_call(hn, wb['w_in_o'][j], p['lnv_g'][j], p['lnv_b'][j], grp)
            new_cv.append(vf.reshape(n, t, E_C))
            ws, bs = _spatial_weights(p['w_s'][j], p['b_s'][j], grp)
            x = _gmlp_out_call(u, vb, ws, bs, wb['w_out_o'][j], x, g1, grp)
            hn, logits = _norm_mod_router_call(x, p['norm2_g'][i], sc2, sh2, p['router_w'][j],
                                               p['router_b'][j], grp)
            moe_grp = grp if grp.mod_rows > 1 else _Group(n, t, min(MOE_TILE, t))
            x, hn = _moe_call(hn, logits, wb['exp_gate'], wb['exp_up'], wb['exp_down'], j, x, g2,
                              next_norm(i), moe_grp)
    return (hn.reshape(n, t, D_MODEL), jnp.stack(new_k), jnp.stack(new_v), jnp.stack(new_wkv),
            jnp.stack(new_shift), jnp.stack(new_cv))


_BF16_WEIGHTS = ('w_in_e', 'w_out_e', 'ffn_gate', 'ffn_up', 'ffn_down', 'w_in_o', 'w_out_o',
                 'exp_gate', 'exp_up', 'exp_down')


def _run(x_prompt, x_sample, cache_k, cache_v, state_wkv, state_shift, page_table, c_prompt, c_sample, p):
    n_p, t_p, _ = x_prompt.shape
    n_s, t_s, _ = x_sample.shape
    grp_p = _Group(n_p, t_p, min(ROW_TILE, t_p))
    grp_s = _Group(n_s, t_s, n_s * t_s)
    wb = {name: p[name].astype(BF16) for name in _BF16_WEIGHTS}
    mod = _mod_call(jnp.concatenate([c_prompt, c_sample], axis=0), p['w_mod'], p['b_mod'])
    mods_p = [jnp.split(mod[i, :n_p], 6, axis=-1) for i in range(DEPTH)]
    mods_s = [jnp.split(mod[i, n_p:], 6, axis=-1) for i in range(DEPTH)]
    y_p, k_p, v_p, wkv_p, sh_p, _ = _forward(x_prompt.reshape(n_p * t_p, D_MODEL), mods_p, p, wb, grp_p, None)
    y_s, k_s, v_s, wkv_s, sh_s, cv_s = _forward(
        x_sample.reshape(n_s * t_s, D_MODEL), mods_s, p, wb, grp_s,
        (cache_k, cache_v, page_table, state_wkv, state_shift))
    return (y_p, y_s, k_p, v_p, wkv_p, sh_p, k_s, v_s, wkv_s, sh_s, cv_s)


def kernel(x_prompt, x_sample, cache_k, cache_v, state_wkv, state_shift, page_table, c_prompt, c_sample,
           w_mod, b_mod, norm1_g, norm2_g, final_g, w_in_e, w_out_e, lam_q1, lam_k1, lam_q2, lam_k2,
           subln_g, mu_b, w0, w_up, a0, a_up, g_up, k_k, k_a, r_k, lnx_g, lnx_b, w_in_o, lnv_g, lnv_b,
           w_s, b_s, w_out_o, ffn_gate, ffn_up, ffn_down, router_w, router_b, exp_gate, exp_up, exp_down):
    p = dict(w_mod=w_mod, b_mod=b_mod, norm1_g=norm1_g, norm2_g=norm2_g, final_g=final_g,
             w_in_e=w_in_e, w_out_e=w_out_e, lam_q1=lam_q1, lam_k1=lam_k1, lam_q2=lam_q2,
             lam_k2=lam_k2, subln_g=subln_g, mu_b=mu_b, w0=w0, w_up=w_up, a0=a0, a_up=a_up,
             g_up=g_up, k_k=k_k, k_a=k_a, r_k=r_k, lnx_g=lnx_g, lnx_b=lnx_b, w_in_o=w_in_o,
             lnv_g=lnv_g, lnv_b=lnv_b, w_s=w_s, b_s=b_s, w_out_o=w_out_o, ffn_gate=ffn_gate,
             ffn_up=ffn_up, ffn_down=ffn_down, router_w=router_w, router_b=router_b,
             exp_gate=exp_gate, exp_up=exp_up, exp_down=exp_down)
    return _run(x_prompt, x_sample, cache_k, cache_v, state_wkv, state_shift, page_table,
                c_prompt, c_sample, p)
```

```python
import functools
import math
from typing import NamedTuple

import jax
import jax.numpy as jnp
from jax import lax
from jax.experimental import pallas as pl
from jax.experimental.pallas import tpu as pltpu

F32 = jnp.float32
BF16 = jnp.bfloat16

D_MODEL = 1024
DEPTH = 4
H_A, DK_A, DV_A = 8, 32, 64
QK_A = H_A * 2 * DK_A
W_A = H_A * DV_A
A_COLS = 2 * QK_A + W_A
H_B, N_B = 8, 64
W_B = H_B * N_B
LORA_W, LORA_A, LORA_G = 64, 64, 128
LORA_COLS = LORA_W + LORA_A + LORA_G
B_COLS = 3 * W_B + LORA_COLS
IN_COLS_EVEN = A_COLS + B_COLS
CHUNK, G_C = 128, 8
E_C = 2 * D_MODEL
CG = E_C // G_C
D_FF = 2816
N_EXPERTS, D_FF_E = 8, 1408
PAGE_SIZE = 128
RMS_EPS, LN_EPS, LNX_EPS = 1e-6, 1e-5, 64e-5
NEG_INF = -1e30
ATTN_SCALE = DK_A ** -0.5
DECAY_SCALE = math.exp(-0.5)
INV_SQRT2 = 2.0 ** -0.5

LANES = 128
SUBLANES = 8
MIB = 1 << 20
VMEM_LIMIT = 56 * MIB

ROW_TILE = 512
ATTN_TQ, ATTN_TK = 256, 256
PAGES_PER_STEP = 16
MOE_TILE = 512
MOE_CHUNK = 160
SCAN_TC = 32
CHAINS = 64


class _Group(NamedTuple):
    n: int
    t: int
    tm: int

    @property
    def rows(self):
        return self.n * self.t

    @property
    def tiles(self):
        return self.rows // self.tm

    @property
    def tiles_per_seq(self):
        return max(self.t // self.tm, 1)

    @property
    def mod_rows(self):
        return 1 if self.t >= self.tm else self.tm


def _cparams(*sem):
    return pltpu.CompilerParams(dimension_semantics=sem, vmem_limit_bytes=VMEM_LIMIT)


def _row_spec(tm, cols):
    return pl.BlockSpec((tm, cols), lambda i: (i, 0))


def _full_spec(shape):
    zeros = (0,) * len(shape)
    return pl.BlockSpec(shape, lambda *_: zeros)


def _mod_spec(grp):
    tps = grp.tiles_per_seq
    return pl.BlockSpec((None, grp.mod_rows, D_MODEL), lambda i: (i // tps, 0, 0))


def _expand_mod(m, grp):
    if grp.mod_rows == 1:
        return m[:, None, :]
    return jnp.repeat(m, grp.t, axis=0)[None]


def _dot(a, b):
    return jnp.dot(a, b, preferred_element_type=F32)


def _dot_nt(a, b):
    return lax.dot_general(a, b, (((1,), (1,)), ((), ())), preferred_element_type=F32)


def _split_bf16(x):
    hi = x.astype(BF16)
    lo = (x - hi.astype(F32)).astype(BF16)
    return hi, lo


def _segsum(x, seg):
    hi, lo = _split_bf16(x)
    return _dot(hi, seg) + _dot(lo, seg)


def _silu(x):
    return x * jax.nn.sigmoid(x)


def _mod_kernel(c_ref, w_ref, b_ref, o_ref):
    a_hi, a_lo = _split_bf16(_silu(c_ref[...]))
    w_hi, w_lo = _split_bf16(w_ref[...])
    o_ref[...] = _dot(a_hi, w_hi) + _dot(a_hi, w_lo) + _dot(a_lo, w_hi) + b_ref[...]


def _mod_call(c_all, w_mod, b_mod):
    rows = c_all.shape[0]
    tn = D_MODEL
    return pl.pallas_call(
        _mod_kernel,
        grid=(DEPTH, 6 * D_MODEL // tn),
        in_specs=[
            pl.BlockSpec((rows, D_MODEL), lambda i, j: (0, 0)),
            pl.BlockSpec((None, D_MODEL, tn), lambda i, j: (i, 0, j)),
            pl.BlockSpec((None, 1, tn), lambda i, j: (i, 0, j)),
        ],
        out_specs=pl.BlockSpec((None, rows, tn), lambda i, j: (i, 0, j)),
        out_shape=jax.ShapeDtypeStruct((DEPTH, rows, 6 * D_MODEL), F32),
        compiler_params=_cparams("parallel", "parallel"),
        name="adaln_mod",
    )(c_all, w_mod, b_mod[:, None, :])


def _norm_mod_kernel(x_ref, g_ref, sc_ref, sh_ref, o_ref):
    x = x_ref[...]
    y = x * lax.rsqrt(jnp.mean(x * x, axis=-1, keepdims=True) + RMS_EPS) * g_ref[...]
    o_ref[...] = (y * (1.0 + sc_ref[...]) + sh_ref[...]).astype(o_ref.dtype)


def _norm_mod_call(x, g, sc, sh, grp):
    return pl.pallas_call(
        _norm_mod_kernel,
        grid=(grp.tiles,),
        in_specs=[_row_spec(grp.tm, D_MODEL), _full_spec((1, D_MODEL)), _mod_spec(grp), _mod_spec(grp)],
        out_specs=_row_spec(grp.tm, D_MODEL),
        out_shape=jax.ShapeDtypeStruct((grp.rows, D_MODEL), BF16),
        compiler_params=_cparams("parallel"),
        name="norm_mod",
    )(x, g[None, :], sc, sh)


def _norm_mod_router_kernel(x_ref, g_ref, sc_ref, sh_ref, wr_ref, br_ref, o_ref, logit_ref):
    x = x_ref[...]
    y = x * lax.rsqrt(jnp.mean(x * x, axis=-1, keepdims=True) + RMS_EPS) * g_ref[...]
    hn = y * (1.0 + sc_ref[...]) + sh_ref[...]
    o_ref[...] = hn.astype(o_ref.dtype)
    h_hi, h_lo = _split_bf16(hn)
    w_hi, w_lo = _split_bf16(wr_ref[...])
    logit_ref[...] = _dot(h_hi, w_hi) + _dot(h_hi, w_lo) + _dot(h_lo, w_hi) + br_ref[...]


def _norm_mod_router_call(x, g, sc, sh, wr, br, grp):
    wr_pad = jnp.zeros((D_MODEL, LANES), F32).at[:, :N_EXPERTS].set(wr)
    br_pad = jnp.full((1, LANES), -jnp.inf, F32).at[0, :N_EXPERTS].set(br)
    return pl.pallas_call(
        _norm_mod_router_kernel,
        grid=(grp.tiles,),
        in_specs=[_row_spec(grp.tm, D_MODEL), _full_spec((1, D_MODEL)), _mod_spec(grp), _mod_spec(grp),
                  _full_spec((D_MODEL, LANES)), _full_spec((1, LANES))],
        out_specs=[_row_spec(grp.tm, D_MODEL), _row_spec(grp.tm, LANES)],
        out_shape=[jax.ShapeDtypeStruct((grp.rows, D_MODEL), BF16),
                   jax.ShapeDtypeStruct((grp.rows, LANES), F32)],
        compiler_params=_cparams("parallel"),
        name="norm_mod_router",
    )(x, g[None, :], sc, sh, wr_pad, br_pad)


def _inproj_even_kernel(a_ref, w_ref, q_ref, k_ref, v_ref, kb_ref, vb_ref, pb_ref):
    a = a_ref[...]
    q_ref[...] = (_dot(a, w_ref[:, :QK_A]) * ATTN_SCALE).astype(BF16)
    k = _dot(a, w_ref[:, QK_A:2 * QK_A])
    k_ref[...] = k
    kb_ref[...] = k.astype(BF16)
    v = _dot(a, w_ref[:, 2 * QK_A:A_COLS])
    v_ref[...] = v
    vb_ref[...] = v.astype(BF16)
    half = B_COLS // 2
    pb_ref[:, :half] = _dot(a, w_ref[:, A_COLS:A_COLS + half])
    pb_ref[:, half:] = _dot(a, w_ref[:, A_COLS + half:])


def _inproj_even_call(hn, w, grp):
    tm, rows = grp.tm, grp.rows
    sds = jax.ShapeDtypeStruct
    return pl.pallas_call(
        _inproj_even_kernel,
        grid=(grp.tiles,),
        in_specs=[_row_spec(tm, D_MODEL), _full_spec((D_MODEL, IN_COLS_EVEN))],
        out_specs=[_row_spec(tm, QK_A), _row_spec(tm, QK_A), _row_spec(tm, W_A),
                   _row_spec(tm, QK_A), _row_spec(tm, W_A), _row_spec(tm, B_COLS)],
        out_shape=[sds((rows, QK_A), BF16), sds((rows, QK_A), F32), sds((rows, W_A), F32),
                   sds((rows, QK_A), BF16), sds((rows, W_A), BF16), sds((rows, B_COLS), F32)],
        compiler_params=_cparams("parallel"),
        name="inproj_even",
    )(hn, w)


def _lambda(lq_ref, lk_ref, lam_init):
    t = jnp.sum(lq_ref[...] * lk_ref[...], axis=1, keepdims=True)
    e = jnp.exp(t)
    return e[0:1] - e[1:2] + lam_init


def _attn_prompt_kernel(lq_ref, lk_ref, g_ref, qf_ref, q_ref, k_ref, kf_ref, vt_ref, o_ref,
                        m_scr, l_scr, acc_scr, *, tq, tk, lam_init):
    hp = pl.program_id(1)
    qi = pl.program_id(2)
    lane = lax.broadcasted_iota(jnp.int32, (1, LANES), 1)
    q = q_ref[...]
    q_aug = []
    for i in range(4):
        q_map = jnp.where(lane // DK_A == i, q, jnp.zeros_like(q))
        feat = jnp.broadcast_to(qf_ref[pl.ds(2 * hp + i // 2, 1), :], (tq, LANES)).astype(BF16)
        q_aug.append(jnp.concatenate([q_map, feat], axis=1))
    m_scr[...] = jnp.full(m_scr.shape, NEG_INF, F32)
    l_scr[...] = jnp.zeros(l_scr.shape, F32)
    acc_scr[...] = jnp.zeros(acc_scr.shape, F32)
    row = lax.broadcasted_iota(jnp.int32, (tk, 1), 0)
    col = lax.broadcasted_iota(jnp.int32, (1, tq), 1)
    q0 = qi * tq

    def scores(kt):
        k0 = pl.multiple_of(kt * tk, tk)
        k_aug = jnp.concatenate([k_ref[pl.ds(k0, tk), :], kf_ref[pl.ds(k0, tk), :]], axis=1)
        return tuple(_dot_nt(k_aug, q_aug[i]) for i in range(4))

    def softmax_pv(kt, s_maps, masked):
        vt = vt_ref[kt]
        if masked:
            keep = (kt * tk + row) <= (q0 + col)
        for i in range(4):
            h = i // 2
            s = jnp.where(keep, s_maps[i], NEG_INF) if masked else s_maps[i]
            m_old = m_scr[i]
            m_new = jnp.maximum(m_old, jnp.max(s, axis=0, keepdims=True))
            alpha = jnp.exp(m_old - m_new)
            p = jnp.exp(s - m_new)
            l_scr[i] = alpha * l_scr[i] + jnp.sum(p, axis=0, keepdims=True)
            acc_scr[i] = alpha * acc_scr[i] + _dot(vt[h * DV_A:(h + 1) * DV_A, :], p.astype(BF16))
            m_scr[i] = m_new

    n_full = q0 // tk

    def body(kt, s_maps):
        s_next = scores(kt + 1)
        softmax_pv(kt, s_maps, False)
        return s_next

    s_last = lax.fori_loop(0, n_full, body, scores(0))
    softmax_pv(n_full, s_last, True)

    lam = _lambda(lq_ref, lk_ref, lam_init)
    heads = []
    for h in range(2):
        oh = acc_scr[2 * h] / l_scr[2 * h] - lam * (acc_scr[2 * h + 1] / l_scr[2 * h + 1])
        ms = jnp.mean(oh * oh, axis=0, keepdims=True)
        heads.append(oh * lax.rsqrt(ms + RMS_EPS))
    o = jnp.concatenate(heads, axis=0).T
    o_ref[...] = (o * g_ref[...] * (1.0 - lam_init)).astype(o_ref.dtype)


def _attn_prompt_call(q, k, v, slopes, lq, lk, g2, n, t, lam_init):
    tq, tk = ATTN_TQ, min(ATTN_TK, t)
    tq = min(tq, tk)
    kern = functools.partial(_attn_prompt_kernel, tq=tq, tk=tk, lam_init=lam_init)
    pos = jnp.arange(t)
    k_feat = jnp.zeros((t, LANES), F32).at[:, 0].set(pos // 256).at[:, 1].set(pos % 256).astype(BF16)
    q_feat = jnp.zeros((H_A, LANES), F32).at[:, 0].set(256.0 * slopes).at[:, 1].set(slopes)
    v_t = v.reshape(n, t // tk, tk, W_A).transpose(0, 1, 3, 2)
    qspec = pl.BlockSpec((None, tq, LANES), lambda b, hp, qi: (b, qi, hp))
    return pl.pallas_call(
        kern,
        grid=(n, H_A // 2, t // tq),
        in_specs=[_full_spec((2, DK_A)), _full_spec((2, DK_A)), _full_spec((1, LANES)),
                  _full_spec((H_A, LANES)), qspec,
                  pl.BlockSpec((None, t, LANES), lambda b, hp, qi: (b, 0, hp)),
                  _full_spec((t, LANES)),
                  pl.BlockSpec((None, t // tk, LANES, tk), lambda b, hp, qi: (b, 0, hp, 0))],
        out_specs=qspec,
        out_shape=jax.ShapeDtypeStruct((n, t, W_A), BF16),
        scratch_shapes=[pltpu.VMEM((4, 1, tq), F32), pltpu.VMEM((4, 1, tq), F32),
                        pltpu.VMEM((4, DV_A, tq), F32)],
        compiler_params=_cparams("parallel", "parallel", "parallel"),
        name="diff_attn_prompt",
    )(lq, lk, g2, q_feat, q.reshape(n, t, QK_A), k.reshape(n, t, QK_A), k_feat, v_t)


def _attn_sample_kernel(pt_ref, lq_ref, lk_ref, g_ref, q_ref, kn_ref, vn_ref, *rest,
                        pages, steps, past_len, t_new, lam_init):
    k_pages, v_pages = rest[:pages], rest[pages:2 * pages]
    o_ref, m_scr, l_scr, acc_scr = rest[2 * pages:]
    step = pl.program_id(1)
    hr = 2 * t_new
    row = lax.broadcasted_iota(jnp.int32, (H_A * hr, 1), 0)
    slope = jnp.exp2(-(row // hr + 1).astype(F32))

    @pl.when(step == 0)
    def _():
        m_scr[...] = jnp.full(m_scr.shape, NEG_INF, F32)
        l_scr[...] = jnp.zeros(l_scr.shape, F32)
        acc_scr[...] = jnp.zeros(acc_scr.shape, F32)

    def update(scores, values_t, keep):
        s = jnp.concatenate(scores, axis=0)
        if keep is not None:
            s = jnp.where(keep, s, NEG_INF)
        m_old = m_scr[...]
        m_new = jnp.maximum(m_old, jnp.max(s, axis=1, keepdims=True))
        alpha = jnp.exp(m_old - m_new)
        p = jnp.exp(s - m_new)
        l_scr[...] = alpha * l_scr[...] + jnp.sum(p, axis=1, keepdims=True)
        p = p.astype(BF16)
        pv = jnp.concatenate([_dot_nt(p[h * hr:(h + 1) * hr], values_t[h]) for h in range(H_A)], axis=0)
        acc_scr[...] = alpha * acc_scr[...] + pv
        m_scr[...] = m_new

    col = lax.broadcasted_iota(jnp.int32, (1, pages * PAGE_SIZE), 1)
    k_rel = (step * (pages * PAGE_SIZE) - past_len + col).astype(F32)
    scores, values_t = [], []
    for h in range(H_A):
        k_t = jnp.concatenate([kp[h] for kp in k_pages], axis=1).astype(BF16)
        values_t.append(jnp.concatenate([vp[h] for vp in v_pages], axis=1).astype(BF16))
        scores.append(_dot(q_ref[h], k_t))
    update([s + slope[h * hr:(h + 1) * hr] * k_rel for h, s in enumerate(scores)], values_t, None)

    @pl.when(step == steps - 1)
    def _():
        new_col = lax.broadcasted_iota(jnp.int32, (1, PAGE_SIZE), 1)
        keep = new_col <= row % t_new
        bias = slope * new_col.astype(F32)
        update([_dot(q_ref[h], kn_ref[h]) + bias[h * hr:(h + 1) * hr] for h in range(H_A)],
               [vn_ref[h] for h in range(H_A)], keep)
        lam = _lambda(lq_ref, lk_ref, lam_init)
        a = acc_scr[...] / l_scr[...]
        for h in range(H_A):
            oh = a[h * hr:h * hr + t_new] - lam * a[h * hr + t_new:(h + 1) * hr]
            ms = jnp.mean(oh * oh, axis=1, keepdims=True)
            o_ref[h] = oh * lax.rsqrt(ms + RMS_EPS) * g_ref[...] * (1.0 - lam_init)


def _attn_sample_call(q, k_new, v_new, cache_k, cache_v, page_table, j, lq, lk, g1, n, t, lam_init):
    n_pages = page_table.shape[1]
    pages = PAGES_PER_STEP
    steps = n_pages // pages
    n_pool = cache_k.shape[1]
    ck = cache_k.transpose(0, 1, 3, 4, 2)
    cv = cache_v.transpose(0, 1, 3, 4, 2)
    kern = functools.partial(_attn_sample_kernel, pages=pages, steps=steps,
                             past_len=n_pages * PAGE_SIZE, t_new=t, lam_init=lam_init)
    qh = jnp.einsum('nthmd,km->nhktmd', q.reshape(n, t, H_A, 2, DK_A), jnp.eye(2, dtype=q.dtype))
    qh = qh.reshape(n, H_A, 2 * t, 2 * DK_A)

    def new_page(x):
        x = x.reshape(n, t, H_A, DV_A).transpose(0, 2, 3, 1)
        return jnp.pad(x, ((0, 0), (0, 0), (0, 0), (0, PAGE_SIZE - t)))

    def page_spec(i):
        return pl.BlockSpec((None, None, H_A, DV_A, PAGE_SIZE),
                            lambda b, s, pt: (j, pt[b, s * pages + i], 0, 0, 0))

    small = lambda shape: pl.BlockSpec(shape, lambda b, s, pt: (0,) * len(shape))
    head_spec = lambda r: pl.BlockSpec((None, H_A, r, DV_A), lambda b, s, pt: (b, 0, 0, 0))
    new_spec = pl.BlockSpec((None, H_A, DV_A, PAGE_SIZE), lambda b, s, pt: (b, 0, 0, 0))
    grid_spec = pltpu.PrefetchScalarGridSpec(
        num_scalar_prefetch=1,
        grid=(n, steps),
        in_specs=[small((2, DK_A)), small((2, DK_A)), small((1, DV_A)),
                  head_spec(2 * t), new_spec, new_spec]
                 + [page_spec(i) for i in range(pages)] * 2,
        out_specs=head_spec(t),
        scratch_shapes=[pltpu.VMEM((H_A * 2 * t, 1), F32), pltpu.VMEM((H_A * 2 * t, 1), F32),
                        pltpu.VMEM((H_A * 2 * t, DV_A), F32)],
    )
    o = pl.pallas_call(
        kern,
        grid_spec=grid_spec,
        out_shape=jax.ShapeDtypeStruct((n, H_A, t, DV_A), F32),
        compiler_params=_cparams("parallel", "arbitrary"),
        name="diff_attn_paged",
    )(page_table, lq, lk, g1, qh, new_page(k_new), new_page(v_new), *([ck] * pages), *([cv] * pages))
    return o.transpose(0, 2, 1, 3).reshape(n * t, W_A).astype(BF16)


def _rwkv_prep_kernel(pb_ref, first_ref, mu_ref, wl_ref, w0_ref, a0_ref, kk_w_ref, ka_w_ref, rk_ref,
                      seg_ref, r_o, w_o, k_o, v_o, kk_o, b_o, bonus_o, g_o, *, period):
    pb = pb_ref[...]
    tm = pb.shape[0]
    row = lax.broadcasted_iota(jnp.int32, (tm, 1), 0)
    prev = jnp.where(row % period == 0, first_ref[...], pltpu.roll(pb, 1, 0))
    xb = pb + (prev - pb) * mu_ref[...]
    r = xb[:, :W_B]
    kb = xb[:, W_B:2 * W_B]
    vb = xb[:, 2 * W_B:3 * W_B]
    z = xb[:, 3 * W_B:]
    ll = lax.broadcasted_iota(jnp.int32, (1, LORA_COLS), 1)
    zz = jnp.where(ll < LORA_W, jnp.tanh(z), jnp.where(ll < LORA_W + LORA_A, z, jax.nn.sigmoid(z)))
    lo = _dot(zz.astype(BF16), wl_ref[...])
    decay = jnp.exp(-DECAY_SCALE * jax.nn.sigmoid(w0_ref[...] + lo[:, :W_B]))
    a = jax.nn.sigmoid(a0_ref[...] + lo[:, W_B:2 * W_B])
    seg = seg_ref[...]
    kk = kb * kk_w_ref[...]
    kk = kk / jnp.maximum(jnp.sqrt(_segsum(kk * kk, seg)), 1e-12)
    k_adj = kb * (1.0 + (a - 1.0) * ka_w_ref[...])
    r_o[...] = r
    w_o[...] = decay
    k_o[...] = k_adj
    v_o[...] = vb
    kk_o[...] = kk
    b_o[...] = kk * a
    bonus_o[...] = _segsum(r * k_adj * rk_ref[...], seg) * vb
    g_o[...] = lo[:, 2 * W_B:]


def _rwkv_prep_call(pb, first, p, j, seg, grp):
    tm, rows = grp.tm, grp.rows
    period = tm if grp.mod_rows == 1 else grp.t
    first_rows = first.shape[1]
    vec = lambda x: x[j].reshape(1, -1)
    wl = jnp.zeros((LORA_COLS, 3 * W_B), F32)
    wl = wl.at[:LORA_W, :W_B].set(p['w_up'][j])
    wl = wl.at[LORA_W:LORA_W + LORA_A, W_B:2 * W_B].set(p['a_up'][j])
    wl = wl.at[LORA_W + LORA_A:, 2 * W_B:].set(p['g_up'][j])
    outs = [jax.ShapeDtypeStruct((rows, W_B), F32)] * 8
    return pl.pallas_call(
        functools.partial(_rwkv_prep_kernel, period=period),
        grid=(grp.tiles,),
        in_specs=[_row_spec(tm, B_COLS),
                  pl.BlockSpec((None, first_rows, B_COLS), lambda i: (i, 0, 0)),
                  _full_spec((1, B_COLS)), _full_spec((LORA_COLS, 3 * W_B)),
                  _full_spec((1, W_B)), _full_spec((1, W_B)), _full_spec((1, W_B)),
                  _full_spec((1, W_B)), _full_spec((1, W_B)), _full_spec((W_B, W_B))],
        out_specs=[_row_spec(tm, W_B)] * 8,
        out_shape=outs,
        compiler_params=_cparams("parallel"),
        name="rwkv_prep",
    )(pb, first, vec(p['mu_b']), wl.astype(BF16), vec(p['w0']), vec(p['a0']), vec(p['k_k']),
      vec(p['k_a']), vec(p['r_k']), seg)


def _rwkv_scan_kernel(kk_ref, w_ref, b_ref, k_ref, r_ref, v_ref, s0_ref, y_ref, s_ref, *, tc):
    @pl.when(pl.program_id(1) == 0)
    def _():
        s_ref[...] = s0_ref[...]

    n_acc = 4

    def step(t0, j):
        row = pl.ds(t0 + j, 1)
        parts = [None] * n_acc
        for k in range(N_B):
            term = s_ref[k] * kk_ref[k, row, :]
            parts[k % n_acc] = term if parts[k % n_acc] is None else parts[k % n_acc] + term
        sa = (parts[0] + parts[1]) + (parts[2] + parts[3])
        v_t = v_ref[t0 + j]
        parts = [None] * n_acc
        for k in range(N_B):
            s_new = s_ref[k] * w_ref[k, row, :] - sa * b_ref[k, row, :] + v_t * k_ref[k, row, :]
            s_ref[k] = s_new
            term = s_new * r_ref[k, row, :]
            parts[k % n_acc] = term if parts[k % n_acc] is None else parts[k % n_acc] + term
        y_ref[t0 + j] = (parts[0] + parts[1]) + (parts[2] + parts[3])

    def steps(i, carry):
        t0 = pl.multiple_of(i * SUBLANES, SUBLANES)
        for j in range(SUBLANES):
            step(t0, j)
        return carry

    lax.fori_loop(0, tc // SUBLANES, steps, 0)


def _to_chain_layout(x, n, t):
    groups = n * H_B // CHAINS
    x = x.reshape(n, t, H_B, N_B).transpose(3, 1, 0, 2).reshape(N_B, t, groups, CHAINS)
    x = x.transpose(2, 0, 1, 3)
    return jnp.concatenate([x, x], axis=-1)


def _value_to_chain_layout(v, n, t):
    groups = n * H_B // CHAINS
    v = v.reshape(n, t, H_B, N_B).transpose(1, 3, 0, 2).reshape(t, 2, N_B // 2, groups, CHAINS)
    return v.transpose(3, 0, 2, 1, 4).reshape(groups, t, N_B // 2, 2 * CHAINS)


def _value_from_chain_layout(y, n, t):
    groups = n * H_B // CHAINS
    y = y.reshape(groups, t, N_B // 2, 2, CHAINS).transpose(1, 3, 2, 0, 4)
    y = y.reshape(t, N_B, n, H_B).transpose(2, 0, 3, 1)
    return y.reshape(n * t, W_B)


def _state_to_chain_layout(s, n):
    groups = n * H_B // CHAINS
    s = s.reshape(groups, CHAINS, 2, N_B // 2, N_B)
    return s.transpose(0, 4, 3, 2, 1).reshape(groups, N_B, N_B // 2, 2 * CHAINS)


def _state_from_chain_layout(s, n):
    groups = n * H_B // CHAINS
    s = s.reshape(groups, N_B, N_B // 2, 2, CHAINS).transpose(0, 4, 3, 2, 1)
    return s.reshape(n, H_B, N_B, N_B)


def _chain_in_kernel(x_ref, o_ref, z_scr, *, per_key, tt):
    n_seq = x_ref.shape[0]
    rows_out = N_B if per_key else N_B // 2
    for s in range(n_seq):
        z_scr[s] = x_ref[s].T
    for j in range(rows_out):
        rows = []
        for half in range(2):
            start = j if per_key else half * (N_B // 2) + j
            rows += [z_scr[s, pl.ds(start, H_B, stride=N_B), :] for s in range(n_seq)]
        tile = jnp.concatenate(rows, axis=0).T
        if per_key:
            o_ref[j] = tile
        else:
            o_ref[pl.ds(j, tt, stride=rows_out), :] = tile


def _chain_in_call(x, n, t, per_key):
    tt = LANES
    if per_key:
        out_spec = pl.BlockSpec((N_B, tt, 2 * CHAINS), lambda i: (0, i, 0))
        out_shape = jax.ShapeDtypeStruct((N_B, t, 2 * CHAINS), F32)
    else:
        out_spec = pl.BlockSpec((tt * (N_B // 2), 2 * CHAINS), lambda i: (i, 0))
        out_shape = jax.ShapeDtypeStruct((t * (N_B // 2), 2 * CHAINS), F32)
    out = pl.pallas_call(
        functools.partial(_chain_in_kernel, per_key=per_key, tt=tt),
        grid=(t // tt,),
        in_specs=[pl.BlockSpec((n, tt, W_B), lambda i: (0, i, 0))],
        out_specs=out_spec,
        out_shape=out_shape,
        scratch_shapes=[pltpu.VMEM((n, W_B, tt), F32)],
        compiler_params=_cparams("parallel"),
        name="chain_layout_in",
    )(x.reshape(n, t, W_B))
    return out[None] if per_key else out.reshape(1, t, N_B // 2, 2 * CHAINS)


def _chain_out_kernel(y_ref, o_ref, z_scr, *, tt):
    n_seq = o_ref.shape[0]
    rows = N_B // 2
    for j in range(rows):
        tile = y_ref[pl.ds(j, tt, stride=rows), :].T
        for half in range(2):
            for s in range(n_seq):
                r0 = (half * n_seq + s) * H_B
                z_scr[s, pl.ds(half * rows + j, H_B, stride=N_B), :] = tile[r0:r0 + H_B]
    for s in range(n_seq):
        o_ref[s] = z_scr[s].T


def _chain_out_call(y, n, t):
    tt = LANES
    rows = N_B // 2
    out = pl.pallas_call(
        functools.partial(_chain_out_kernel, tt=tt),
        grid=(t // tt,),
        in_specs=[pl.BlockSpec((tt * rows, 2 * CHAINS), lambda i: (i, 0))],
        out_specs=pl.BlockSpec((n, tt, W_B), lambda i: (0, i, 0)),
        out_shape=jax.ShapeDtypeStruct((n, t, W_B), F32),
        scratch_shapes=[pltpu.VMEM((n, W_B, tt), F32)],
        compiler_params=_cparams("parallel"),
        name="chain_layout_out",
    )(y.reshape(t * rows, 2 * CHAINS))
    return out.reshape(n * t, W_B)


def _rwkv_scan_call(kk, w, b, k, r, v, s0, n, t):
    groups = n * H_B // CHAINS
    tc = min(SCAN_TC, t)
    vec_spec = pl.BlockSpec((None, N_B, tc, 2 * CHAINS), lambda g, i: (g, 0, i, 0))
    val_spec = pl.BlockSpec((None, tc, N_B // 2, 2 * CHAINS), lambda g, i: (g, i, 0, 0))
    st_spec = pl.BlockSpec((None, N_B, N_B // 2, 2 * CHAINS), lambda g, i: (g, 0, 0, 0))
    in_kernel_layout = groups == 1 and t % LANES == 0
    if in_kernel_layout:
        vecs = [_chain_in_call(x, n, t, True) for x in (kk, w, b, k, r)]
        val = _chain_in_call(v, n, t, False)
    else:
        vecs = [_to_chain_layout(x, n, t) for x in (kk, w, b, k, r)]
        val = _value_to_chain_layout(v, n, t)
    y, s_fin = pl.pallas_call(
        functools.partial(_rwkv_scan_kernel, tc=tc),
        grid=(groups, t // tc),
        in_specs=[vec_spec] * 5 + [val_spec, st_spec],
        out_specs=[val_spec, st_spec],
        out_shape=[jax.ShapeDtypeStruct((groups, t, N_B // 2, 2 * CHAINS), F32),
                   jax.ShapeDtypeStruct((groups, N_B, N_B // 2, 2 * CHAINS), F32)],
        compiler_params=_cparams("parallel", "arbitrary"),
        name="rwkv_scan",
    )(*vecs, val, _state_to_chain_layout(s0, n))
    y = _chain_out_call(y, n, t) if in_kernel_layout else _value_from_chain_layout(y, n, t)
    return y, _state_from_chain_layout(s_fin, n)


def _rwkv_post_kernel(y_ref, bonus_ref, g_ref, lg_ref, lb_ref, seg_ref, o_ref):
    y = y_ref[...]
    seg = seg_ref[...]
    yc = y - _segsum(y, seg) * (1.0 / N_B)
    var = _segsum(yc * yc, seg) * (1.0 / N_B)
    yn = yc * lax.rsqrt(var + LNX_EPS) * lg_ref[...] + lb_ref[...]
    o_ref[...] = ((yn + bonus_ref[...]) * g_ref[...]).astype(o_ref.dtype)


def _rwkv_post_call(y, bonus, g, lg, lb, seg, grp):
    tm = grp.tm
    return pl.pallas_call(
        _rwkv_post_kernel,
        grid=(grp.tiles,),
        in_specs=[_row_spec(tm, W_B)] * 3 + [_full_spec((1, W_B)), _full_spec((1, W_B)),
                                              _full_spec((W_B, W_B))],
        out_specs=_row_spec(tm, W_B),
        out_shape=jax.ShapeDtypeStruct((grp.rows, W_B), BF16),
        compiler_params=_cparams("parallel"),
        name="rwkv_post",
    )(y, bonus, g, lg[None, :], lb[None, :], seg)


def _norm_mod(x, g, sc, sh):
    y = x * lax.rsqrt(jnp.mean(x * x, axis=-1, keepdims=True) + RMS_EPS) * g
    return y * (1.0 + sc) + sh


def _norm_specs(grp):
    return [_full_spec((1, D_MODEL)), _mod_spec(grp), _mod_spec(grp)]


def _residual_out(grp, hn_dtype):
    return dict(out_specs=[_row_spec(grp.tm, D_MODEL), _row_spec(grp.tm, D_MODEL)],
                out_shape=[jax.ShapeDtypeStruct((grp.rows, D_MODEL), F32),
                           jax.ShapeDtypeStruct((grp.rows, D_MODEL), hn_dtype)])


def _outproj_even_kernel(oa_ref, ob_ref, w_ref, x_ref, g_ref, ng_ref, nsc_ref, nsh_ref, o_ref, hn_ref):
    mix = _dot(oa_ref[...], w_ref[:W_A, :]) + _dot(ob_ref[...], w_ref[W_A:, :])
    x = x_ref[...] + g_ref[...] * mix
    o_ref[...] = x
    hn_ref[...] = _norm_mod(x, ng_ref[...], nsc_ref[...], nsh_ref[...]).astype(hn_ref.dtype)


def _outproj_even_call(oa, ob, w, x, gate, norm, grp):
    tm = grp.tm
    ng, nsc, nsh, hn_dtype = norm
    return pl.pallas_call(
        _outproj_even_kernel,
        grid=(grp.tiles,),
        in_specs=[_row_spec(tm, W_A), _row_spec(tm, W_B), _full_spec((W_A + W_B, D_MODEL)),
                  _row_spec(tm, D_MODEL), _mod_spec(grp)] + _norm_specs(grp),
        compiler_params=_cparams("parallel"),
        name="outproj_even",
        **_residual_out(grp, hn_dtype),
    )(oa, ob, w, x, gate, ng[None, :], nsc, nsh)


def _swiglu_up_kernel(a_ref, wg_ref, wu_ref, o_ref):
    a = a_ref[...]
    o_ref[...] = (_silu(_dot(a, wg_ref[...])) * _dot(a, wu_ref[...])).astype(o_ref.dtype)


def _swiglu_up_call(hn, wg, wu, grp):
    tm = grp.tm
    tn = D_FF // 2
    return pl.pallas_call(
        _swiglu_up_kernel,
        grid=(D_FF // tn, grp.tiles),
        in_specs=[pl.BlockSpec((tm, D_MODEL), lambda c, i: (i, 0)),
                  pl.BlockSpec((D_MODEL, tn), lambda c, i: (0, c)),
                  pl.BlockSpec((D_MODEL, tn), lambda c, i: (0, c))],
        out_specs=pl.BlockSpec((tm, tn), lambda c, i: (i, c)),
        out_shape=jax.ShapeDtypeStruct((grp.rows, D_FF), BF16),
        compiler_params=_cparams("parallel", "parallel"),
        name="swiglu_up",
    )(hn, wg, wu)


def _down_res_kernel(h_ref, w_ref, x_ref, g_ref, ng_ref, nsc_ref, nsh_ref, o_ref, hn_ref):
    x = x_ref[...] + g_ref[...] * _dot(h_ref[...], w_ref[...])
    o_ref[...] = x
    hn_ref[...] = _norm_mod(x, ng_ref[...], nsc_ref[...], nsh_ref[...]).astype(hn_ref.dtype)


def _down_res_call(h, w, x, gate, norm, grp):
    tm = grp.tm
    kdim = h.shape[1]
    ng, nsc, nsh, hn_dtype = norm
    return pl.pallas_call(
        _down_res_kernel,
        grid=(grp.tiles,),
        in_specs=[_row_spec(tm, kdim), _full_spec((kdim, D_MODEL)), _row_spec(tm, D_MODEL), _mod_spec(grp)]
                 + _norm_specs(grp),
        compiler_params=_cparams("parallel"),
        name="down_res",
        **_residual_out(grp, hn_dtype),
    )(h, w, x, gate, ng[None, :], nsc, nsh)


def _gelu(x):
    return 0.5 * x * (1.0 + lax.erf(x * INV_SQRT2))


def _gmlp_u_kernel(a_ref, w_ref, o_ref):
    o_ref[...] = _gelu(_dot(a_ref[...], w_ref[...]))


def _gmlp_v_kernel(a_ref, w_ref, lg_ref, lb_ref, vb_ref, vf_ref):
    v = _gelu(_dot(a_ref[...], w_ref[...]))
    vc = v - jnp.mean(v, axis=-1, keepdims=True)
    vn = vc * lax.rsqrt(jnp.mean(vc * vc, axis=-1, keepdims=True) + LN_EPS) * lg_ref[...] + lb_ref[...]
    vb_ref[...] = vn.astype(BF16)
    vf_ref[...] = vn


def _gmlp_in_call(hn, w, lg, lb, grp):
    tm, rows = grp.tm, grp.rows
    wspec = lambda half: pl.BlockSpec((D_MODEL, E_C), lambda i: (0, half))
    u = pl.pallas_call(
        _gmlp_u_kernel,
        grid=(grp.tiles,),
        in_specs=[_row_spec(tm, D_MODEL), wspec(0)],
        out_specs=_row_spec(tm, E_C),
        out_shape=jax.ShapeDtypeStruct((rows, E_C), F32),
        compiler_params=_cparams("parallel"),
        name="gmlp_u",
    )(hn, w)
    vb, vf = pl.pallas_call(
        _gmlp_v_kernel,
        grid=(grp.tiles,),
        in_specs=[_row_spec(tm, D_MODEL), wspec(1), _full_spec((1, E_C)), _full_spec((1, E_C))],
        out_specs=[_row_spec(tm, E_C), _row_spec(tm, E_C)],
        out_shape=[jax.ShapeDtypeStruct((rows, E_C), BF16), jax.ShapeDtypeStruct((rows, E_C), F32)],
        compiler_params=_cparams("parallel"),
        name="gmlp_v",
    )(hn, w, lg[None, :], lb[None, :])
    return u, vb, vf


def _gmlp_out_kernel(u_ref, v_ref, ws_ref, bs_ref, wo_ref, x_ref, g_ref, o_ref, z_scr, *, lm):
    tm = u_ref.shape[0]
    for c in range(tm // lm):
        r0 = c * lm
        for gi in range(G_C):
            c0 = gi * CG
            mixed = _dot(ws_ref[gi], v_ref[r0:r0 + lm, c0:c0 + CG]) + bs_ref[:, gi:gi + 1]
            z_scr[r0:r0 + lm, c0:c0 + CG] = (u_ref[r0:r0 + lm, c0:c0 + CG] * mixed).astype(BF16)
    o_ref[...] = x_ref[...] + g_ref[...] * _dot(z_scr[...], wo_ref[...])


def _gmlp_out_call(u, vb, ws, bs, wo, x, gate, grp):
    tm = grp.tm
    lm = ws.shape[1]
    return pl.pallas_call(
        functools.partial(_gmlp_out_kernel, lm=lm),
        grid=(grp.tiles,),
        in_specs=[_row_spec(tm, E_C), _row_spec(tm, E_C), _full_spec((G_C, lm, lm)), _full_spec((lm, G_C)),
                  _full_spec((E_C, D_MODEL)), _row_spec(tm, D_MODEL), _mod_spec(grp)],
        out_specs=_row_spec(tm, D_MODEL),
        out_shape=jax.ShapeDtypeStruct((grp.rows, D_MODEL), F32),
        scratch_shapes=[pltpu.VMEM((tm, E_C), BF16)],
        compiler_params=_cparams("parallel"),
        name="gmlp_out",
    )(u, vb, ws, bs, wo, x, gate)


def _spatial_weights(w_s, b_s, grp):
    length = min(grp.t, CHUNK)
    ws = jnp.tril(w_s[:, :length, :length])
    bs = b_s[:, :length]
    if length < CHUNK:
        reps = grp.tm // length
        ws = jnp.einsum('ab,gij->gaibj', jnp.eye(reps, dtype=ws.dtype), ws).reshape(
            G_C, reps * length, reps * length)
        bs = jnp.tile(bs, (1, reps))
    return ws.astype(BF16), bs.T


def _moe_kernel(a_ref, logit_ref, tri_ref, wg_ref, wu_ref, wd_ref, x_ref, g_ref, ng_ref, nsc_ref, nsh_ref,
                o_ref, hn_ref, gate_scr, rank_scr, gate_t_scr, rank_t_scr, acc_scr, *, tc):
    e = pl.program_id(1)
    lane = lax.broadcasted_iota(jnp.int32, (1, LANES), 1)

    @pl.when(e == 0)
    def _():
        logits = logit_ref[...]
        m1 = jnp.max(logits, axis=1, keepdims=True)
        i1 = jnp.min(jnp.where(logits == m1, lane, LANES), axis=1, keepdims=True)
        rest = jnp.where(lane == i1, -jnp.inf, logits)
        m2 = jnp.max(rest, axis=1, keepdims=True)
        i2 = jnp.min(jnp.where(rest == m2, lane, LANES), axis=1, keepdims=True)
        e2 = jnp.exp(m2 - m1)
        gate = jnp.where(lane == i1, 1.0 / (1.0 + e2), jnp.where(lane == i2, e2 / (1.0 + e2), 0.0))
        picked = gate != 0.0
        rank = _dot(tri_ref[...], jnp.where(picked, 1.0, 0.0).astype(BF16))
        rank = jnp.where(picked, rank, -1.0)
        gate_scr[...] = gate
        rank_scr[...] = rank
        gate_t_scr[...] = gate.T
        rank_t_scr[...] = rank.T
        acc_scr[...] = jnp.zeros(acc_scr.shape, F32)

    gate_col = jnp.sum(jnp.where(lane == e, gate_scr[...], 0.0), axis=1, keepdims=True)
    rank_col = jnp.sum(jnp.where(lane == e, rank_scr[...], 0.0), axis=1, keepdims=True)
    gate_row = gate_t_scr[pl.ds(e, 1), :]
    rank_row = rank_t_scr[pl.ds(e, 1), :]
    count = jnp.sum(jnp.where(rank_row >= 0.0, 1, 0))
    slot_col = lax.broadcasted_iota(jnp.int32, (tc, 1), 0).astype(F32)
    slot_row = lax.broadcasted_iota(jnp.int32, (1, tc), 1).astype(F32)

    def chunk(c, carry):
        base = (c * tc).astype(F32)
        take = rank_row == base + slot_col
        xg = _dot(jnp.where(take, 1.0, 0.0).astype(BF16), a_ref[...]).astype(BF16)
        h = (_silu(_dot(xg, wg_ref[...])) * _dot(xg, wu_ref[...])).astype(BF16)
        y = _dot(h, wd_ref[...])
        gate_c = jnp.sum(jnp.where(take, gate_row, 0.0), axis=1, keepdims=True)
        put = jnp.where(rank_col == base + slot_row, 1.0, 0.0).astype(BF16)
        acc_scr[...] += _dot(put, (gate_c * y).astype(BF16))
        return carry

    lax.fori_loop(0, (count + tc - 1) // tc, chunk, 0)

    @pl.when(e == N_EXPERTS - 1)
    def _():
        x = x_ref[...] + g_ref[...] * acc_scr[...]
        o_ref[...] = x
        hn_ref[...] = _norm_mod(x, ng_ref[...], nsc_ref[...], nsh_ref[...]).astype(hn_ref.dtype)


def _moe_call(hn, logits, wg, wu, wd, j, x, gate, norm, grp):
    tm = grp.tm
    tps = grp.tiles_per_seq
    tc = min(MOE_CHUNK, tm)
    ng, nsc, nsh, hn_dtype = norm
    mod = pl.BlockSpec((None, grp.mod_rows, D_MODEL), lambda i, e: (i // tps, 0, 0))
    tri = jnp.tril(jnp.ones((tm, tm), F32), -1).astype(BF16)
    row = lambda cols: pl.BlockSpec((tm, cols), lambda i, e: (i, 0))
    return pl.pallas_call(
        functools.partial(_moe_kernel, tc=tc),
        grid=(grp.tiles, N_EXPERTS),
        in_specs=[row(D_MODEL), row(LANES), pl.BlockSpec((tm, tm), lambda i, e: (0, 0)),
                  pl.BlockSpec((None, None, D_MODEL, D_FF_E), lambda i, e: (j, e, 0, 0)),
                  pl.BlockSpec((None, None, D_MODEL, D_FF_E), lambda i, e: (j, e, 0, 0)),
                  pl.BlockSpec((None, None, D_FF_E, D_MODEL), lambda i, e: (j, e, 0, 0)),
                  row(D_MODEL), mod,
                  pl.BlockSpec((1, D_MODEL), lambda i, e: (0, 0)), mod, mod],
        out_specs=[row(D_MODEL), row(D_MODEL)],
        out_shape=[jax.ShapeDtypeStruct((grp.rows, D_MODEL), F32),
                   jax.ShapeDtypeStruct((grp.rows, D_MODEL), hn_dtype)],
        scratch_shapes=[pltpu.VMEM((tm, LANES), F32), pltpu.VMEM((tm, LANES), F32),
                        pltpu.VMEM((LANES, tm), F32), pltpu.VMEM((LANES, tm), F32),
                        pltpu.VMEM((tm, D_MODEL), F32)],
        compiler_params=_cparams("parallel", "arbitrary"),
        name="moe_experts",
    )(hn, logits, tri, wg, wu, wd, x, gate, ng[None, :], nsc, nsh)


def _forward(x, mods, p, wb, grp, cache):
    n, t = grp.n, grp.t
    seg = jnp.kron(jnp.eye(H_B, dtype=F32), jnp.ones((N_B, N_B), F32)).astype(BF16)
    slopes = jnp.exp2(-8.0 * jnp.arange(1, H_A + 1, dtype=F32) / H_A)
    new_k, new_v, new_wkv, new_shift, new_cv = [], [], [], [], []
    mods = [[_expand_mod(m, grp) for m in layer] for layer in mods]

    def next_norm(i):
        if i + 1 < DEPTH:
            sh, sc = mods[i + 1][0], mods[i + 1][1]
            return p['norm1_g'][i + 1], sc, sh, BF16
        zero = jnp.zeros_like(mods[i][0])
        return p['final_g'], zero, zero, F32

    hn = _norm_mod_call(x, p['norm1_g'][0], mods[0][1], mods[0][0], grp)
    for i in range(DEPTH):
        j = i // 2
        sh1, sc1, g1, sh2, sc2, g2 = mods[i]
        if i % 2 == 0:
            q, k, v, kb, vb, pb = _inproj_even_call(hn, wb['w_in_e'][j], grp)
            lam_init = 0.8 - 0.6 * math.exp(-0.3 * i)
            lq = jnp.stack([p['lam_q1'][j], p['lam_q2'][j]])
            lk = jnp.stack([p['lam_k1'][j], p['lam_k2'][j]])
            if cache is None:
                g2h = jnp.tile(p['subln_g'][j], 2)[None, :]
                oa = _attn_prompt_call(q, kb, vb, slopes, lq, lk, g2h, n, t, lam_init)
                wkv0 = jnp.zeros((n, H_B, N_B, N_B), F32)
                last = pb.reshape(grp.tiles, grp.tm, B_COLS)[:, -1, :]
                first = jnp.concatenate([jnp.zeros((1, B_COLS), F32), last[:-1]], axis=0)
                starts = (jnp.arange(grp.tiles) % grp.tiles_per_seq == 0)[:, None]
                first = jnp.where(starts, 0.0, first)[:, None, :]
            else:
                cache_k, cache_v, page_table, state_wkv, state_shift = cache
                oa = _attn_sample_call(q, kb, vb, cache_k, cache_v, page_table, j, lq, lk,
                                       p['subln_g'][j][None, :], n, t, lam_init)
                wkv0 = state_wkv[j]
                first = jnp.repeat(state_shift[j], t, axis=0)[None]
            r, w, kv, vv, kk, b, bonus, g = _rwkv_prep_call(pb, first, p, j, seg, grp)
            y, s_fin = _rwkv_scan_call(kk, w, b, kv, r, vv, wkv0, n, t)
            ob = _rwkv_post_call(y, bonus, g, p['lnx_g'][j], p['lnx_b'][j], seg, grp)
            new_k.append(k.reshape(n, t, H_A, 2 * DK_A))
            new_v.append(v.reshape(n, t, H_A, DV_A))
            new_wkv.append(s_fin)
            new_shift.append(pb.reshape(n, t, B_COLS)[:, -1])
            x, hn = _outproj_even_call(oa.reshape(grp.rows, W_A), ob, wb['w_out_e'][j], x, g1,
                                       (p['norm2_g'][i], sc2, sh2, BF16), grp)
            h = _swiglu_up_call(hn, wb['ffn_gate'][j], wb['ffn_up'][j], grp)
            x, hn = _down_res_call(h, wb['ffn_down'][j], x, g2, next_norm(i), grp)
        else:
            u, vb, vf = _gmlp_in_call(hn, wb['w_in_o'][j], p['lnv_g'][j], p['lnv_b'][j], grp)
            new_cv.append(vf.reshape(n, t, E_C))
            ws, bs = _spatial_weights(p['w_s'][j], p['b_s'][j], grp)
            x = _gmlp_out_call(u, vb, ws, bs, wb['w_out_o'][j], x, g1, grp)
            hn, logits = _norm_mod_router_call(x, p['norm2_g'][i], sc2, sh2, p['router_w'][j],
                                               p['router_b'][j], grp)
            moe_grp = grp if grp.mod_rows > 1 else _Group(n, t, min(MOE_TILE, t))
            x, hn = _moe_call(hn, logits, wb['exp_gate'], wb['exp_up'], wb['exp_down'], j, x, g2,
                              next_norm(i), moe_grp)
    return (hn.reshape(n, t, D_MODEL), jnp.stack(new_k), jnp.stack(new_v), jnp.stack(new_wkv),
            jnp.stack(new_shift), jnp.stack(new_cv))


_BF16_WEIGHTS = ('w_in_e', 'w_out_e', 'ffn_gate', 'ffn_up', 'ffn_down', 'w_in_o', 'w_out_o',
                 'exp_gate', 'exp_up', 'exp_down')


def _run(x_prompt, x_sample, cache_k, cache_v, state_wkv, state_shift, page_table, c_prompt, c_sample, p):
    n_p, t_p, _ = x_prompt.shape
    n_s, t_s, _ = x_sample.shape
    grp_p = _Group(n_p, t_p, min(ROW_TILE, t_p))
    grp_s = _Group(n_s, t_s, n_s * t_s)
    wb = {name: p[name].astype(BF16) for name in _BF16_WEIGHTS}
    mod = _mod_call(jnp.concatenate([c_prompt, c_sample], axis=0), p['w_mod'], p['b_mod'])
    mods_p = [jnp.split(mod[i, :n_p], 6, axis=-1) for i in range(DEPTH)]
    mods_s = [jnp.split(mod[i, n_p:], 6, axis=-1) for i in range(DEPTH)]
    y_p, k_p, v_p, wkv_p, sh_p, _ = _forward(x_prompt.reshape(n_p * t_p, D_MODEL), mods_p, p, wb, grp_p, None)
    y_s, k_s, v_s, wkv_s, sh_s, cv_s = _forward(
        x_sample.reshape(n_s * t_s, D_MODEL), mods_s, p, wb, grp_s,
        (cache_k, cache_v, page_table, state_wkv, state_shift))
    return (y_p, y_s, k_p, v_p, wkv_p, sh_p, k_s, v_s, wkv_s, sh_s, cv_s)


def kernel(x_prompt, x_sample, cache_k, cache_v, state_wkv, state_shift, page_table, c_prompt, c_sample,
           w_mod, b_mod, norm1_g, norm2_g, final_g, w_in_e, w_out_e, lam_q1, lam_k1, lam_q2, lam_k2,
           subln_g, mu_b, w0, w_up, a0, a_up, g_up, k_k, k_a, r_k, lnx_g, lnx_b, w_in_o, lnv_g, lnv_b,
           w_s, b_s, w_out_o, ffn_gate, ffn_up, ffn_down, router_w, router_b, exp_gate, exp_up, exp_down):
    p = dict(w_mod=w_mod, b_mod=b_mod, norm1_g=norm1_g, norm2_g=norm2_g, final_g=final_g,
             w_in_e=w_in_e, w_out_e=w_out_e, lam_q1=lam_q1, lam_k1=lam_k1, lam_q2=lam_q2,
             lam_k2=lam_k2, subln_g=subln_g, mu_b=mu_b, w0=w0, w_up=w_up, a0=a0, a_up=a_up,
             g_up=g_up, k_k=k_k, k_a=k_a, r_k=r_k, lnx_g=lnx_g, lnx_b=lnx_b, w_in_o=w_in_o,
             lnv_g=lnv_g, lnv_b=lnv_b, w_s=w_s, b_s=b_s, w_out_o=w_out_o, ffn_gate=ffn_gate,
             ffn_up=ffn_up, ffn_down=ffn_down, router_w=router_w, router_b=router_b,
             exp_gate=exp_gate, exp_up=exp_up, exp_down=exp_down)
    return _run(x_prompt, x_sample, cache_k, cache_v, state_wkv, state_shift, page_table,
                c_prompt, c_sample, p)
```

```python
import functools
import math
from typing import NamedTuple

import jax
import jax.numpy as jnp
from jax import lax
from jax.experimental import pallas as pl
from jax.experimental.pallas import tpu as pltpu

F32 = jnp.float32
BF16 = jnp.bfloat16

D_MODEL = 1024
DEPTH = 4
H_A, DK_A, DV_A = 8, 32, 64
QK_A = H_A * 2 * DK_A
W_A = H_A * DV_A
A_COLS = 2 * QK_A + W_A
H_B, N_B = 8, 64
W_B = H_B * N_B
LORA_W, LORA_A, LORA_G = 64, 64, 128
LORA_COLS = LORA_W + LORA_A + LORA_G
B_COLS = 3 * W_B + LORA_COLS
IN_COLS_EVEN = A_COLS + B_COLS
CHUNK, G_C = 128, 8
E_C = 2 * D_MODEL
CG = E_C // G_C
D_FF = 2816
N_EXPERTS, D_FF_E = 8, 1408
PAGE_SIZE = 128
RMS_EPS, LN_EPS, LNX_EPS = 1e-6, 1e-5, 64e-5
NEG_INF = -1e30
ATTN_SCALE = DK_A ** -0.5
DECAY_SCALE = math.exp(-0.5)
INV_SQRT2 = 2.0 ** -0.5

LANES = 128
SUBLANES = 8
MIB = 1 << 20
VMEM_LIMIT = 56 * MIB

ROW_TILE = 512
ATTN_TQ, ATTN_TK = 256, 256
PAGES_PER_STEP = 32
MOE_TILE = 512
MOE_CHUNK = 160
SCAN_TC = 32
CHAINS = 64


class _Group(NamedTuple):
    n: int
    t: int
    tm: int

    @property
    def rows(self):
        return self.n * self.t

    @property
    def tiles(self):
        return self.rows // self.tm

    @property
    def tiles_per_seq(self):
        return max(self.t // self.tm, 1)

    @property
    def mod_rows(self):
        return 1 if self.t >= self.tm else self.tm


def _cparams(*sem):
    return pltpu.CompilerParams(dimension_semantics=sem, vmem_limit_bytes=VMEM_LIMIT)


def _row_spec(tm, cols):
    return pl.BlockSpec((tm, cols), lambda i: (i, 0))


def _full_spec(shape):
    zeros = (0,) * len(shape)
    return pl.BlockSpec(shape, lambda *_: zeros)


def _mod_spec(grp):
    tps = grp.tiles_per_seq
    return pl.BlockSpec((None, grp.mod_rows, D_MODEL), lambda i: (i // tps, 0, 0))


def _expand_mod(m, grp):
    if grp.mod_rows == 1:
        return m[:, None, :]
    return jnp.repeat(m, grp.t, axis=0)[None]


def _dot(a, b):
    return jnp.dot(a, b, preferred_element_type=F32)


def _dot_nt(a, b):
    return lax.dot_general(a, b, (((1,), (1,)), ((), ())), preferred_element_type=F32)


def _split_bf16(x):
    hi = x.astype(BF16)
    lo = (x - hi.astype(F32)).astype(BF16)
    return hi, lo


def _segsum(x, seg):
    hi, lo = _split_bf16(x)
    return _dot(hi, seg) + _dot(lo, seg)


def _silu(x):
    return x * jax.nn.sigmoid(x)


def _mod_kernel(c_ref, w_ref, b_ref, o_ref):
    a_hi, a_lo = _split_bf16(_silu(c_ref[...]))
    w_hi, w_lo = _split_bf16(w_ref[...])
    o_ref[...] = _dot(a_hi, w_hi) + _dot(a_hi, w_lo) + _dot(a_lo, w_hi) + b_ref[...]


def _mod_call(c_all, w_mod, b_mod):
    rows = c_all.shape[0]
    tn = D_MODEL
    return pl.pallas_call(
        _mod_kernel,
        grid=(DEPTH, 6 * D_MODEL // tn),
        in_specs=[
            pl.BlockSpec((rows, D_MODEL), lambda i, j: (0, 0)),
            pl.BlockSpec((None, D_MODEL, tn), lambda i, j: (i, 0, j)),
            pl.BlockSpec((None, 1, tn), lambda i, j: (i, 0, j)),
        ],
        out_specs=pl.BlockSpec((None, rows, tn), lambda i, j: (i, 0, j)),
        out_shape=jax.ShapeDtypeStruct((DEPTH, rows, 6 * D_MODEL), F32),
        compiler_params=_cparams("parallel", "parallel"),
        name="adaln_mod",
    )(c_all, w_mod, b_mod[:, None, :])


def _norm_mod_kernel(x_ref, g_ref, sc_ref, sh_ref, o_ref):
    x = x_ref[...]
    y = x * lax.rsqrt(jnp.mean(x * x, axis=-1, keepdims=True) + RMS_EPS) * g_ref[...]
    o_ref[...] = (y * (1.0 + sc_ref[...]) + sh_ref[...]).astype(o_ref.dtype)


def _norm_mod_call(x, g, sc, sh, grp):
    return pl.pallas_call(
        _norm_mod_kernel,
        grid=(grp.tiles,),
        in_specs=[_row_spec(grp.tm, D_MODEL), _full_spec((1, D_MODEL)), _mod_spec(grp), _mod_spec(grp)],
        out_specs=_row_spec(grp.tm, D_MODEL),
        out_shape=jax.ShapeDtypeStruct((grp.rows, D_MODEL), BF16),
        compiler_params=_cparams("parallel"),
        name="norm_mod",
    )(x, g[None, :], sc, sh)


def _norm_mod_router_kernel(x_ref, g_ref, sc_ref, sh_ref, wr_ref, br_ref, o_ref, logit_ref):
    x = x_ref[...]
    y = x * lax.rsqrt(jnp.mean(x * x, axis=-1, keepdims=True) + RMS_EPS) * g_ref[...]
    hn = y * (1.0 + sc_ref[...]) + sh_ref[...]
    o_ref[...] = hn.astype(o_ref.dtype)
    h_hi, h_lo = _split_bf16(hn)
    w_hi, w_lo = _split_bf16(wr_ref[...])
    logit_ref[...] = _dot(h_hi, w_hi) + _dot(h_hi, w_lo) + _dot(h_lo, w_hi) + br_ref[...]


def _norm_mod_router_call(x, g, sc, sh, wr, br, grp):
    wr_pad = jnp.zeros((D_MODEL, LANES), F32).at[:, :N_EXPERTS].set(wr)
    br_pad = jnp.full((1, LANES), -jnp.inf, F32).at[0, :N_EXPERTS].set(br)
    return pl.pallas_call(
        _norm_mod_router_kernel,
        grid=(grp.tiles,),
        in_specs=[_row_spec(grp.tm, D_MODEL), _full_spec((1, D_MODEL)), _mod_spec(grp), _mod_spec(grp),
                  _full_spec((D_MODEL, LANES)), _full_spec((1, LANES))],
        out_specs=[_row_spec(grp.tm, D_MODEL), _row_spec(grp.tm, LANES)],
        out_shape=[jax.ShapeDtypeStruct((grp.rows, D_MODEL), BF16),
                   jax.ShapeDtypeStruct((grp.rows, LANES), F32)],
        compiler_params=_cparams("parallel"),
        name="norm_mod_router",
    )(x, g[None, :], sc, sh, wr_pad, br_pad)


def _inproj_even_kernel(a_ref, w_ref, q_ref, k_ref, v_ref, kb_ref, vb_ref, pb_ref):
    a = a_ref[...]
    q_ref[...] = (_dot(a, w_ref[:, :QK_A]) * ATTN_SCALE).astype(BF16)
    k = _dot(a, w_ref[:, QK_A:2 * QK_A])
    k_ref[...] = k
    kb_ref[...] = k.astype(BF16)
    v = _dot(a, w_ref[:, 2 * QK_A:A_COLS])
    v_ref[...] = v
    vb_ref[...] = v.astype(BF16)
    half = B_COLS // 2
    pb_ref[:, :half] = _dot(a, w_ref[:, A_COLS:A_COLS + half])
    pb_ref[:, half:] = _dot(a, w_ref[:, A_COLS + half:])


def _inproj_even_call(hn, w, grp):
    tm, rows = grp.tm, grp.rows
    sds = jax.ShapeDtypeStruct
    return pl.pallas_call(
        _inproj_even_kernel,
        grid=(grp.tiles,),
        in_specs=[_row_spec(tm, D_MODEL), _full_spec((D_MODEL, IN_COLS_EVEN))],
        out_specs=[_row_spec(tm, QK_A), _row_spec(tm, QK_A), _row_spec(tm, W_A),
                   _row_spec(tm, QK_A), _row_spec(tm, W_A), _row_spec(tm, B_COLS)],
        out_shape=[sds((rows, QK_A), BF16), sds((rows, QK_A), F32), sds((rows, W_A), F32),
                   sds((rows, QK_A), BF16), sds((rows, W_A), BF16), sds((rows, B_COLS), F32)],
        compiler_params=_cparams("parallel"),
        name="inproj_even",
    )(hn, w)


def _lambda(lq_ref, lk_ref, lam_init):
    t = jnp.sum(lq_ref[...] * lk_ref[...], axis=1, keepdims=True)
    e = jnp.exp(t)
    return e[0:1] - e[1:2] + lam_init


def _attn_prompt_kernel(lq_ref, lk_ref, g_ref, qf_ref, q_ref, k_ref, kf_ref, vt_ref, o_ref,
                        m_scr, l_scr, acc_scr, *, tq, tk, lam_init):
    hp = pl.program_id(1)
    qi = pl.program_id(2)
    lane = lax.broadcasted_iota(jnp.int32, (1, LANES), 1)
    q = q_ref[...]
    q_aug = []
    for i in range(4):
        q_map = jnp.where(lane // DK_A == i, q, jnp.zeros_like(q))
        feat = jnp.broadcast_to(qf_ref[pl.ds(2 * hp + i // 2, 1), :], (tq, LANES)).astype(BF16)
        q_aug.append(jnp.concatenate([q_map, feat], axis=1))
    m_scr[...] = jnp.full(m_scr.shape, NEG_INF, F32)
    l_scr[...] = jnp.zeros(l_scr.shape, F32)
    acc_scr[...] = jnp.zeros(acc_scr.shape, F32)
    row = lax.broadcasted_iota(jnp.int32, (tk, 1), 0)
    col = lax.broadcasted_iota(jnp.int32, (1, tq), 1)
    q0 = qi * tq

    def scores(kt):
        k0 = pl.multiple_of(kt * tk, tk)
        k_aug = jnp.concatenate([k_ref[pl.ds(k0, tk), :], kf_ref[pl.ds(k0, tk), :]], axis=1)
        return tuple(_dot_nt(k_aug, q_aug[i]) for i in range(4))

    def softmax_pv(kt, s_maps, masked):
        vt = vt_ref[kt]
        if masked:
            keep = (kt * tk + row) <= (q0 + col)
        for i in range(4):
            h = i // 2
            s = jnp.where(keep, s_maps[i], NEG_INF) if masked else s_maps[i]
            m_old = m_scr[i]
            m_new = jnp.maximum(m_old, jnp.max(s, axis=0, keepdims=True))
            alpha = jnp.exp(m_old - m_new)
            p = jnp.exp(s - m_new)
            l_scr[i] = alpha * l_scr[i] + jnp.sum(p, axis=0, keepdims=True)
            acc_scr[i] = alpha * acc_scr[i] + _dot(vt[h * DV_A:(h + 1) * DV_A, :], p.astype(BF16))
            m_scr[i] = m_new

    n_full = q0 // tk

    def body(kt, s_maps):
        s_next = scores(kt + 1)
        softmax_pv(kt, s_maps, False)
        return s_next

    s_last = lax.fori_loop(0, n_full, body, scores(0))
    softmax_pv(n_full, s_last, True)

    lam = _lambda(lq_ref, lk_ref, lam_init)
    heads = []
    for h in range(2):
        oh = acc_scr[2 * h] / l_scr[2 * h] - lam * (acc_scr[2 * h + 1] / l_scr[2 * h + 1])
        ms = jnp.mean(oh * oh, axis=0, keepdims=True)
        heads.append(oh * lax.rsqrt(ms + RMS_EPS))
    o = jnp.concatenate(heads, axis=0).T
    o_ref[...] = (o * g_ref[...] * (1.0 - lam_init)).astype(o_ref.dtype)


def _attn_prompt_call(q, k, v, slopes, lq, lk, g2, n, t, lam_init):
    tq, tk = ATTN_TQ, min(ATTN_TK, t)
    tq = min(tq, tk)
    kern = functools.partial(_attn_prompt_kernel, tq=tq, tk=tk, lam_init=lam_init)
    pos = jnp.arange(t)
    k_feat = jnp.zeros((t, LANES), F32).at[:, 0].set(pos // 256).at[:, 1].set(pos % 256).astype(BF16)
    q_feat = jnp.zeros((H_A, LANES), F32).at[:, 0].set(256.0 * slopes).at[:, 1].set(slopes)
    v_t = v.reshape(n, t // tk, tk, W_A).transpose(0, 1, 3, 2)
    qspec = pl.BlockSpec((None, tq, LANES), lambda b, hp, qi: (b, qi, hp))
    return pl.pallas_call(
        kern,
        grid=(n, H_A // 2, t // tq),
        in_specs=[_full_spec((2, DK_A)), _full_spec((2, DK_A)), _full_spec((1, LANES)),
                  _full_spec((H_A, LANES)), qspec,
                  pl.BlockSpec((None, t, LANES), lambda b, hp, qi: (b, 0, hp)),
                  _full_spec((t, LANES)),
                  pl.BlockSpec((None, t // tk, LANES, tk), lambda b, hp, qi: (b, 0, hp, 0))],
        out_specs=qspec,
        out_shape=jax.ShapeDtypeStruct((n, t, W_A), BF16),
        scratch_shapes=[pltpu.VMEM((4, 1, tq), F32), pltpu.VMEM((4, 1, tq), F32),
                        pltpu.VMEM((4, DV_A, tq), F32)],
        compiler_params=_cparams("parallel", "parallel", "parallel"),
        name="diff_attn_prompt",
    )(lq, lk, g2, q_feat, q.reshape(n, t, QK_A), k.reshape(n, t, QK_A), k_feat, v_t)


def _attn_sample_kernel(pt_ref, lq_ref, lk_ref, g_ref, q_ref, kn_ref, vn_ref, *rest,
                        pages, steps, past_len, t_new, lam_init):
    k_pages, v_pages = rest[:pages], rest[pages:2 * pages]
    o_ref, m_scr, l_scr, acc_scr = rest[2 * pages:]
    step = pl.program_id(1)
    hr = 2 * t_new
    row = lax.broadcasted_iota(jnp.int32, (H_A * hr, 1), 0)
    slope = jnp.exp2(-(row // hr + 1).astype(F32))

    @pl.when(step == 0)
    def _():
        m_scr[...] = jnp.full(m_scr.shape, NEG_INF, F32)
        l_scr[...] = jnp.zeros(l_scr.shape, F32)
        acc_scr[...] = jnp.zeros(acc_scr.shape, F32)

    def update(scores, values_t, keep):
        s = jnp.concatenate(scores, axis=0)
        if keep is not None:
            s = jnp.where(keep, s, NEG_INF)
        m_old = m_scr[...]
        m_new = jnp.maximum(m_old, jnp.max(s, axis=1, keepdims=True))
        alpha = jnp.exp(m_old - m_new)
        p = jnp.exp(s - m_new)
        l_scr[...] = alpha * l_scr[...] + jnp.sum(p, axis=1, keepdims=True)
        p = p.astype(BF16)
        pv = jnp.concatenate([_dot_nt(p[h * hr:(h + 1) * hr], values_t[h]) for h in range(H_A)], axis=0)
        acc_scr[...] = alpha * acc_scr[...] + pv
        m_scr[...] = m_new

    col = lax.broadcasted_iota(jnp.int32, (1, pages * PAGE_SIZE), 1)
    k_rel = (step * (pages * PAGE_SIZE) - past_len + col).astype(F32)
    scores, values_t = [], []
    for h in range(H_A):
        k_t = jnp.concatenate([kp[h] for kp in k_pages], axis=1).astype(BF16)
        values_t.append(jnp.concatenate([vp[h] for vp in v_pages], axis=1).astype(BF16))
        scores.append(_dot(q_ref[h], k_t))
    update([s + slope[h * hr:(h + 1) * hr] * k_rel for h, s in enumerate(scores)], values_t, None)

    @pl.when(step == steps - 1)
    def _():
        new_col = lax.broadcasted_iota(jnp.int32, (1, PAGE_SIZE), 1)
        keep = new_col <= row % t_new
        bias = slope * new_col.astype(F32)
        update([_dot(q_ref[h], kn_ref[h]) + bias[h * hr:(h + 1) * hr] for h in range(H_A)],
               [vn_ref[h] for h in range(H_A)], keep)
        lam = _lambda(lq_ref, lk_ref, lam_init)
        a = acc_scr[...] / l_scr[...]
        for h in range(H_A):
            oh = a[h * hr:h * hr + t_new] - lam * a[h * hr + t_new:(h + 1) * hr]
            ms = jnp.mean(oh * oh, axis=1, keepdims=True)
            o_ref[h] = oh * lax.rsqrt(ms + RMS_EPS) * g_ref[...] * (1.0 - lam_init)


def _attn_sample_call(q, k_new, v_new, cache_k, cache_v, page_table, j, lq, lk, g1, n, t, lam_init):
    n_pages = page_table.shape[1]
    pages = PAGES_PER_STEP
    steps = n_pages // pages
    n_pool = cache_k.shape[1]
    ck = cache_k.transpose(0, 1, 3, 4, 2)
    cv = cache_v.transpose(0, 1, 3, 4, 2)
    kern = functools.partial(_attn_sample_kernel, pages=pages, steps=steps,
                             past_len=n_pages * PAGE_SIZE, t_new=t, lam_init=lam_init)
    qh = jnp.einsum('nthmd,km->nhktmd', q.reshape(n, t, H_A, 2, DK_A), jnp.eye(2, dtype=q.dtype))
    qh = qh.reshape(n, H_A, 2 * t, 2 * DK_A)

    def new_page(x):
        x = x.reshape(n, t, H_A, DV_A).transpose(0, 2, 3, 1)
        return jnp.pad(x, ((0, 0), (0, 0), (0, 0), (0, PAGE_SIZE - t)))

    def page_spec(i):
        return pl.BlockSpec((None, None, H_A, DV_A, PAGE_SIZE),
                            lambda b, s, pt: (j, pt[b, s * pages + i], 0, 0, 0))

    small = lambda shape: pl.BlockSpec(shape, lambda b, s, pt: (0,) * len(shape))
    head_spec = lambda r: pl.BlockSpec((None, H_A, r, DV_A), lambda b, s, pt: (b, 0, 0, 0))
    new_spec = pl.BlockSpec((None, H_A, DV_A, PAGE_SIZE), lambda b, s, pt: (b, 0, 0, 0))
    grid_spec = pltpu.PrefetchScalarGridSpec(
        num_scalar_prefetch=1,
        grid=(n, steps),
        in_specs=[small((2, DK_A)), small((2, DK_A)), small((1, DV_A)),
                  head_spec(2 * t), new_spec, new_spec]
                 + [page_spec(i) for i in range(pages)] * 2,
        out_specs=head_spec(t),
        scratch_shapes=[pltpu.VMEM((H_A * 2 * t, 1), F32), pltpu.VMEM((H_A * 2 * t, 1), F32),
                        pltpu.VMEM((H_A * 2 * t, DV_A), F32)],
    )
    o = pl.pallas_call(
        kern,
        grid_spec=grid_spec,
        out_shape=jax.ShapeDtypeStruct((n, H_A, t, DV_A), F32),
        compiler_params=_cparams("parallel", "arbitrary"),
        name="diff_attn_paged",
    )(page_table, lq, lk, g1, qh, new_page(k_new), new_page(v_new), *([ck] * pages), *([cv] * pages))
    return o.transpose(0, 2, 1, 3).reshape(n * t, W_A).astype(BF16)


def _rwkv_prep_kernel(pb_ref, first_ref, mu_ref, wl_ref, w0_ref, a0_ref, kk_w_ref, ka_w_ref, rk_ref,
                      seg_ref, r_o, w_o, k_o, v_o, kk_o, b_o, bonus_o, g_o, *, period):
    pb = pb_ref[...]
    tm = pb.shape[0]
    row = lax.broadcasted_iota(jnp.int32, (tm, 1), 0)
    prev = jnp.where(row % period == 0, first_ref[...], pltpu.roll(pb, 1, 0))
    xb = pb + (prev - pb) * mu_ref[...]
    r = xb[:, :W_B]
    kb = xb[:, W_B:2 * W_B]
    vb = xb[:, 2 * W_B:3 * W_B]
    z = xb[:, 3 * W_B:]
    ll = lax.broadcasted_iota(jnp.int32, (1, LORA_COLS), 1)
    zz = jnp.where(ll < LORA_W, jnp.tanh(z), jnp.where(ll < LORA_W + LORA_A, z, jax.nn.sigmoid(z)))
    lo = _dot(zz.astype(BF16), wl_ref[...])
    decay = jnp.exp(-DECAY_SCALE * jax.nn.sigmoid(w0_ref[...] + lo[:, :W_B]))
    a = jax.nn.sigmoid(a0_ref[...] + lo[:, W_B:2 * W_B])
    seg = seg_ref[...]
    kk = kb * kk_w_ref[...]
    kk = kk / jnp.maximum(jnp.sqrt(_segsum(kk * kk, seg)), 1e-12)
    k_adj = kb * (1.0 + (a - 1.0) * ka_w_ref[...])
    r_o[...] = r
    w_o[...] = decay
    k_o[...] = k_adj
    v_o[...] = vb
    kk_o[...] = kk
    b_o[...] = kk * a
    bonus_o[...] = _segsum(r * k_adj * rk_ref[...], seg) * vb
    g_o[...] = lo[:, 2 * W_B:]


def _rwkv_prep_call(pb, first, p, j, seg, grp):
    tm, rows = grp.tm, grp.rows
    period = tm if grp.mod_rows == 1 else grp.t
    first_rows = first.shape[1]
    vec = lambda x: x[j].reshape(1, -1)
    wl = jnp.zeros((LORA_COLS, 3 * W_B), F32)
    wl = wl.at[:LORA_W, :W_B].set(p['w_up'][j])
    wl = wl.at[LORA_W:LORA_W + LORA_A, W_B:2 * W_B].set(p['a_up'][j])
    wl = wl.at[LORA_W + LORA_A:, 2 * W_B:].set(p['g_up'][j])
    outs = [jax.ShapeDtypeStruct((rows, W_B), F32)] * 8
    return pl.pallas_call(
        functools.partial(_rwkv_prep_kernel, period=period),
        grid=(grp.tiles,),
        in_specs=[_row_spec(tm, B_COLS),
                  pl.BlockSpec((None, first_rows, B_COLS), lambda i: (i, 0, 0)),
                  _full_spec((1, B_COLS)), _full_spec((LORA_COLS, 3 * W_B)),
                  _full_spec((1, W_B)), _full_spec((1, W_B)), _full_spec((1, W_B)),
                  _full_spec((1, W_B)), _full_spec((1, W_B)), _full_spec((W_B, W_B))],
        out_specs=[_row_spec(tm, W_B)] * 8,
        out_shape=outs,
        compiler_params=_cparams("parallel"),
        name="rwkv_prep",
    )(pb, first, vec(p['mu_b']), wl.astype(BF16), vec(p['w0']), vec(p['a0']), vec(p['k_k']),
      vec(p['k_a']), vec(p['r_k']), seg)


def _rwkv_scan_kernel(kk_ref, w_ref, b_ref, k_ref, r_ref, v_ref, s0_ref, y_ref, s_ref, *, tc):
    @pl.when(pl.program_id(1) == 0)
    def _():
        s_ref[...] = s0_ref[...]

    n_acc = 4

    def step(t0, j):
        row = pl.ds(t0 + j, 1)
        parts = [None] * n_acc
        for k in range(N_B):
            term = s_ref[k] * kk_ref[k, row, :]
            parts[k % n_acc] = term if parts[k % n_acc] is None else parts[k % n_acc] + term
        sa = (parts[0] + parts[1]) + (parts[2] + parts[3])
        v_t = v_ref[t0 + j]
        parts = [None] * n_acc
        for k in range(N_B):
            s_new = s_ref[k] * w_ref[k, row, :] - sa * b_ref[k, row, :] + v_t * k_ref[k, row, :]
            s_ref[k] = s_new
            term = s_new * r_ref[k, row, :]
            parts[k % n_acc] = term if parts[k % n_acc] is None else parts[k % n_acc] + term
        y_ref[t0 + j] = (parts[0] + parts[1]) + (parts[2] + parts[3])

    def steps(i, carry):
        t0 = pl.multiple_of(i * SUBLANES, SUBLANES)
        for j in range(SUBLANES):
            step(t0, j)
        return carry

    lax.fori_loop(0, tc // SUBLANES, steps, 0)


def _to_chain_layout(x, n, t):
    groups = n * H_B // CHAINS
    x = x.reshape(n, t, H_B, N_B).transpose(3, 1, 0, 2).reshape(N_B, t, groups, CHAINS)
    x = x.transpose(2, 0, 1, 3)
    return jnp.concatenate([x, x], axis=-1)


def _value_to_chain_layout(v, n, t):
    groups = n * H_B // CHAINS
    v = v.reshape(n, t, H_B, N_B).transpose(1, 3, 0, 2).reshape(t, 2, N_B // 2, groups, CHAINS)
    return v.transpose(3, 0, 2, 1, 4).reshape(groups, t, N_B // 2, 2 * CHAINS)


def _value_from_chain_layout(y, n, t):
    groups = n * H_B // CHAINS
    y = y.reshape(groups, t, N_B // 2, 2, CHAINS).transpose(1, 3, 2, 0, 4)
    y = y.reshape(t, N_B, n, H_B).transpose(2, 0, 3, 1)
    return y.reshape(n * t, W_B)


def _state_to_chain_layout(s, n):
    groups = n * H_B // CHAINS
    s = s.reshape(groups, CHAINS, 2, N_B // 2, N_B)
    return s.transpose(0, 4, 3, 2, 1).reshape(groups, N_B, N_B // 2, 2 * CHAINS)


def _state_from_chain_layout(s, n):
    groups = n * H_B // CHAINS
    s = s.reshape(groups, N_B, N_B // 2, 2, CHAINS).transpose(0, 4, 3, 2, 1)
    return s.reshape(n, H_B, N_B, N_B)


def _chain_in_kernel(x_ref, o_ref, z_scr, *, per_key, tt):
    n_seq = x_ref.shape[0]
    rows_out = N_B if per_key else N_B // 2
    for s in range(n_seq):
        z_scr[s] = x_ref[s].T
    for j in range(rows_out):
        rows = []
        for half in range(2):
            start = j if per_key else half * (N_B // 2) + j
            rows += [z_scr[s, pl.ds(start, H_B, stride=N_B), :] for s in range(n_seq)]
        tile = jnp.concatenate(rows, axis=0).T
        if per_key:
            o_ref[j] = tile
        else:
            o_ref[pl.ds(j, tt, stride=rows_out), :] = tile


def _chain_in_call(x, n, t, per_key):
    tt = LANES
    if per_key:
        out_spec = pl.BlockSpec((N_B, tt, 2 * CHAINS), lambda i: (0, i, 0))
        out_shape = jax.ShapeDtypeStruct((N_B, t, 2 * CHAINS), F32)
    else:
        out_spec = pl.BlockSpec((tt * (N_B // 2), 2 * CHAINS), lambda i: (i, 0))
        out_shape = jax.ShapeDtypeStruct((t * (N_B // 2), 2 * CHAINS), F32)
    out = pl.pallas_call(
        functools.partial(_chain_in_kernel, per_key=per_key, tt=tt),
        grid=(t // tt,),
        in_specs=[pl.BlockSpec((n, tt, W_B), lambda i: (0, i, 0))],
        out_specs=out_spec,
        out_shape=out_shape,
        scratch_shapes=[pltpu.VMEM((n, W_B, tt), F32)],
        compiler_params=_cparams("parallel"),
        name="chain_layout_in",
    )(x.reshape(n, t, W_B))
    return out[None] if per_key else out.reshape(1, t, N_B // 2, 2 * CHAINS)


def _chain_out_kernel(y_ref, o_ref, z_scr, *, tt):
    n_seq = o_ref.shape[0]
    rows = N_B // 2
    for j in range(rows):
        tile = y_ref[pl.ds(j, tt, stride=rows), :].T
        for half in range(2):
            for s in range(n_seq):
                r0 = (half * n_seq + s) * H_B
                z_scr[s, pl.ds(half * rows + j, H_B, stride=N_B), :] = tile[r0:r0 + H_B]
    for s in range(n_seq):
        o_ref[s] = z_scr[s].T


def _chain_out_call(y, n, t):
    tt = LANES
    rows = N_B // 2
    out = pl.pallas_call(
        functools.partial(_chain_out_kernel, tt=tt),
        grid=(t // tt,),
        in_specs=[pl.BlockSpec((tt * rows, 2 * CHAINS), lambda i: (i, 0))],
        out_specs=pl.BlockSpec((n, tt, W_B), lambda i: (0, i, 0)),
        out_shape=jax.ShapeDtypeStruct((n, t, W_B), F32),
        scratch_shapes=[pltpu.VMEM((n, W_B, tt), F32)],
        compiler_params=_cparams("parallel"),
        name="chain_layout_out",
    )(y.reshape(t * rows, 2 * CHAINS))
    return out.reshape(n * t, W_B)


def _rwkv_scan_call(kk, w, b, k, r, v, s0, n, t):
    groups = n * H_B // CHAINS
    tc = min(SCAN_TC, t)
    vec_spec = pl.BlockSpec((None, N_B, tc, 2 * CHAINS), lambda g, i: (g, 0, i, 0))
    val_spec = pl.BlockSpec((None, tc, N_B // 2, 2 * CHAINS), lambda g, i: (g, i, 0, 0))
    st_spec = pl.BlockSpec((None, N_B, N_B // 2, 2 * CHAINS), lambda g, i: (g, 0, 0, 0))
    in_kernel_layout = groups == 1 and t % LANES == 0
    if in_kernel_layout:
        vecs = [_chain_in_call(x, n, t, True) for x in (kk, w, b, k, r)]
        val = _chain_in_call(v, n, t, False)
    else:
        vecs = [_to_chain_layout(x, n, t) for x in (kk, w, b, k, r)]
        val = _value_to_chain_layout(v, n, t)
    y, s_fin = pl.pallas_call(
        functools.partial(_rwkv_scan_kernel, tc=tc),
        grid=(groups, t // tc),
        in_specs=[vec_spec] * 5 + [val_spec, st_spec],
        out_specs=[val_spec, st_spec],
        out_shape=[jax.ShapeDtypeStruct((groups, t, N_B // 2, 2 * CHAINS), F32),
                   jax.ShapeDtypeStruct((groups, N_B, N_B // 2, 2 * CHAINS), F32)],
        compiler_params=_cparams("parallel", "arbitrary"),
        name="rwkv_scan",
    )(*vecs, val, _state_to_chain_layout(s0, n))
    y = _chain_out_call(y, n, t) if in_kernel_layout else _value_from_chain_layout(y, n, t)
    return y, _state_from_chain_layout(s_fin, n)


def _rwkv_post_kernel(y_ref, bonus_ref, g_ref, lg_ref, lb_ref, seg_ref, o_ref):
    y = y_ref[...]
    seg = seg_ref[...]
    yc = y - _segsum(y, seg) * (1.0 / N_B)
    var = _segsum(yc * yc, seg) * (1.0 / N_B)
    yn = yc * lax.rsqrt(var + LNX_EPS) * lg_ref[...] + lb_ref[...]
    o_ref[...] = ((yn + bonus_ref[...]) * g_ref[...]).astype(o_ref.dtype)


def _rwkv_post_call(y, bonus, g, lg, lb, seg, grp):
    tm = grp.tm
    return pl.pallas_call(
        _rwkv_post_kernel,
        grid=(grp.tiles,),
        in_specs=[_row_spec(tm, W_B)] * 3 + [_full_spec((1, W_B)), _full_spec((1, W_B)),
                                              _full_spec((W_B, W_B))],
        out_specs=_row_spec(tm, W_B),
        out_shape=jax.ShapeDtypeStruct((grp.rows, W_B), BF16),
        compiler_params=_cparams("parallel"),
        name="rwkv_post",
    )(y, bonus, g, lg[None, :], lb[None, :], seg)


def _norm_mod(x, g, sc, sh):
    y = x * lax.rsqrt(jnp.mean(x * x, axis=-1, keepdims=True) + RMS_EPS) * g
    return y * (1.0 + sc) + sh


def _norm_specs(grp):
    return [_full_spec((1, D_MODEL)), _mod_spec(grp), _mod_spec(grp)]


def _residual_out(grp, hn_dtype):
    return dict(out_specs=[_row_spec(grp.tm, D_MODEL), _row_spec(grp.tm, D_MODEL)],
                out_shape=[jax.ShapeDtypeStruct((grp.rows, D_MODEL), F32),
                           jax.ShapeDtypeStruct((grp.rows, D_MODEL), hn_dtype)])


def _outproj_even_kernel(oa_ref, ob_ref, w_ref, x_ref, g_ref, ng_ref, nsc_ref, nsh_ref, o_ref, hn_ref):
    mix = _dot(oa_ref[...], w_ref[:W_A, :]) + _dot(ob_ref[...], w_ref[W_A:, :])
    x = x_ref[...] + g_ref[...] * mix
    o_ref[...] = x
    hn_ref[...] = _norm_mod(x, ng_ref[...], nsc_ref[...], nsh_ref[...]).astype(hn_ref.dtype)


def _outproj_even_call(oa, ob, w, x, gate, norm, grp):
    tm = grp.tm
    ng, nsc, nsh, hn_dtype = norm
    return pl.pallas_call(
        _outproj_even_kernel,
        grid=(grp.tiles,),
        in_specs=[_row_spec(tm, W_A), _row_spec(tm, W_B), _full_spec((W_A + W_B, D_MODEL)),
                  _row_spec(tm, D_MODEL), _mod_spec(grp)] + _norm_specs(grp),
        compiler_params=_cparams("parallel"),
        name="outproj_even",
        **_residual_out(grp, hn_dtype),
    )(oa, ob, w, x, gate, ng[None, :], nsc, nsh)


def _swiglu_up_kernel(a_ref, wg_ref, wu_ref, o_ref):
    a = a_ref[...]
    o_ref[...] = (_silu(_dot(a, wg_ref[...])) * _dot(a, wu_ref[...])).astype(o_ref.dtype)


def _swiglu_up_call(hn, wg, wu, grp):
    tm = grp.tm
    tn = D_FF // 2
    return pl.pallas_call(
        _swiglu_up_kernel,
        grid=(D_FF // tn, grp.tiles),
        in_specs=[pl.BlockSpec((tm, D_MODEL), lambda c, i: (i, 0)),
                  pl.BlockSpec((D_MODEL, tn), lambda c, i: (0, c)),
                  pl.BlockSpec((D_MODEL, tn), lambda c, i: (0, c))],
        out_specs=pl.BlockSpec((tm, tn), lambda c, i: (i, c)),
        out_shape=jax.ShapeDtypeStruct((grp.rows, D_FF), BF16),
        compiler_params=_cparams("parallel", "parallel"),
        name="swiglu_up",
    )(hn, wg, wu)


def _down_res_kernel(h_ref, w_ref, x_ref, g_ref, ng_ref, nsc_ref, nsh_ref, o_ref, hn_ref):
    x = x_ref[...] + g_ref[...] * _dot(h_ref[...], w_ref[...])
    o_ref[...] = x
    hn_ref[...] = _norm_mod(x, ng_ref[...], nsc_ref[...], nsh_ref[...]).astype(hn_ref.dtype)


def _down_res_call(h, w, x, gate, norm, grp):
    tm = grp.tm
    kdim = h.shape[1]
    ng, nsc, nsh, hn_dtype = norm
    return pl.pallas_call(
        _down_res_kernel,
        grid=(grp.tiles,),
        in_specs=[_row_spec(tm, kdim), _full_spec((kdim, D_MODEL)), _row_spec(tm, D_MODEL), _mod_spec(grp)]
                 + _norm_specs(grp),
        compiler_params=_cparams("parallel"),
        name="down_res",
        **_residual_out(grp, hn_dtype),
    )(h, w, x, gate, ng[None, :], nsc, nsh)


def _gelu(x):
    return 0.5 * x * (1.0 + lax.erf(x * INV_SQRT2))


def _gmlp_u_kernel(a_ref, w_ref, o_ref):
    o_ref[...] = _gelu(_dot(a_ref[...], w_ref[...]))


def _gmlp_v_kernel(a_ref, w_ref, lg_ref, lb_ref, vb_ref, vf_ref):
    v = _gelu(_dot(a_ref[...], w_ref[...]))
    vc = v - jnp.mean(v, axis=-1, keepdims=True)
    vn = vc * lax.rsqrt(jnp.mean(vc * vc, axis=-1, keepdims=True) + LN_EPS) * lg_ref[...] + lb_ref[...]
    vb_ref[...] = vn.astype(BF16)
    vf_ref[...] = vn


def _gmlp_in_call(hn, w, lg, lb, grp):
    tm, rows = grp.tm, grp.rows
    wspec = lambda half: pl.BlockSpec((D_MODEL, E_C), lambda i: (0, half))
    u = pl.pallas_call(
        _gmlp_u_kernel,
        grid=(grp.tiles,),
        in_specs=[_row_spec(tm, D_MODEL), wspec(0)],
        out_specs=_row_spec(tm, E_C),
        out_shape=jax.ShapeDtypeStruct((rows, E_C), F32),
        compiler_params=_cparams("parallel"),
        name="gmlp_u",
    )(hn, w)
    vb, vf = pl.pallas_call(
        _gmlp_v_kernel,
        grid=(grp.tiles,),
        in_specs=[_row_spec(tm, D_MODEL), wspec(1), _full_spec((1, E_C)), _full_spec((1, E_C))],
        out_specs=[_row_spec(tm, E_C), _row_spec(tm, E_C)],
        out_shape=[jax.ShapeDtypeStruct((rows, E_C), BF16), jax.ShapeDtypeStruct((rows, E_C), F32)],
        compiler_params=_cparams("parallel"),
        name="gmlp_v",
    )(hn, w, lg[None, :], lb[None, :])
    return u, vb, vf


def _gmlp_out_kernel(u_ref, v_ref, ws_ref, bs_ref, wo_ref, x_ref, g_ref, o_ref, z_scr, *, lm):
    tm = u_ref.shape[0]
    for c in range(tm // lm):
        r0 = c * lm
        for gi in range(G_C):
            c0 = gi * CG
            mixed = _dot(ws_ref[gi], v_ref[r0:r0 + lm, c0:c0 + CG]) + bs_ref[:, gi:gi + 1]
            z_scr[r0:r0 + lm, c0:c0 + CG] = (u_ref[r0:r0 + lm, c0:c0 + CG] * mixed).astype(BF16)
    o_ref[...] = x_ref[...] + g_ref[...] * _dot(z_scr[...], wo_ref[...])


def _gmlp_out_call(u, vb, ws, bs, wo, x, gate, grp):
    tm = grp.tm
    lm = ws.shape[1]
    return pl.pallas_call(
        functools.partial(_gmlp_out_kernel, lm=lm),
        grid=(grp.tiles,),
        in_specs=[_row_spec(tm, E_C), _row_spec(tm, E_C), _full_spec((G_C, lm, lm)), _full_spec((lm, G_C)),
                  _full_spec((E_C, D_MODEL)), _row_spec(tm, D_MODEL), _mod_spec(grp)],
        out_specs=_row_spec(tm, D_MODEL),
        out_shape=jax.ShapeDtypeStruct((grp.rows, D_MODEL), F32),
        scratch_shapes=[pltpu.VMEM((tm, E_C), BF16)],
        compiler_params=_cparams("parallel"),
        name="gmlp_out",
    )(u, vb, ws, bs, wo, x, gate)


def _spatial_weights(w_s, b_s, grp):
    length = min(grp.t, CHUNK)
    ws = jnp.tril(w_s[:, :length, :length])
    bs = b_s[:, :length]
    if length < CHUNK:
        reps = grp.tm // length
        ws = jnp.einsum('ab,gij->gaibj', jnp.eye(reps, dtype=ws.dtype), ws).reshape(
            G_C, reps * length, reps * length)
        bs = jnp.tile(bs, (1, reps))
    return ws.astype(BF16), bs.T


def _moe_kernel(a_ref, logit_ref, tri_ref, wg_ref, wu_ref, wd_ref, x_ref, g_ref, ng_ref, nsc_ref, nsh_ref,
                o_ref, hn_ref, gate_scr, rank_scr, gate_t_scr, rank_t_scr, acc_scr, *, tc):
    e = pl.program_id(1)
    lane = lax.broadcasted_iota(jnp.int32, (1, LANES), 1)

    @pl.when(e == 0)
    def _():
        logits = logit_ref[...]
        m1 = jnp.max(logits, axis=1, keepdims=True)
        i1 = jnp.min(jnp.where(logits == m1, lane, LANES), axis=1, keepdims=True)
        rest = jnp.where(lane == i1, -jnp.inf, logits)
        m2 = jnp.max(rest, axis=1, keepdims=True)
        i2 = jnp.min(jnp.where(rest == m2, lane, LANES), axis=1, keepdims=True)
        e2 = jnp.exp(m2 - m1)
        gate = jnp.where(lane == i1, 1.0 / (1.0 + e2), jnp.where(lane == i2, e2 / (1.0 + e2), 0.0))
        picked = gate != 0.0
        rank = _dot(tri_ref[...], jnp.where(picked, 1.0, 0.0).astype(BF16))
        rank = jnp.where(picked, rank, -1.0)
        gate_scr[...] = gate
        rank_scr[...] = rank
        gate_t_scr[...] = gate.T
        rank_t_scr[...] = rank.T
        acc_scr[...] = jnp.zeros(acc_scr.shape, F32)

    gate_col = jnp.sum(jnp.where(lane == e, gate_scr[...], 0.0), axis=1, keepdims=True)
    rank_col = jnp.sum(jnp.where(lane == e, rank_scr[...], 0.0), axis=1, keepdims=True)
    gate_row = gate_t_scr[pl.ds(e, 1), :]
    rank_row = rank_t_scr[pl.ds(e, 1), :]
    count = jnp.sum(jnp.where(rank_row >= 0.0, 1, 0))
    slot_col = lax.broadcasted_iota(jnp.int32, (tc, 1), 0).astype(F32)
    slot_row = lax.broadcasted_iota(jnp.int32, (1, tc), 1).astype(F32)

    def chunk(c, carry):
        base = (c * tc).astype(F32)
        take = rank_row == base + slot_col
        xg = _dot(jnp.where(take, 1.0, 0.0).astype(BF16), a_ref[...]).astype(BF16)
        h = (_silu(_dot(xg, wg_ref[...])) * _dot(xg, wu_ref[...])).astype(BF16)
        y = _dot(h, wd_ref[...])
        gate_c = jnp.sum(jnp.where(take, gate_row, 0.0), axis=1, keepdims=True)
        put = jnp.where(rank_col == base + slot_row, 1.0, 0.0).astype(BF16)
        acc_scr[...] += _dot(put, (gate_c * y).astype(BF16))
        return carry

    lax.fori_loop(0, (count + tc - 1) // tc, chunk, 0)

    @pl.when(e == N_EXPERTS - 1)
    def _():
        x = x_ref[...] + g_ref[...] * acc_scr[...]
        o_ref[...] = x
        hn_ref[...] = _norm_mod(x, ng_ref[...], nsc_ref[...], nsh_ref[...]).astype(hn_ref.dtype)


def _moe_call(hn, logits, wg, wu, wd, j, x, gate, norm, grp):
    tm = grp.tm
    tps = grp.tiles_per_seq
    tc = min(MOE_CHUNK, tm)
    ng, nsc, nsh, hn_dtype = norm
    mod = pl.BlockSpec((None, grp.mod_rows, D_MODEL), lambda i, e: (i // tps, 0, 0))
    tri = jnp.tril(jnp.ones((tm, tm), F32), -1).astype(BF16)
    row = lambda cols: pl.BlockSpec((tm, cols), lambda i, e: (i, 0))
    return pl.pallas_call(
        functools.partial(_moe_kernel, tc=tc),
        grid=(grp.tiles, N_EXPERTS),
        in_specs=[row(D_MODEL), row(LANES), pl.BlockSpec((tm, tm), lambda i, e: (0, 0)),
                  pl.BlockSpec((None, None, D_MODEL, D_FF_E), lambda i, e: (j, e, 0, 0)),
                  pl.BlockSpec((None, None, D_MODEL, D_FF_E), lambda i, e: (j, e, 0, 0)),
                  pl.BlockSpec((None, None, D_FF_E, D_MODEL), lambda i, e: (j, e, 0, 0)),
                  row(D_MODEL), mod,
                  pl.BlockSpec((1, D_MODEL), lambda i, e: (0, 0)), mod, mod],
        out_specs=[row(D_MODEL), row(D_MODEL)],
        out_shape=[jax.ShapeDtypeStruct((grp.rows, D_MODEL), F32),
                   jax.ShapeDtypeStruct((grp.rows, D_MODEL), hn_dtype)],
        scratch_shapes=[pltpu.VMEM((tm, LANES), F32), pltpu.VMEM((tm, LANES), F32),
                        pltpu.VMEM((LANES, tm), F32), pltpu.VMEM((LANES, tm), F32),
                        pltpu.VMEM((tm, D_MODEL), F32)],
        compiler_params=_cparams("parallel", "arbitrary"),
        name="moe_experts",
    )(hn, logits, tri, wg, wu, wd, x, gate, ng[None, :], nsc, nsh)


def _forward(x, mods, p, wb, grp, cache):
    n, t = grp.n, grp.t
    seg = jnp.kron(jnp.eye(H_B, dtype=F32), jnp.ones((N_B, N_B), F32)).astype(BF16)
    slopes = jnp.exp2(-8.0 * jnp.arange(1, H_A + 1, dtype=F32) / H_A)
    new_k, new_v, new_wkv, new_shift, new_cv = [], [], [], [], []
    mods = [[_expand_mod(m, grp) for m in layer] for layer in mods]

    def next_norm(i):
        if i + 1 < DEPTH:
            sh, sc = mods[i + 1][0], mods[i + 1][1]
            return p['norm1_g'][i + 1], sc, sh, BF16
        zero = jnp.zeros_like(mods[i][0])
        return p['final_g'], zero, zero, F32

    hn = _norm_mod_call(x, p['norm1_g'][0], mods[0][1], mods[0][0], grp)
    for i in range(DEPTH):
        j = i // 2
        sh1, sc1, g1, sh2, sc2, g2 = mods[i]
        if i % 2 == 0:
            q, k, v, kb, vb, pb = _inproj_even_call(hn, wb['w_in_e'][j], grp)
            lam_init = 0.8 - 0.6 * math.exp(-0.3 * i)
            lq = jnp.stack([p['lam_q1'][j], p['lam_q2'][j]])
            lk = jnp.stack([p['lam_k1'][j], p['lam_k2'][j]])
            if cache is None:
                g2h = jnp.tile(p['subln_g'][j], 2)[None, :]
                oa = _attn_prompt_call(q, kb, vb, slopes, lq, lk, g2h, n, t, lam_init)
                wkv0 = jnp.zeros((n, H_B, N_B, N_B), F32)
                last = pb.reshape(grp.tiles, grp.tm, B_COLS)[:, -1, :]
                first = jnp.concatenate([jnp.zeros((1, B_COLS), F32), last[:-1]], axis=0)
                starts = (jnp.arange(grp.tiles) % grp.tiles_per_seq == 0)[:, None]
                first = jnp.where(starts, 0.0, first)[:, None, :]
            else:
                cache_k, cache_v, page_table, state_wkv, state_shift = cache
                oa = _attn_sample_call(q, kb, vb, cache_k, cache_v, page_table, j, lq, lk,
                                       p['subln_g'][j][None, :], n, t, lam_init)
                wkv0 = state_wkv[j]
                first = jnp.repeat(state_shift[j], t, axis=0)[None]
            r, w, kv, vv, kk, b, bonus, g = _rwkv_prep_call(pb, first, p, j, seg, grp)
            y, s_fin = _rwkv_scan_call(kk, w, b, kv, r, vv, wkv0, n, t)
            ob = _rwkv_post_call(y, bonus, g, p['lnx_g'][j], p['lnx_b'][j], seg, grp)
            new_k.append(k.reshape(n, t, H_A, 2 * DK_A))
            new_v.append(v.reshape(n, t, H_A, DV_A))
            new_wkv.append(s_fin)
            new_shift.append(pb.reshape(n, t, B_COLS)[:, -1])
            x, hn = _outproj_even_call(oa.reshape(grp.rows, W_A), ob, wb['w_out_e'][j], x, g1,
                                       (p['norm2_g'][i], sc2, sh2, BF16), grp)
            h = _swiglu_up_call(hn, wb['ffn_gate'][j], wb['ffn_up'][j], grp)
            x, hn = _down_res_call(h, wb['ffn_down'][j], x, g2, next_norm(i), grp)
        else:
            u, vb, vf = _gmlp_in_call(hn, wb['w_in_o'][j], p['lnv_g'][j], p['lnv_b'][j], grp)
            new_cv.append(vf.reshape(n, t, E_C))
            ws, bs = _spatial_weights(p['w_s'][j], p['b_s'][j], grp)
            x = _gmlp_out_call(u, vb, ws, bs, wb['w_out_o'][j], x, g1, grp)
            hn, logits = _norm_mod_router_call(x, p['norm2_g'][i], sc2, sh2, p['router_w'][j],
                                               p['router_b'][j], grp)
            moe_grp = grp if grp.mod_rows > 1 else _Group(n, t, min(MOE_TILE, t))
            x, hn = _moe_call(hn, logits, wb['exp_gate'], wb['exp_up'], wb['exp_down'], j, x, g2,
                              next_norm(i), moe_grp)
    return (hn.reshape(n, t, D_MODEL), jnp.stack(new_k), jnp.stack(new_v), jnp.stack(new_wkv),
            jnp.stack(new_shift), jnp.stack(new_cv))


_BF16_WEIGHTS = ('w_in_e', 'w_out_e', 'ffn_gate', 'ffn_up', 'ffn_down', 'w_in_o', 'w_out_o',
                 'exp_gate', 'exp_up', 'exp_down')


def _run(x_prompt, x_sample, cache_k, cache_v, state_wkv, state_shift, page_table, c_prompt, c_sample, p):
    n_p, t_p, _ = x_prompt.shape
    n_s, t_s, _ = x_sample.shape
    grp_p = _Group(n_p, t_p, min(ROW_TILE, t_p))
    grp_s = _Group(n_s, t_s, n_s * t_s)
    wb = {name: p[name].astype(BF16) for name in _BF16_WEIGHTS}
    mod = _mod_call(jnp.concatenate([c_prompt, c_sample], axis=0), p['w_mod'], p['b_mod'])
    mods_p = [jnp.split(mod[i, :n_p], 6, axis=-1) for i in range(DEPTH)]
    mods_s = [jnp.split(mod[i, n_p:], 6, axis=-1) for i in range(DEPTH)]
    y_p, k_p, v_p, wkv_p, sh_p, _ = _forward(x_prompt.reshape(n_p * t_p, D_MODEL), mods_p, p, wb, grp_p, None)
    y_s, k_s, v_s, wkv_s, sh_s, cv_s = _forward(
        x_sample.reshape(n_s * t_s, D_MODEL), mods_s, p, wb, grp_s,
        (cache_k, cache_v, page_table, state_wkv, state_shift))
    return (y_p, y_s, k_p, v_p, wkv_p, sh_p, k_s, v_s, wkv_s, sh_s, cv_s)


def kernel(x_prompt, x_sample, cache_k, cache_v, state_wkv, state_shift, page_table, c_prompt, c_sample,
           w_mod, b_mod, norm1_g, norm2_g, final_g, w_in_e, w_out_e, lam_q1, lam_k1, lam_q2, lam_k2,
           subln_g, mu_b, w0, w_up, a0, a_up, g_up, k_k, k_a, r_k, lnx_g, lnx_b, w_in_o, lnv_g, lnv_b,
           w_s, b_s, w_out_o, ffn_gate, ffn_up, ffn_down, router_w, router_b, exp_gate, exp_up, exp_down):
    p = dict(w_mod=w_mod, b_mod=b_mod, norm1_g=norm1_g, norm2_g=norm2_g, final_g=final_g,
             w_in_e=w_in_e, w_out_e=w_out_e, lam_q1=lam_q1, lam_k1=lam_k1, lam_q2=lam_q2,
             lam_k2=lam_k2, subln_g=subln_g, mu_b=mu_b, w0=w0, w_up=w_up, a0=a0, a_up=a_up,
             g_up=g_up, k_k=k_k, k_a=k_a, r_k=r_k, lnx_g=lnx_g, lnx_b=lnx_b, w_in_o=w_in_o,
             lnv_g=lnv_g, lnv_b=lnv_b, w_s=w_s, b_s=b_s, w_out_o=w_out_o, ffn_gate=ffn_gate,
             ffn_up=ffn_up, ffn_down=ffn_down, router_w=router_w, router_b=router_b,
             exp_gate=exp_gate, exp_up=exp_up, exp_down=exp_down)
    return _run(x_prompt, x_sample, cache_k, cache_v, state_wkv, state_shift, page_table,
                c_prompt, c_sample, p)
```

```python
import functools
import math
from typing import NamedTuple

import jax
import jax.numpy as jnp
from jax import lax
from jax.experimental import pallas as pl
from jax.experimental.pallas import tpu as pltpu

F32 = jnp.float32
BF16 = jnp.bfloat16

D_MODEL = 1024
DEPTH = 4
H_A, DK_A, DV_A = 8, 32, 64
QK_A = H_A * 2 * DK_A
W_A = H_A * DV_A
A_COLS = 2 * QK_A + W_A
H_B, N_B = 8, 64
W_B = H_B * N_B
LORA_W, LORA_A, LORA_G = 64, 64, 128
LORA_COLS = LORA_W + LORA_A + LORA_G
B_COLS = 3 * W_B + LORA_COLS
IN_COLS_EVEN = A_COLS + B_COLS
CHUNK, G_C = 128, 8
E_C = 2 * D_MODEL
CG = E_C // G_C
D_FF = 2816
N_EXPERTS, D_FF_E = 8, 1408
PAGE_SIZE = 128
RMS_EPS, LN_EPS, LNX_EPS = 1e-6, 1e-5, 64e-5
NEG_INF = -1e30
ATTN_SCALE = DK_A ** -0.5
DECAY_SCALE = math.exp(-0.5)
INV_SQRT2 = 2.0 ** -0.5

LANES = 128
SUBLANES = 8
MIB = 1 << 20
VMEM_LIMIT = 56 * MIB

ROW_TILE = 512
ATTN_TQ, ATTN_TK = 256, 256
PAGES_PER_STEP = 32
MOE_TILE = 512
MOE_CHUNK = 160
SCAN_TC = 32
CHAINS = 64


class _Group(NamedTuple):
    n: int
    t: int
    tm: int

    @property
    def rows(self):
        return self.n * self.t

    @property
    def tiles(self):
        return self.rows // self.tm

    @property
    def tiles_per_seq(self):
        return max(self.t // self.tm, 1)

    @property
    def mod_rows(self):
        return 1 if self.t >= self.tm else self.tm


def _cparams(*sem):
    return pltpu.CompilerParams(dimension_semantics=sem, vmem_limit_bytes=VMEM_LIMIT)


def _row_spec(tm, cols):
    return pl.BlockSpec((tm, cols), lambda i: (i, 0))


def _full_spec(shape):
    zeros = (0,) * len(shape)
    return pl.BlockSpec(shape, lambda *_: zeros)


def _mod_spec(grp):
    tps = grp.tiles_per_seq
    return pl.BlockSpec((None, grp.mod_rows, D_MODEL), lambda i: (i // tps, 0, 0))


def _expand_mod(m, grp):
    if grp.mod_rows == 1:
        return m[:, None, :]
    return jnp.repeat(m, grp.t, axis=0)[None]


def _dot(a, b):
    return jnp.dot(a, b, preferred_element_type=F32)


def _dot_nt(a, b):
    return lax.dot_general(a, b, (((1,), (1,)), ((), ())), preferred_element_type=F32)


def _split_bf16(x):
    hi = x.astype(BF16)
    lo = (x - hi.astype(F32)).astype(BF16)
    return hi, lo


def _segsum(x, seg):
    hi, lo = _split_bf16(x)
    return _dot(hi, seg) + _dot(lo, seg)


def _silu(x):
    return x * jax.nn.sigmoid(x)


def _mod_kernel(c_ref, w_ref, b_ref, o_ref):
    a_hi, a_lo = _split_bf16(_silu(c_ref[...]))
    w_hi, w_lo = _split_bf16(w_ref[...])
    o_ref[...] = _dot(a_hi, w_hi) + _dot(a_hi, w_lo) + _dot(a_lo, w_hi) + b_ref[...]


def _mod_call(c_all, w_mod, b_mod):
    rows = c_all.shape[0]
    tn = D_MODEL
    return pl.pallas_call(
        _mod_kernel,
        grid=(DEPTH, 6 * D_MODEL // tn),
        in_specs=[
            pl.BlockSpec((rows, D_MODEL), lambda i, j: (0, 0)),
            pl.BlockSpec((None, D_MODEL, tn), lambda i, j: (i, 0, j)),
            pl.BlockSpec((None, 1, tn), lambda i, j: (i, 0, j)),
        ],
        out_specs=pl.BlockSpec((None, rows, tn), lambda i, j: (i, 0, j)),
        out_shape=jax.ShapeDtypeStruct((DEPTH, rows, 6 * D_MODEL), F32),
        compiler_params=_cparams("parallel", "parallel"),
        name="adaln_mod",
    )(c_all, w_mod, b_mod[:, None, :])


def _norm_mod_kernel(x_ref, g_ref, sc_ref, sh_ref, o_ref):
    x = x_ref[...]
    y = x * lax.rsqrt(jnp.mean(x * x, axis=-1, keepdims=True) + RMS_EPS) * g_ref[...]
    o_ref[...] = (y * (1.0 + sc_ref[...]) + sh_ref[...]).astype(o_ref.dtype)


def _norm_mod_call(x, g, sc, sh, grp):
    return pl.pallas_call(
        _norm_mod_kernel,
        grid=(grp.tiles,),
        in_specs=[_row_spec(grp.tm, D_MODEL), _full_spec((1, D_MODEL)), _mod_spec(grp), _mod_spec(grp)],
        out_specs=_row_spec(grp.tm, D_MODEL),
        out_shape=jax.ShapeDtypeStruct((grp.rows, D_MODEL), BF16),
        compiler_params=_cparams("parallel"),
        name="norm_mod",
    )(x, g[None, :], sc, sh)


def _norm_mod_router_kernel(x_ref, g_ref, sc_ref, sh_ref, wr_ref, br_ref, o_ref, logit_ref):
    x = x_ref[...]
    y = x * lax.rsqrt(jnp.mean(x * x, axis=-1, keepdims=True) + RMS_EPS) * g_ref[...]
    hn = y * (1.0 + sc_ref[...]) + sh_ref[...]
    o_ref[...] = hn.astype(o_ref.dtype)
    h_hi, h_lo = _split_bf16(hn)
    w_hi, w_lo = _split_bf16(wr_ref[...])
    logit_ref[...] = _dot(h_hi, w_hi) + _dot(h_hi, w_lo) + _dot(h_lo, w_hi) + br_ref[...]


def _norm_mod_router_call(x, g, sc, sh, wr, br, grp):
    wr_pad = jnp.zeros((D_MODEL, LANES), F32).at[:, :N_EXPERTS].set(wr)
    br_pad = jnp.full((1, LANES), -jnp.inf, F32).at[0, :N_EXPERTS].set(br)
    return pl.pallas_call(
        _norm_mod_router_kernel,
        grid=(grp.tiles,),
        in_specs=[_row_spec(grp.tm, D_MODEL), _full_spec((1, D_MODEL)), _mod_spec(grp), _mod_spec(grp),
                  _full_spec((D_MODEL, LANES)), _full_spec((1, LANES))],
        out_specs=[_row_spec(grp.tm, D_MODEL), _row_spec(grp.tm, LANES)],
        out_shape=[jax.ShapeDtypeStruct((grp.rows, D_MODEL), BF16),
                   jax.ShapeDtypeStruct((grp.rows, LANES), F32)],
        compiler_params=_cparams("parallel"),
        name="norm_mod_router",
    )(x, g[None, :], sc, sh, wr_pad, br_pad)


def _inproj_even_kernel(a_ref, w_ref, q_ref, k_ref, v_ref, kb_ref, vb_ref, pb_ref):
    a = a_ref[...]
    q_ref[...] = (_dot(a, w_ref[:, :QK_A]) * ATTN_SCALE).astype(BF16)
    k = _dot(a, w_ref[:, QK_A:2 * QK_A])
    k_ref[...] = k
    kb_ref[...] = k.astype(BF16)
    v = _dot(a, w_ref[:, 2 * QK_A:A_COLS])
    v_ref[...] = v
    vb_ref[...] = v.astype(BF16)
    half = B_COLS // 2
    pb_ref[:, :half] = _dot(a, w_ref[:, A_COLS:A_COLS + half])
    pb_ref[:, half:] = _dot(a, w_ref[:, A_COLS + half:])


def _inproj_even_call(hn, w, grp):
    tm, rows = grp.tm, grp.rows
    sds = jax.ShapeDtypeStruct
    return pl.pallas_call(
        _inproj_even_kernel,
        grid=(grp.tiles,),
        in_specs=[_row_spec(tm, D_MODEL), _full_spec((D_MODEL, IN_COLS_EVEN))],
        out_specs=[_row_spec(tm, QK_A), _row_spec(tm, QK_A), _row_spec(tm, W_A),
                   _row_spec(tm, QK_A), _row_spec(tm, W_A), _row_spec(tm, B_COLS)],
        out_shape=[sds((rows, QK_A), BF16), sds((rows, QK_A), F32), sds((rows, W_A), F32),
                   sds((rows, QK_A), BF16), sds((rows, W_A), BF16), sds((rows, B_COLS), F32)],
        compiler_params=_cparams("parallel"),
        name="inproj_even",
    )(hn, w)


def _lambda(lq_ref, lk_ref, lam_init):
    t = jnp.sum(lq_ref[...] * lk_ref[...], axis=1, keepdims=True)
    e = jnp.exp(t)
    return e[0:1] - e[1:2] + lam_init


def _attn_prompt_kernel(lq_ref, lk_ref, g_ref, qf_ref, q_ref, k_ref, kf_ref, vt_ref, o_ref,
                        m_scr, l_scr, acc_scr, *, tq, tk, lam_init):
    hp = pl.program_id(1)
    qi = pl.program_id(2)
    lane = lax.broadcasted_iota(jnp.int32, (1, LANES), 1)
    q = q_ref[...]
    q_aug = []
    for i in range(4):
        q_map = jnp.where(lane // DK_A == i, q, jnp.zeros_like(q))
        feat = jnp.broadcast_to(qf_ref[pl.ds(2 * hp + i // 2, 1), :], (tq, LANES)).astype(BF16)
        q_aug.append(jnp.concatenate([q_map, feat], axis=1))
    m_scr[...] = jnp.full(m_scr.shape, NEG_INF, F32)
    l_scr[...] = jnp.zeros(l_scr.shape, F32)
    acc_scr[...] = jnp.zeros(acc_scr.shape, F32)
    row = lax.broadcasted_iota(jnp.int32, (tk, 1), 0)
    col = lax.broadcasted_iota(jnp.int32, (1, tq), 1)
    q0 = qi * tq

    def scores(kt):
        k0 = pl.multiple_of(kt * tk, tk)
        k_aug = jnp.concatenate([k_ref[pl.ds(k0, tk), :], kf_ref[pl.ds(k0, tk), :]], axis=1)
        return tuple(_dot_nt(k_aug, q_aug[i]) for i in range(4))

    def softmax_pv(kt, s_maps, masked):
        vt = vt_ref[kt]
        if masked:
            keep = (kt * tk + row) <= (q0 + col)
        for i in range(4):
            h = i // 2
            s = jnp.where(keep, s_maps[i], NEG_INF) if masked else s_maps[i]
            m_old = m_scr[i]
            m_new = jnp.maximum(m_old, jnp.max(s, axis=0, keepdims=True))
            alpha = jnp.exp(m_old - m_new)
            p = jnp.exp(s - m_new)
            l_scr[i] = alpha * l_scr[i] + jnp.sum(p, axis=0, keepdims=True)
            acc_scr[i] = alpha * acc_scr[i] + _dot(vt[h * DV_A:(h + 1) * DV_A, :], p.astype(BF16))
            m_scr[i] = m_new

    n_full = q0 // tk

    def body(kt, s_maps):
        s_next = scores(kt + 1)
        softmax_pv(kt, s_maps, False)
        return s_next

    s_last = lax.fori_loop(0, n_full, body, scores(0))
    softmax_pv(n_full, s_last, True)

    lam = _lambda(lq_ref, lk_ref, lam_init)
    heads = []
    for h in range(2):
        oh = acc_scr[2 * h] / l_scr[2 * h] - lam * (acc_scr[2 * h + 1] / l_scr[2 * h + 1])
        ms = jnp.mean(oh * oh, axis=0, keepdims=True)
        heads.append(oh * lax.rsqrt(ms + RMS_EPS))
    o = jnp.concatenate(heads, axis=0).T
    o_ref[...] = (o * g_ref[...] * (1.0 - lam_init)).astype(o_ref.dtype)


def _attn_prompt_call(q, k, v, slopes, lq, lk, g2, n, t, lam_init):
    tq, tk = ATTN_TQ, min(ATTN_TK, t)
    tq = min(tq, tk)
    kern = functools.partial(_attn_prompt_kernel, tq=tq, tk=tk, lam_init=lam_init)
    pos = jnp.arange(t)
    k_feat = jnp.zeros((t, LANES), F32).at[:, 0].set(pos // 256).at[:, 1].set(pos % 256).astype(BF16)
    q_feat = jnp.zeros((H_A, LANES), F32).at[:, 0].set(256.0 * slopes).at[:, 1].set(slopes)
    v_t = v.reshape(n, t // tk, tk, W_A).transpose(0, 1, 3, 2)
    qspec = pl.BlockSpec((None, tq, LANES), lambda b, hp, qi: (b, qi, hp))
    return pl.pallas_call(
        kern,
        grid=(n, H_A // 2, t // tq),
        in_specs=[_full_spec((2, DK_A)), _full_spec((2, DK_A)), _full_spec((1, LANES)),
                  _full_spec((H_A, LANES)), qspec,
                  pl.BlockSpec((None, t, LANES), lambda b, hp, qi: (b, 0, hp)),
                  _full_spec((t, LANES)),
                  pl.BlockSpec((None, t // tk, LANES, tk), lambda b, hp, qi: (b, 0, hp, 0))],
        out_specs=qspec,
        out_shape=jax.ShapeDtypeStruct((n, t, W_A), BF16),
        scratch_shapes=[pltpu.VMEM((4, 1, tq), F32), pltpu.VMEM((4, 1, tq), F32),
                        pltpu.VMEM((4, DV_A, tq), F32)],
        compiler_params=_cparams("parallel", "parallel", "parallel"),
        name="diff_attn_prompt",
    )(lq, lk, g2, q_feat, q.reshape(n, t, QK_A), k.reshape(n, t, QK_A), k_feat, v_t)


def _attn_sample_kernel(pt_ref, lq_ref, lk_ref, g_ref, q_ref, kn_ref, vn_ref, *rest,
                        pages, steps, past_len, t_new, lam_init):
    k_pages, v_pages = rest[:pages], rest[pages:2 * pages]
    o_ref, m_scr, l_scr, acc_scr = rest[2 * pages:]
    step = pl.program_id(1)
    hr = 2 * t_new
    row = lax.broadcasted_iota(jnp.int32, (H_A * hr, 1), 0)
    slope = jnp.exp2(-(row // hr + 1).astype(F32))

    @pl.when(step == 0)
    def _():
        m_scr[...] = jnp.full(m_scr.shape, NEG_INF, F32)
        l_scr[...] = jnp.zeros(l_scr.shape, F32)
        acc_scr[...] = jnp.zeros(acc_scr.shape, F32)

    def update(scores, values_t, keep):
        s = jnp.concatenate(scores, axis=0)
        if keep is not None:
            s = jnp.where(keep, s, NEG_INF)
        m_old = m_scr[...]
        m_new = jnp.maximum(m_old, jnp.max(s, axis=1, keepdims=True))
        alpha = jnp.exp(m_old - m_new)
        p = jnp.exp(s - m_new)
        l_scr[...] = alpha * l_scr[...] + jnp.sum(p, axis=1, keepdims=True)
        p = p.astype(BF16)
        pv = jnp.concatenate([_dot_nt(p[h * hr:(h + 1) * hr], values_t[h]) for h in range(H_A)], axis=0)
        acc_scr[...] = alpha * acc_scr[...] + pv
        m_scr[...] = m_new

    col = lax.broadcasted_iota(jnp.int32, (1, pages * PAGE_SIZE), 1)
    k_rel = (step * (pages * PAGE_SIZE) - past_len + col).astype(F32)
    scores, values_t = [], []
    for h in range(H_A):
        k_t = jnp.concatenate([kp[h] for kp in k_pages], axis=1).astype(BF16)
        values_t.append(jnp.concatenate([vp[h] for vp in v_pages], axis=1).astype(BF16))
        scores.append(_dot(q_ref[h], k_t))
    update([s + slope[h * hr:(h + 1) * hr] * k_rel for h, s in enumerate(scores)], values_t, None)

    @pl.when(step == steps - 1)
    def _():
        new_col = lax.broadcasted_iota(jnp.int32, (1, PAGE_SIZE), 1)
        keep = new_col <= row % t_new
        bias = slope * new_col.astype(F32)
        update([_dot(q_ref[h], kn_ref[h]) + bias[h * hr:(h + 1) * hr] for h in range(H_A)],
               [vn_ref[h] for h in range(H_A)], keep)
        lam = _lambda(lq_ref, lk_ref, lam_init)
        a = acc_scr[...] / l_scr[...]
        for h in range(H_A):
            oh = a[h * hr:h * hr + t_new] - lam * a[h * hr + t_new:(h + 1) * hr]
            ms = jnp.mean(oh * oh, axis=1, keepdims=True)
            o_ref[h] = oh * lax.rsqrt(ms + RMS_EPS) * g_ref[...] * (1.0 - lam_init)


def _attn_sample_call(q, k_new, v_new, cache_k, cache_v, page_table, j, lq, lk, g1, n, t, lam_init):
    n_pages = page_table.shape[1]
    pages = PAGES_PER_STEP
    steps = n_pages // pages
    n_pool = cache_k.shape[1]
    ck = cache_k.transpose(0, 1, 3, 4, 2)
    cv = cache_v.transpose(0, 1, 3, 4, 2)
    kern = functools.partial(_attn_sample_kernel, pages=pages, steps=steps,
                             past_len=n_pages * PAGE_SIZE, t_new=t, lam_init=lam_init)
    qh = jnp.einsum('nthmd,km->nhktmd', q.reshape(n, t, H_A, 2, DK_A), jnp.eye(2, dtype=q.dtype))
    qh = qh.reshape(n, H_A, 2 * t, 2 * DK_A)

    def new_page(x):
        x = x.reshape(n, t, H_A, DV_A).transpose(0, 2, 3, 1)
        return jnp.pad(x, ((0, 0), (0, 0), (0, 0), (0, PAGE_SIZE - t)))

    def page_spec(i):
        return pl.BlockSpec((None, None, H_A, DV_A, PAGE_SIZE),
                            lambda b, s, pt: (j, pt[b, s * pages + i], 0, 0, 0))

    small = lambda shape: pl.BlockSpec(shape, lambda b, s, pt: (0,) * len(shape))
    head_spec = lambda r: pl.BlockSpec((None, H_A, r, DV_A), lambda b, s, pt: (b, 0, 0, 0))
    new_spec = pl.BlockSpec((None, H_A, DV_A, PAGE_SIZE), lambda b, s, pt: (b, 0, 0, 0))
    grid_spec = pltpu.PrefetchScalarGridSpec(
        num_scalar_prefetch=1,
        grid=(n, steps),
        in_specs=[small((2, DK_A)), small((2, DK_A)), small((1, DV_A)),
                  head_spec(2 * t), new_spec, new_spec]
                 + [page_spec(i) for i in range(pages)] * 2,
        out_specs=head_spec(t),
        scratch_shapes=[pltpu.VMEM((H_A * 2 * t, 1), F32), pltpu.VMEM((H_A * 2 * t, 1), F32),
                        pltpu.VMEM((H_A * 2 * t, DV_A), F32)],
    )
    o = pl.pallas_call(
        kern,
        grid_spec=grid_spec,
        out_shape=jax.ShapeDtypeStruct((n, H_A, t, DV_A), F32),
        compiler_params=_cparams("parallel", "arbitrary"),
        name="diff_attn_paged",
    )(page_table, lq, lk, g1, qh, new_page(k_new), new_page(v_new), *([ck] * pages), *([cv] * pages))
    return o.transpose(0, 2, 1, 3).reshape(n * t, W_A).astype(BF16)


def _rwkv_prep_kernel(pb_ref, first_ref, mu_ref, wl_ref, w0_ref, a0_ref, kk_w_ref, ka_w_ref, rk_ref,
                      seg_ref, r_o, w_o, k_o, v_o, kk_o, b_o, bonus_o, g_o, *, period):
    pb = pb_ref[...]
    tm = pb.shape[0]
    row = lax.broadcasted_iota(jnp.int32, (tm, 1), 0)
    prev = jnp.where(row % period == 0, first_ref[...], pltpu.roll(pb, 1, 0))
    xb = pb + (prev - pb) * mu_ref[...]
    r = xb[:, :W_B]
    kb = xb[:, W_B:2 * W_B]
    vb = xb[:, 2 * W_B:3 * W_B]
    z = xb[:, 3 * W_B:]
    ll = lax.broadcasted_iota(jnp.int32, (1, LORA_COLS), 1)
    zz = jnp.where(ll < LORA_W, jnp.tanh(z), jnp.where(ll < LORA_W + LORA_A, z, jax.nn.sigmoid(z)))
    lo = _dot(zz.astype(BF16), wl_ref[...])
    decay = jnp.exp(-DECAY_SCALE * jax.nn.sigmoid(w0_ref[...] + lo[:, :W_B]))
    a = jax.nn.sigmoid(a0_ref[...] + lo[:, W_B:2 * W_B])
    seg = seg_ref[...]
    kk = kb * kk_w_ref[...]
    kk = kk / jnp.maximum(jnp.sqrt(_segsum(kk * kk, seg)), 1e-12)
    k_adj = kb * (1.0 + (a - 1.0) * ka_w_ref[...])
    r_o[...] = r
    w_o[...] = decay
    k_o[...] = k_adj
    v_o[...] = vb
    kk_o[...] = kk
    b_o[...] = kk * a
    bonus_o[...] = _segsum(r * k_adj * rk_ref[...], seg) * vb
    g_o[...] = lo[:, 2 * W_B:]


def _rwkv_prep_call(pb, first, p, j, seg, grp):
    tm, rows = grp.tm, grp.rows
    period = tm if grp.mod_rows == 1 else grp.t
    first_rows = first.shape[1]
    vec = lambda x: x[j].reshape(1, -1)
    wl = jnp.zeros((LORA_COLS, 3 * W_B), F32)
    wl = wl.at[:LORA_W, :W_B].set(p['w_up'][j])
    wl = wl.at[LORA_W:LORA_W + LORA_A, W_B:2 * W_B].set(p['a_up'][j])
    wl = wl.at[LORA_W + LORA_A:, 2 * W_B:].set(p['g_up'][j])
    outs = [jax.ShapeDtypeStruct((rows, W_B), F32)] * 8
    return pl.pallas_call(
        functools.partial(_rwkv_prep_kernel, period=period),
        grid=(grp.tiles,),
        in_specs=[_row_spec(tm, B_COLS),
                  pl.BlockSpec((None, first_rows, B_COLS), lambda i: (i, 0, 0)),
                  _full_spec((1, B_COLS)), _full_spec((LORA_COLS, 3 * W_B)),
                  _full_spec((1, W_B)), _full_spec((1, W_B)), _full_spec((1, W_B)),
                  _full_spec((1, W_B)), _full_spec((1, W_B)), _full_spec((W_B, W_B))],
        out_specs=[_row_spec(tm, W_B)] * 8,
        out_shape=outs,
        compiler_params=_cparams("parallel"),
        name="rwkv_prep",
    )(pb, first, vec(p['mu_b']), wl.astype(BF16), vec(p['w0']), vec(p['a0']), vec(p['k_k']),
      vec(p['k_a']), vec(p['r_k']), seg)


def _rwkv_scan_kernel(kk_ref, w_ref, b_ref, k_ref, r_ref, v_ref, s0_ref, y_ref, s_ref, *, tc):
    @pl.when(pl.program_id(1) == 0)
    def _():
        s_ref[...] = s0_ref[...]

    n_acc = 4

    def step(t0, j):
        row = pl.ds(t0 + j, 1)
        parts = [None] * n_acc
        for k in range(N_B):
            term = s_ref[k] * kk_ref[k, row, :]
            parts[k % n_acc] = term if parts[k % n_acc] is None else parts[k % n_acc] + term
        sa = (parts[0] + parts[1]) + (parts[2] + parts[3])
        v_t = v_ref[t0 + j]
        parts = [None] * n_acc
        for k in range(N_B):
            s_new = s_ref[k] * w_ref[k, row, :] - sa * b_ref[k, row, :] + v_t * k_ref[k, row, :]
            s_ref[k] = s_new
            term = s_new * r_ref[k, row, :]
            parts[k % n_acc] = term if parts[k % n_acc] is None else parts[k % n_acc] + term
        y_ref[t0 + j] = (parts[0] + parts[1]) + (parts[2] + parts[3])

    def steps(i, carry):
        t0 = pl.multiple_of(i * SUBLANES, SUBLANES)
        for j in range(SUBLANES):
            step(t0, j)
        return carry

    lax.fori_loop(0, tc // SUBLANES, steps, 0)


def _to_chain_layout(x, n, t):
    groups = n * H_B // CHAINS
    x = x.reshape(n, t, H_B, N_B).transpose(3, 1, 0, 2).reshape(N_B, t, groups, CHAINS)
    x = x.transpose(2, 0, 1, 3)
    return jnp.concatenate([x, x], axis=-1)


def _value_to_chain_layout(v, n, t):
    groups = n * H_B // CHAINS
    v = v.reshape(n, t, H_B, N_B).transpose(1, 3, 0, 2).reshape(t, 2, N_B // 2, groups, CHAINS)
    return v.transpose(3, 0, 2, 1, 4).reshape(groups, t, N_B // 2, 2 * CHAINS)


def _value_from_chain_layout(y, n, t):
    groups = n * H_B // CHAINS
    y = y.reshape(groups, t, N_B // 2, 2, CHAINS).transpose(1, 3, 2, 0, 4)
    y = y.reshape(t, N_B, n, H_B).transpose(2, 0, 3, 1)
    return y.reshape(n * t, W_B)


def _state_to_chain_layout(s, n):
    groups = n * H_B // CHAINS
    s = s.reshape(groups, CHAINS, 2, N_B // 2, N_B)
    return s.transpose(0, 4, 3, 2, 1).reshape(groups, N_B, N_B // 2, 2 * CHAINS)


def _state_from_chain_layout(s, n):
    groups = n * H_B // CHAINS
    s = s.reshape(groups, N_B, N_B // 2, 2, CHAINS).transpose(0, 4, 3, 2, 1)
    return s.reshape(n, H_B, N_B, N_B)


def _chain_in_kernel(x_ref, o_ref, z_scr, *, per_key, tt):
    n_seq = x_ref.shape[0]
    rows_out = N_B if per_key else N_B // 2
    for s in range(n_seq):
        z_scr[s] = x_ref[s].T
    for j in range(rows_out):
        rows = []
        for half in range(2):
            start = j if per_key else half * (N_B // 2) + j
            rows += [z_scr[s, pl.ds(start, H_B, stride=N_B), :] for s in range(n_seq)]
        tile = jnp.concatenate(rows, axis=0).T
        if per_key:
            o_ref[j] = tile
        else:
            o_ref[pl.ds(j, tt, stride=rows_out), :] = tile


def _chain_in_call(x, n, t, per_key):
    tt = LANES
    if per_key:
        out_spec = pl.BlockSpec((N_B, tt, 2 * CHAINS), lambda i: (0, i, 0))
        out_shape = jax.ShapeDtypeStruct((N_B, t, 2 * CHAINS), F32)
    else:
        out_spec = pl.BlockSpec((tt * (N_B // 2), 2 * CHAINS), lambda i: (i, 0))
        out_shape = jax.ShapeDtypeStruct((t * (N_B // 2), 2 * CHAINS), F32)
    out = pl.pallas_call(
        functools.partial(_chain_in_kernel, per_key=per_key, tt=tt),
        grid=(t // tt,),
        in_specs=[pl.BlockSpec((n, tt, W_B), lambda i: (0, i, 0))],
        out_specs=out_spec,
        out_shape=out_shape,
        scratch_shapes=[pltpu.VMEM((n, W_B, tt), F32)],
        compiler_params=_cparams("parallel"),
        name="chain_layout_in",
    )(x.reshape(n, t, W_B))
    return out[None] if per_key else out.reshape(1, t, N_B // 2, 2 * CHAINS)


def _chain_out_kernel(y_ref, o_ref, z_scr, *, tt):
    n_seq = o_ref.shape[0]
    rows = N_B // 2
    for j in range(rows):
        tile = y_ref[pl.ds(j, tt, stride=rows), :].T
        for half in range(2):
            for s in range(n_seq):
                r0 = (half * n_seq + s) * H_B
                z_scr[s, pl.ds(half * rows + j, H_B, stride=N_B), :] = tile[r0:r0 + H_B]
    for s in range(n_seq):
        o_ref[s] = z_scr[s].T


def _chain_out_call(y, n, t):
    tt = LANES
    rows = N_B // 2
    out = pl.pallas_call(
        functools.partial(_chain_out_kernel, tt=tt),
        grid=(t // tt,),
        in_specs=[pl.BlockSpec((tt * rows, 2 * CHAINS), lambda i: (i, 0))],
        out_specs=pl.BlockSpec((n, tt, W_B), lambda i: (0, i, 0)),
        out_shape=jax.ShapeDtypeStruct((n, t, W_B), F32),
        scratch_shapes=[pltpu.VMEM((n, W_B, tt), F32)],
        compiler_params=_cparams("parallel"),
        name="chain_layout_out",
    )(y.reshape(t * rows, 2 * CHAINS))
    return out.reshape(n * t, W_B)


def _rwkv_scan_call(kk, w, b, k, r, v, s0, n, t):
    groups = n * H_B // CHAINS
    tc = min(SCAN_TC, t)
    vec_spec = pl.BlockSpec((None, N_B, tc, 2 * CHAINS), lambda g, i: (g, 0, i, 0))
    val_spec = pl.BlockSpec((None, tc, N_B // 2, 2 * CHAINS), lambda g, i: (g, i, 0, 0))
    st_spec = pl.BlockSpec((None, N_B, N_B // 2, 2 * CHAINS), lambda g, i: (g, 0, 0, 0))
    in_kernel_layout = groups == 1 and t % LANES == 0
    if in_kernel_layout:
        vecs = [_chain_in_call(x, n, t, True) for x in (kk, w, b, k, r)]
        val = _chain_in_call(v, n, t, False)
    else:
        vecs = [_to_chain_layout(x, n, t) for x in (kk, w, b, k, r)]
        val = _value_to_chain_layout(v, n, t)
    y, s_fin = pl.pallas_call(
        functools.partial(_rwkv_scan_kernel, tc=tc),
        grid=(groups, t // tc),
        in_specs=[vec_spec] * 5 + [val_spec, st_spec],
        out_specs=[val_spec, st_spec],
        out_shape=[jax.ShapeDtypeStruct((groups, t, N_B // 2, 2 * CHAINS), F32),
                   jax.ShapeDtypeStruct((groups, N_B, N_B // 2, 2 * CHAINS), F32)],
        compiler_params=_cparams("parallel", "arbitrary"),
        name="rwkv_scan",
    )(*vecs, val, _state_to_chain_layout(s0, n))
    y = _chain_out_call(y, n, t) if in_kernel_layout else _value_from_chain_layout(y, n, t)
    return y, _state_from_chain_layout(s_fin, n)


def _rwkv_post_kernel(y_ref, bonus_ref, g_ref, lg_ref, lb_ref, seg_ref, o_ref):
    y = y_ref[...]
    seg = seg_ref[...]
    yc = y - _segsum(y, seg) * (1.0 / N_B)
    var = _segsum(yc * yc, seg) * (1.0 / N_B)
    yn = yc * lax.rsqrt(var + LNX_EPS) * lg_ref[...] + lb_ref[...]
    o_ref[...] = ((yn + bonus_ref[...]) * g_ref[...]).astype(o_ref.dtype)


def _rwkv_post_call(y, bonus, g, lg, lb, seg, grp):
    tm = grp.tm
    return pl.pallas_call(
        _rwkv_post_kernel,
        grid=(grp.tiles,),
        in_specs=[_row_spec(tm, W_B)] * 3 + [_full_spec((1, W_B)), _full_spec((1, W_B)),
                                              _full_spec((W_B, W_B))],
        out_specs=_row_spec(tm, W_B),
        out_shape=jax.ShapeDtypeStruct((grp.rows, W_B), BF16),
        compiler_params=_cparams("parallel"),
        name="rwkv_post",
    )(y, bonus, g, lg[None, :], lb[None, :], seg)


def _norm_mod(x, g, sc, sh):
    y = x * lax.rsqrt(jnp.mean(x * x, axis=-1, keepdims=True) + RMS_EPS) * g
    return y * (1.0 + sc) + sh


def _norm_specs(grp):
    return [_full_spec((1, D_MODEL)), _mod_spec(grp), _mod_spec(grp)]


def _residual_out(grp, hn_dtype):
    return dict(out_specs=[_row_spec(grp.tm, D_MODEL), _row_spec(grp.tm, D_MODEL)],
                out_shape=[jax.ShapeDtypeStruct((grp.rows, D_MODEL), F32),
                           jax.ShapeDtypeStruct((grp.rows, D_MODEL), hn_dtype)])


def _outproj_even_kernel(oa_ref, ob_ref, w_ref, x_ref, g_ref, ng_ref, nsc_ref, nsh_ref, o_ref, hn_ref):
    mix = _dot(oa_ref[...], w_ref[:W_A, :]) + _dot(ob_ref[...], w_ref[W_A:, :])
    x = x_ref[...] + g_ref[...] * mix
    o_ref[...] = x
    hn_ref[...] = _norm_mod(x, ng_ref[...], nsc_ref[...], nsh_ref[...]).astype(hn_ref.dtype)


def _outproj_even_call(oa, ob, w, x, gate, norm, grp):
    tm = grp.tm
    ng, nsc, nsh, hn_dtype = norm
    return pl.pallas_call(
        _outproj_even_kernel,
        grid=(grp.tiles,),
        in_specs=[_row_spec(tm, W_A), _row_spec(tm, W_B), _full_spec((W_A + W_B, D_MODEL)),
                  _row_spec(tm, D_MODEL), _mod_spec(grp)] + _norm_specs(grp),
        compiler_params=_cparams("parallel"),
        name="outproj_even",
        **_residual_out(grp, hn_dtype),
    )(oa, ob, w, x, gate, ng[None, :], nsc, nsh)


def _swiglu_up_kernel(a_ref, wg_ref, wu_ref, o_ref):
    a = a_ref[...]
    o_ref[...] = (_silu(_dot(a, wg_ref[...])) * _dot(a, wu_ref[...])).astype(o_ref.dtype)


def _swiglu_up_call(hn, wg, wu, grp):
    tm = grp.tm
    tn = D_FF // 2
    return pl.pallas_call(
        _swiglu_up_kernel,
        grid=(D_FF // tn, grp.tiles),
        in_specs=[pl.BlockSpec((tm, D_MODEL), lambda c, i: (i, 0)),
                  pl.BlockSpec((D_MODEL, tn), lambda c, i: (0, c)),
                  pl.BlockSpec((D_MODEL, tn), lambda c, i: (0, c))],
        out_specs=pl.BlockSpec((tm, tn), lambda c, i: (i, c)),
        out_shape=jax.ShapeDtypeStruct((grp.rows, D_FF), BF16),
        compiler_params=_cparams("parallel", "parallel"),
        name="swiglu_up",
    )(hn, wg, wu)


def _down_res_kernel(h_ref, w_ref, x_ref, g_ref, ng_ref, nsc_ref, nsh_ref, o_ref, hn_ref):
    x = x_ref[...] + g_ref[...] * _dot(h_ref[...], w_ref[...])
    o_ref[...] = x
    hn_ref[...] = _norm_mod(x, ng_ref[...], nsc_ref[...], nsh_ref[...]).astype(hn_ref.dtype)


def _down_res_call(h, w, x, gate, norm, grp):
    tm = grp.tm
    kdim = h.shape[1]
    ng, nsc, nsh, hn_dtype = norm
    return pl.pallas_call(
        _down_res_kernel,
        grid=(grp.tiles,),
        in_specs=[_row_spec(tm, kdim), _full_spec((kdim, D_MODEL)), _row_spec(tm, D_MODEL), _mod_spec(grp)]
                 + _norm_specs(grp),
        compiler_params=_cparams("parallel"),
        name="down_res",
        **_residual_out(grp, hn_dtype),
    )(h, w, x, gate, ng[None, :], nsc, nsh)


def _gelu(x):
    return 0.5 * x * (1.0 + lax.erf(x * INV_SQRT2))


def _gmlp_v_kernel(a_ref, w_ref, lg_ref, lb_ref, vb_ref, *maybe_vf_ref):
    v = _gelu(_dot(a_ref[...], w_ref[...]))
    vc = v - jnp.mean(v, axis=-1, keepdims=True)
    vn = vc * lax.rsqrt(jnp.mean(vc * vc, axis=-1, keepdims=True) + LN_EPS) * lg_ref[...] + lb_ref[...]
    vb_ref[...] = vn.astype(BF16)
    for vf_ref in maybe_vf_ref:
        vf_ref[...] = vn


def _gmlp_v_call(hn, w, lg, lb, grp, want_f32):
    tm, rows = grp.tm, grp.rows
    n_out = 2 if want_f32 else 1
    dtypes = (BF16, F32)[:n_out]
    outs = pl.pallas_call(
        _gmlp_v_kernel,
        grid=(grp.tiles,),
        in_specs=[_row_spec(tm, D_MODEL), pl.BlockSpec((D_MODEL, E_C), lambda i: (0, 1)),
                  _full_spec((1, E_C)), _full_spec((1, E_C))],
        out_specs=[_row_spec(tm, E_C)] * n_out,
        out_shape=[jax.ShapeDtypeStruct((rows, E_C), d) for d in dtypes],
        compiler_params=_cparams("parallel"),
        name="gmlp_v",
    )(hn, w, lg[None, :], lb[None, :])
    return outs if want_f32 else (outs[0], None)


def _gmlp_out_kernel(a_ref, wu_ref, v_ref, ws_ref, bs_ref, wo_ref, x_ref, g_ref, o_ref, u_scr, z_scr, *, lm):
    tm = a_ref.shape[0]
    u_scr[...] = _gelu(_dot(a_ref[...], wu_ref[...]))
    for c in range(tm // lm):
        r0 = c * lm
        for gi in range(G_C):
            c0 = gi * CG
            mixed = _dot(ws_ref[gi], v_ref[r0:r0 + lm, c0:c0 + CG]) + bs_ref[:, gi:gi + 1]
            z_scr[r0:r0 + lm, c0:c0 + CG] = (u_scr[r0:r0 + lm, c0:c0 + CG] * mixed).astype(BF16)
    o_ref[...] = x_ref[...] + g_ref[...] * _dot(z_scr[...], wo_ref[...])


def _gmlp_out_call(hn, w_in, vb, ws, bs, wo, x, gate, grp):
    tm = grp.tm
    lm = ws.shape[1]
    return pl.pallas_call(
        functools.partial(_gmlp_out_kernel, lm=lm),
        grid=(grp.tiles,),
        in_specs=[_row_spec(tm, D_MODEL), pl.BlockSpec((D_MODEL, E_C), lambda i: (0, 0)),
                  _row_spec(tm, E_C), _full_spec((G_C, lm, lm)), _full_spec((lm, G_C)),
                  _full_spec((E_C, D_MODEL)), _row_spec(tm, D_MODEL), _mod_spec(grp)],
        out_specs=_row_spec(tm, D_MODEL),
        out_shape=jax.ShapeDtypeStruct((grp.rows, D_MODEL), F32),
        scratch_shapes=[pltpu.VMEM((tm, E_C), F32), pltpu.VMEM((tm, E_C), BF16)],
        compiler_params=_cparams("parallel"),
        name="gmlp_out",
    )(hn, w_in, vb, ws, bs, wo, x, gate)


def _spatial_weights(w_s, b_s, grp):
    length = min(grp.t, CHUNK)
    ws = jnp.tril(w_s[:, :length, :length])
    bs = b_s[:, :length]
    if length < CHUNK:
        reps = grp.tm // length
        ws = jnp.einsum('ab,gij->gaibj', jnp.eye(reps, dtype=ws.dtype), ws).reshape(
            G_C, reps * length, reps * length)
        bs = jnp.tile(bs, (1, reps))
    return ws.astype(BF16), bs.T


def _moe_kernel(a_ref, logit_ref, tri_ref, wg_ref, wu_ref, wd_ref, x_ref, g_ref, ng_ref, nsc_ref, nsh_ref,
                o_ref, hn_ref, gate_scr, rank_scr, gate_t_scr, rank_t_scr, acc_scr, *, tc):
    e = pl.program_id(1)
    lane = lax.broadcasted_iota(jnp.int32, (1, LANES), 1)

    @pl.when(e == 0)
    def _():
        logits = logit_ref[...]
        m1 = jnp.max(logits, axis=1, keepdims=True)
        i1 = jnp.min(jnp.where(logits == m1, lane, LANES), axis=1, keepdims=True)
        rest = jnp.where(lane == i1, -jnp.inf, logits)
        m2 = jnp.max(rest, axis=1, keepdims=True)
        i2 = jnp.min(jnp.where(rest == m2, lane, LANES), axis=1, keepdims=True)
        e2 = jnp.exp(m2 - m1)
        gate = jnp.where(lane == i1, 1.0 / (1.0 + e2), jnp.where(lane == i2, e2 / (1.0 + e2), 0.0))
        picked = gate != 0.0
        rank = _dot(tri_ref[...], jnp.where(picked, 1.0, 0.0).astype(BF16))
        rank = jnp.where(picked, rank, -1.0)
        gate_scr[...] = gate
        rank_scr[...] = rank
        gate_t_scr[...] = gate.T
        rank_t_scr[...] = rank.T
        acc_scr[...] = jnp.zeros(acc_scr.shape, F32)

    gate_col = jnp.sum(jnp.where(lane == e, gate_scr[...], 0.0), axis=1, keepdims=True)
    rank_col = jnp.sum(jnp.where(lane == e, rank_scr[...], 0.0), axis=1, keepdims=True)
    gate_row = gate_t_scr[pl.ds(e, 1), :]
    rank_row = rank_t_scr[pl.ds(e, 1), :]
    count = jnp.sum(jnp.where(rank_row >= 0.0, 1, 0))
    slot_col = lax.broadcasted_iota(jnp.int32, (tc, 1), 0).astype(F32)
    slot_row = lax.broadcasted_iota(jnp.int32, (1, tc), 1).astype(F32)

    def chunk(c, carry):
        base = (c * tc).astype(F32)
        take = rank_row == base + slot_col
        xg = _dot(jnp.where(take, 1.0, 0.0).astype(BF16), a_ref[...]).astype(BF16)
        h = (_silu(_dot(xg, wg_ref[...])) * _dot(xg, wu_ref[...])).astype(BF16)
        y = _dot(h, wd_ref[...])
        gate_c = jnp.sum(jnp.where(take, gate_row, 0.0), axis=1, keepdims=True)
        put = jnp.where(rank_col == base + slot_row, 1.0, 0.0).astype(BF16)
        acc_scr[...] += _dot(put, (gate_c * y).astype(BF16))
        return carry

    lax.fori_loop(0, (count + tc - 1) // tc, chunk, 0)

    @pl.when(e == N_EXPERTS - 1)
    def _():
        x = x_ref[...] + g_ref[...] * acc_scr[...]
        o_ref[...] = x
        hn_ref[...] = _norm_mod(x, ng_ref[...], nsc_ref[...], nsh_ref[...]).astype(hn_ref.dtype)


def _moe_call(hn, logits, wg, wu, wd, j, x, gate, norm, grp):
    tm = grp.tm
    tps = grp.tiles_per_seq
    tc = min(MOE_CHUNK, tm)
    ng, nsc, nsh, hn_dtype = norm
    mod = pl.BlockSpec((None, grp.mod_rows, D_MODEL), lambda i, e: (i // tps, 0, 0))
    tri = jnp.tril(jnp.ones((tm, tm), F32), -1).astype(BF16)
    row = lambda cols: pl.BlockSpec((tm, cols), lambda i, e: (i, 0))
    return pl.pallas_call(
        functools.partial(_moe_kernel, tc=tc),
        grid=(grp.tiles, N_EXPERTS),
        in_specs=[row(D_MODEL), row(LANES), pl.BlockSpec((tm, tm), lambda i, e: (0, 0)),
                  pl.BlockSpec((None, None, D_MODEL, D_FF_E), lambda i, e: (j, e, 0, 0)),
                  pl.BlockSpec((None, None, D_MODEL, D_FF_E), lambda i, e: (j, e, 0, 0)),
                  pl.BlockSpec((None, None, D_FF_E, D_MODEL), lambda i, e: (j, e, 0, 0)),
                  row(D_MODEL), mod,
                  pl.BlockSpec((1, D_MODEL), lambda i, e: (0, 0)), mod, mod],
        out_specs=[row(D_MODEL), row(D_MODEL)],
        out_shape=[jax.ShapeDtypeStruct((grp.rows, D_MODEL), F32),
                   jax.ShapeDtypeStruct((grp.rows, D_MODEL), hn_dtype)],
        scratch_shapes=[pltpu.VMEM((tm, LANES), F32), pltpu.VMEM((tm, LANES), F32),
                        pltpu.VMEM((LANES, tm), F32), pltpu.VMEM((LANES, tm), F32),
                        pltpu.VMEM((tm, D_MODEL), F32)],
        compiler_params=_cparams("parallel", "arbitrary"),
        name="moe_experts",
    )(hn, logits, tri, wg, wu, wd, x, gate, ng[None, :], nsc, nsh)


def _forward(x, mods, p, wb, grp, cache):
    n, t = grp.n, grp.t
    seg = jnp.kron(jnp.eye(H_B, dtype=F32), jnp.ones((N_B, N_B), F32)).astype(BF16)
    slopes = jnp.exp2(-8.0 * jnp.arange(1, H_A + 1, dtype=F32) / H_A)
    new_k, new_v, new_wkv, new_shift, new_cv = [], [], [], [], []
    mods = [[_expand_mod(m, grp) for m in layer] for layer in mods]

    def next_norm(i):
        if i + 1 < DEPTH:
            sh, sc = mods[i + 1][0], mods[i + 1][1]
            return p['norm1_g'][i + 1], sc, sh, BF16
        zero = jnp.zeros_like(mods[i][0])
        return p['final_g'], zero, zero, F32

    hn = _norm_mod_call(x, p['norm1_g'][0], mods[0][1], mods[0][0], grp)
    for i in range(DEPTH):
        j = i // 2
        sh1, sc1, g1, sh2, sc2, g2 = mods[i]
        if i % 2 == 0:
            q, k, v, kb, vb, pb = _inproj_even_call(hn, wb['w_in_e'][j], grp)
            lam_init = 0.8 - 0.6 * math.exp(-0.3 * i)
            lq = jnp.stack([p['lam_q1'][j], p['lam_q2'][j]])
            lk = jnp.stack([p['lam_k1'][j], p['lam_k2'][j]])
            if cache is None:
                g2h = jnp.tile(p['subln_g'][j], 2)[None, :]
                oa = _attn_prompt_call(q, kb, vb, slopes, lq, lk, g2h, n, t, lam_init)
                wkv0 = jnp.zeros((n, H_B, N_B, N_B), F32)
                last = pb.reshape(grp.tiles, grp.tm, B_COLS)[:, -1, :]
                first = jnp.concatenate([jnp.zeros((1, B_COLS), F32), last[:-1]], axis=0)
                starts = (jnp.arange(grp.tiles) % grp.tiles_per_seq == 0)[:, None]
                first = jnp.where(starts, 0.0, first)[:, None, :]
            else:
                cache_k, cache_v, page_table, state_wkv, state_shift = cache
                oa = _attn_sample_call(q, kb, vb, cache_k, cache_v, page_table, j, lq, lk,
                                       p['subln_g'][j][None, :], n, t, lam_init)
                wkv0 = state_wkv[j]
                first = jnp.repeat(state_shift[j], t, axis=0)[None]
            r, w, kv, vv, kk, b, bonus, g = _rwkv_prep_call(pb, first, p, j, seg, grp)
            y, s_fin = _rwkv_scan_call(kk, w, b, kv, r, vv, wkv0, n, t)
            ob = _rwkv_post_call(y, bonus, g, p['lnx_g'][j], p['lnx_b'][j], seg, grp)
            new_k.append(k.reshape(n, t, H_A, 2 * DK_A))
            new_v.append(v.reshape(n, t, H_A, DV_A))
            new_wkv.append(s_fin)
            new_shift.append(pb.reshape(n, t, B_COLS)[:, -1])
            x, hn = _outproj_even_call(oa.reshape(grp.rows, W_A), ob, wb['w_out_e'][j], x, g1,
                                       (p['norm2_g'][i], sc2, sh2, BF16), grp)
            h = _swiglu_up_call(hn, wb['ffn_gate'][j], wb['ffn_up'][j], grp)
            x, hn = _down_res_call(h, wb['ffn_down'][j], x, g2, next_norm(i), grp)
        else:
            vb, vf = _gmlp_v_call(hn, wb['w_in_o'][j], p['lnv_g'][j], p['lnv_b'][j], grp, cache is not None)
            if vf is not None:
                new_cv.append(vf.reshape(n, t, E_C))
            ws, bs = _spatial_weights(p['w_s'][j], p['b_s'][j], grp)
            x = _gmlp_out_call(hn, wb['w_in_o'][j], vb, ws, bs, wb['w_out_o'][j], x, g1, grp)
            hn, logits = _norm_mod_router_call(x, p['norm2_g'][i], sc2, sh2, p['router_w'][j],
                                               p['router_b'][j], grp)
            moe_grp = grp if grp.mod_rows > 1 else _Group(n, t, min(MOE_TILE, t))
            x, hn = _moe_call(hn, logits, wb['exp_gate'], wb['exp_up'], wb['exp_down'], j, x, g2,
                              next_norm(i), moe_grp)
    return (hn.reshape(n, t, D_MODEL), jnp.stack(new_k), jnp.stack(new_v), jnp.stack(new_wkv),
            jnp.stack(new_shift), jnp.stack(new_cv) if new_cv else None)


_BF16_WEIGHTS = ('w_in_e', 'w_out_e', 'ffn_gate', 'ffn_up', 'ffn_down', 'w_in_o', 'w_out_o',
                 'exp_gate', 'exp_up', 'exp_down')


def _run(x_prompt, x_sample, cache_k, cache_v, state_wkv, state_shift, page_table, c_prompt, c_sample, p):
    n_p, t_p, _ = x_prompt.shape
    n_s, t_s, _ = x_sample.shape
    grp_p = _Group(n_p, t_p, min(ROW_TILE, t_p))
    grp_s = _Group(n_s, t_s, n_s * t_s)
    wb = {name: p[name].astype(BF16) for name in _BF16_WEIGHTS}
    mod = _mod_call(jnp.concatenate([c_prompt, c_sample], axis=0), p['w_mod'], p['b_mod'])
    mods_p = [jnp.split(mod[i, :n_p], 6, axis=-1) for i in range(DEPTH)]
    mods_s = [jnp.split(mod[i, n_p:], 6, axis=-1) for i in range(DEPTH)]
    y_p, k_p, v_p, wkv_p, sh_p, _ = _forward(x_prompt.reshape(n_p * t_p, D_MODEL), mods_p, p, wb, grp_p, None)
    y_s, k_s, v_s, wkv_s, sh_s, cv_s = _forward(
        x_sample.reshape(n_s * t_s, D_MODEL), mods_s, p, wb, grp_s,
        (cache_k, cache_v, page_table, state_wkv, state_shift))
    return (y_p, y_s, k_p, v_p, wkv_p, sh_p, k_s, v_s, wkv_s, sh_s, cv_s)


def kernel(x_prompt, x_sample, cache_k, cache_v, state_wkv, state_shift, page_table, c_prompt, c_sample,
           w_mod, b_mod, norm1_g, norm2_g, final_g, w_in_e, w_out_e, lam_q1, lam_k1, lam_q2, lam_k2,
           subln_g, mu_b, w0, w_up, a0, a_up, g_up, k_k, k_a, r_k, lnx_g, lnx_b, w_in_o, lnv_g, lnv_b,
           w_s, b_s, w_out_o, ffn_gate, ffn_up, ffn_down, router_w, router_b, exp_gate, exp_up, exp_down):
    p = dict(w_mod=w_mod, b_mod=b_mod, norm1_g=norm1_g, norm2_g=norm2_g, final_g=final_g,
             w_in_e=w_in_e, w_out_e=w_out_e, lam_q1=lam_q1, lam_k1=lam_k1, lam_q2=lam_q2,
             lam_k2=lam_k2, subln_g=subln_g, mu_b=mu_b, w0=w0, w_up=w_up, a0=a0, a_up=a_up,
             g_up=g_up, k_k=k_k, k_a=k_a, r_k=r_k, lnx_g=lnx_g, lnx_b=lnx_b, w_in_o=w_in_o,
             lnv_g=lnv_g, lnv_b=lnv_b, w_s=w_s, b_s=b_s, w_out_o=w_out_o, ffn_gate=ffn_gate,
             ffn_up=ffn_up, ffn_down=ffn_down, router_w=router_w, router_b=router_b,
             exp_gate=exp_gate, exp_up=exp_up, exp_down=exp_down)
    return _run(x_prompt, x_sample, cache_k, cache_v, state_wkv, state_shift, page_table,
                c_prompt, c_sample, p)
```
